```python
import jax, jax.numpy as jnp
from jax import lax
import numpy as np

D_MODEL = 2048
BATCH = 2
SEQ = 4096
DEPTH = 1

ATTN_HEADS = 8
KV_HEADS = 2
HEAD_DIM = 128
ATTN_WIDTH = ATTN_HEADS * HEAD_DIM
KV_WIDTH = KV_HEADS * HEAD_DIM
IDX_HEADS = 16
IDX_DIM = 64
TOPK_MAX = 256
Q_BLOCK = 128
RNN_WIDTH = 1024
RNN_BLOCKS = 8
RNN_BLOCK_W = RNN_WIDTH // RNN_BLOCKS
CONV_WIDTH = 4
LRU_C = 8.0
REL_BUCKETS = 32
REL_MAX_DIST = 128
N_GROUPS = 8
EXPERTS_PER_GROUP = 8
N_EXPERTS = N_GROUPS * EXPERTS_PER_GROUP
TOP_K_IN_GROUP = 2
EXPERT_FF = 512
MOE_BLOCK = 128
EPS = 1e-6
IN_SPLITS = (ATTN_WIDTH, KV_WIDTH, KV_WIDTH, IDX_HEADS * IDX_DIM, IDX_DIM, IDX_HEADS,
             RNN_WIDTH, RNN_WIDTH, D_MODEL, D_MODEL)
IN_COLS = sum(IN_SPLITS)

kernel_name = "hybrid_dsa_rglru_hmoe_block"


def rms_norm(x, g):
    xf = x.astype(jnp.float32)
    y = xf * lax.rsqrt(jnp.mean(xf * xf, axis=-1, keepdims=True) + EPS)
    return (y * g.astype(jnp.float32)).astype(x.dtype)


def rel_bucket(n):
    n = jnp.maximum(n, 0)
    max_exact = REL_BUCKETS // 2
    nf = jnp.maximum(n, 1).astype(jnp.float32)
    large = max_exact + (jnp.log(nf / max_exact) / np.float32(np.log(REL_MAX_DIST / max_exact))
                         * (REL_BUCKETS - max_exact)).astype(jnp.int32)
    large = jnp.minimum(large, REL_BUCKETS - 1)
    return jnp.where(n < max_exact, n, large)


def split_cols(proj):
    offs, acc = [], 0
    for w in IN_SPLITS[:-1]:
        acc += w
        offs.append(acc)
    return jnp.split(proj, offs, axis=-1)


def dsa_attention(q, k, v, qi, ki, wi, rel_bias):
    B, S = q.shape[0], q.shape[1]
    topk = min(TOPK_MAX, S // 4)
    rep = ATTN_HEADS // KV_HEADS
    n_blk = S // Q_BLOCK
    kpos = jnp.arange(S, dtype=jnp.int32)
    ki32 = ki.astype(jnp.float32)
    idx_scale = np.float32(IDX_DIM ** -0.5)
    w_scale = np.float32(IDX_HEADS ** -0.5)
    att_scale = np.float32(HEAD_DIM ** -0.5)

    def block(j):
        t0 = j * Q_BLOCK
        qb = lax.dynamic_slice_in_dim(q, t0, Q_BLOCK, axis=1)
        qib = lax.dynamic_slice_in_dim(qi, t0, Q_BLOCK, axis=1)
        wib = lax.dynamic_slice_in_dim(wi, t0, Q_BLOCK, axis=1)
        tpos = t0 + jnp.arange(Q_BLOCK, dtype=jnp.int32)
        dots = jnp.einsum('bqhd,bsd->bqhs', qib.astype(jnp.float32), ki32) * idx_scale
        iscore = jnp.einsum('bqhs,bqh->bqs', jax.nn.relu(dots), wib.astype(jnp.float32) * w_scale)
        causal = kpos[None, :] <= tpos[:, None]
        iscore = jnp.where(causal[None], iscore, -jnp.inf)
        _, sel = lax.top_k(iscore, topk)
        valid = sel <= tpos[None, :, None]
        k_sel = jax.vmap(lambda kk, ii: kk[ii])(k, sel)
        v_sel = jax.vmap(lambda vv, ii: vv[ii])(v, sel)
        qg = qb.reshape(B, Q_BLOCK, KV_HEADS, rep, HEAD_DIM)
        logits = jnp.einsum('bqgrd,bqkgd->bqgrk', qg, k_sel).astype(jnp.float32) * att_scale
        bias = rel_bias[rel_bucket(tpos[None, :, None] - sel)]
        bias = bias.reshape(B, Q_BLOCK, topk, KV_HEADS, rep).transpose(0, 1, 3, 4, 2)
        logits = jnp.where(valid[:, :, None, None, :], logits + bias.astype(jnp.float32), -jnp.inf)
        p = jax.nn.softmax(logits, axis=-1).astype(v.dtype)
        o = jnp.einsum('bqgrk,bqkgd->bqgrd', p, v_sel)
        return o.reshape(B, Q_BLOCK, ATTN_WIDTH)

    outs = lax.map(block, jnp.arange(n_blk, dtype=jnp.int32))
    return outs.transpose(1, 0, 2, 3).reshape(B, S, ATTN_WIDTH)


def rg_lru_branch(xr, gate, conv_w, conv_b, wa, ba, wx, bx, lam):
    B, S, W = xr.shape
    xp = jnp.pad(xr, ((0, 0), (CONV_WIDTH - 1, 0), (0, 0)))
    xc = conv_b
    for j in range(CONV_WIDTH):
        xc = xc + xp[:, j:j + S] * conv_w[j]
    xb = xc.reshape(B, S, RNN_BLOCKS, RNN_BLOCK_W).astype(jnp.float32)
    r = jax.nn.sigmoid(jnp.einsum('bsnc,ncd->bsnd', xb, wa.astype(jnp.float32)) + ba.astype(jnp.float32))
    i = jax.nn.sigmoid(jnp.einsum('bsnc,ncd->bsnd', xb, wx.astype(jnp.float32)) + bx.astype(jnp.float32))
    r = r.reshape(B, S, W)
    i = i.reshape(B, S, W)
    log_a = -LRU_C * r * jax.nn.softplus(-lam.astype(jnp.float32))
    a = jnp.exp(log_a)
    mult = jnp.sqrt(-jnp.expm1(2.0 * log_a))
    bterm = mult * (i * xb.reshape(B, S, W))

    def combine(lhs, rhs):
        a1, b1 = lhs
        a2, b2 = rhs
        return a1 * a2, a2 * b1 + b2

    _, h = lax.associative_scan(combine, (a, bterm), axis=1)
    y = h * jax.nn.gelu(gate.astype(jnp.float32))
    return y.astype(xr.dtype)


def hier_moe(u, w_group, b_group, w_er, b_er, w_gate, w_up, w_down):
    T, D = u.shape
    uf = u.astype(jnp.float32)
    glog = uf @ w_group.astype(jnp.float32) + b_group.astype(jnp.float32)
    p_group = jax.nn.softmax(glog, axis=-1)
    g_sel = jnp.argmax(glog, axis=-1).astype(jnp.int32)
    tok_ids = jnp.arange(T, dtype=jnp.int32)
    p_sel = p_group[tok_ids, g_sel]
    elog = (uf @ w_er.astype(jnp.float32) + b_er.astype(jnp.float32)).reshape(T, N_GROUPS, EXPERTS_PER_GROUP)
    elog_g = elog[tok_ids, g_sel]
    top_v, top_i = lax.top_k(elog_g, TOP_K_IN_GROUP)
    weights = p_sel[:, None] * jax.nn.softmax(top_v, axis=-1)
    expert_id = g_sel[:, None] * EXPERTS_PER_GROUP + top_i.astype(jnp.int32)

    n_slots = T * TOP_K_IN_GROUP
    e = expert_id.reshape(-1)
    tok = jnp.repeat(tok_ids, TOP_K_IN_GROUP)
    gw = weights.reshape(-1)
    order = jnp.argsort(e, stable=True)
    se, stok, sgw = e[order], tok[order], gw[order]
    counts = jnp.zeros((N_EXPERTS,), jnp.int32).at[e].add(1)
    starts = jnp.cumsum(counts) - counts
    padded = ((counts + MOE_BLOCK - 1) // MOE_BLOCK) * MOE_BLOCK
    pends = jnp.cumsum(padded)
    pstarts = pends - padded
    dest = pstarts[se] + (jnp.arange(n_slots, dtype=jnp.int32) - starts[se])
    cap = -(-(n_slots + N_EXPERTS * (MOE_BLOCK - 1)) // MOE_BLOCK) * MOE_BLOCK
    n_blk = cap // MOE_BLOCK
    buf_tok = jnp.zeros((cap,), jnp.int32).at[dest].set(stok)
    buf_w = jnp.zeros((cap,), jnp.float32).at[dest].set(sgw)
    blk_start = jnp.arange(n_blk, dtype=jnp.int32) * MOE_BLOCK
    block_expert = jnp.minimum(jnp.searchsorted(pends, blk_start, side='right'), N_EXPERTS - 1).astype(jnp.int32)
    xb = u[buf_tok].reshape(n_blk, MOE_BLOCK, D)

    def expert_block(args):
        xblk, eid = args
        h = jax.nn.silu(xblk @ w_gate[eid]) * (xblk @ w_up[eid])
        return h @ w_down[eid]

    yb = lax.map(expert_block, (xb, block_expert)).reshape(cap, D)
    yb = yb * buf_w[:, None].astype(yb.dtype)
    return jnp.zeros((T, D), u.dtype).at[buf_tok].add(yb.astype(u.dtype))


def setup_inputs(seed: int = 0) -> dict:
    key = jax.random.key(seed)
    ks = jax.random.split(key, 24)
    L, D = DEPTH, D_MODEL

    def nrm(k, shape, scale):
        return jax.random.normal(k, shape, jnp.float32) * scale

    a0 = jax.random.uniform(ks[9], (L, RNN_WIDTH), jnp.float32, minval=0.9, maxval=0.999)
    return {
        "x": nrm(ks[0], (BATCH, SEQ, D), 1.0),
        "norm1_g": 1.0 + nrm(ks[1], (L, D), 0.02),
        "w_in": nrm(ks[2], (L, D, IN_COLS), D ** -0.5),
        "conv_w": nrm(ks[3], (L, CONV_WIDTH, RNN_WIDTH), CONV_WIDTH ** -0.5),
        "conv_b": nrm(ks[4], (L, RNN_WIDTH), 0.02),
        "lru_wa": nrm(ks[5], (L, RNN_BLOCKS, RNN_BLOCK_W, RNN_BLOCK_W), RNN_BLOCK_W ** -0.5),
        "lru_ba": nrm(ks[6], (L, RNN_BLOCKS, RNN_BLOCK_W), 0.02),
        "lru_wx": nrm(ks[7], (L, RNN_BLOCKS, RNN_BLOCK_W, RNN_BLOCK_W), RNN_BLOCK_W ** -0.5),
        "lru_bx": nrm(ks[8], (L, RNN_BLOCKS, RNN_BLOCK_W), 0.02),
        "lru_lambda": jnp.log(a0) - jnp.log1p(-a0),
        "w_proj_attn": nrm(ks[10], (L, ATTN_WIDTH, D), ATTN_WIDTH ** -0.5),
        "w_proj_rnn": nrm(ks[11], (L, RNN_WIDTH, D), RNN_WIDTH ** -0.5),
        "w_out": nrm(ks[12], (L, D, D), D ** -0.5),
        "rel_bias": nrm(ks[13], (REL_BUCKETS, ATTN_HEADS), 0.5),
        "norm2_g": 1.0 + nrm(ks[14], (L, D), 0.02),
        "w_group": nrm(ks[15], (L, D, N_GROUPS), D ** -0.5),
        "b_group": nrm(ks[16], (L, N_GROUPS), 0.01),
        "w_expert_router": nrm(ks[17], (L, D, N_EXPERTS), D ** -0.5),
        "b_expert_router": nrm(ks[18], (L, N_EXPERTS), 0.01),
        "w_gate": nrm(ks[19], (L, N_EXPERTS, D, EXPERT_FF), D ** -0.5),
        "w_up": nrm(ks[20], (L, N_EXPERTS, D, EXPERT_FF), D ** -0.5),
        "w_down": nrm(ks[21], (L, N_EXPERTS, EXPERT_FF, D), EXPERT_FF ** -0.5),
        "norm_f_g": 1.0 + nrm(ks[22], (D,), 0.02),
    }


def reference(x, norm1_g, w_in, conv_w, conv_b, lru_wa, lru_ba, lru_wx, lru_bx, lru_lambda,
              w_proj_attn, w_proj_rnn, w_out, rel_bias, norm2_g, w_group, b_group,
              w_expert_router, b_expert_router, w_gate, w_up, w_down, norm_f_g):
    B, S, D = x.shape
    for l in range(DEPTH):
        u = rms_norm(x, norm1_g[l])
        proj = u @ w_in[l]
        q, k, v, qi, ki, wi, xr, rgate, ga, gr = split_cols(proj)
        q = q.reshape(B, S, ATTN_HEADS, HEAD_DIM)
        k = k.reshape(B, S, KV_HEADS, HEAD_DIM)
        v = v.reshape(B, S, KV_HEADS, HEAD_DIM)
        qi = qi.reshape(B, S, IDX_HEADS, IDX_DIM)
        attn = dsa_attention(q, k, v, qi, ki, wi, rel_bias)
        rnn = rg_lru_branch(xr, rgate, conv_w[l], conv_b[l], lru_wa[l], lru_ba[l],
                            lru_wx[l], lru_bx[l], lru_lambda[l])
        merged = (jax.nn.sigmoid(ga) * (attn @ w_proj_attn[l])
                  + jax.nn.sigmoid(gr) * (rnn @ w_proj_rnn[l]))
        x = x + merged @ w_out[l]
        u2 = rms_norm(x, norm2_g[l]).reshape(B * S, D)
        y = hier_moe(u2, w_group[l], b_group[l], w_expert_router[l], b_expert_router[l],
                     w_gate[l], w_up[l], w_down[l])
        x = x + y.reshape(B, S, D)
    return rms_norm(x, norm_f_g)
```

```python
import functools

import numpy as np
import jax
import jax.numpy as jnp
from jax import lax
from jax.experimental import pallas as pl
from jax.experimental.pallas import tpu as pltpu

D_MODEL = 2048
ATTN_HEADS = 8
KV_HEADS = 2
HEAD_DIM = 128
ATTN_WIDTH = ATTN_HEADS * HEAD_DIM
KV_WIDTH = KV_HEADS * HEAD_DIM
IDX_HEADS = 16
IDX_DIM = 64
TOPK_MAX = 256
RNN_WIDTH = 1024
RNN_BLOCKS = 8
RNN_BLOCK_W = RNN_WIDTH // RNN_BLOCKS
CONV_WIDTH = 4
LRU_C = 8.0
REL_BUCKETS = 32
REL_MAX_DIST = 128
N_GROUPS = 8
EXPERTS_PER_GROUP = 8
N_EXPERTS = N_GROUPS * EXPERTS_PER_GROUP
TOP_K_IN_GROUP = 2
EXPERT_FF = 512
MOE_BLOCK = 128
MOE_SHIFT = 7
EPS = 1e-6

LANES = 128
QB = 128
KT = 256
NEG_BIG = -1e30
INT_MIN = -(2 ** 31)
F32 = jnp.float32
BF16 = jnp.bfloat16
I32 = jnp.int32

OFF_Q = 0
OFF_K = OFF_Q + ATTN_WIDTH
OFF_V = OFF_K + KV_WIDTH
OFF_QI = OFF_V + KV_WIDTH
OFF_KI = OFF_QI + IDX_HEADS * IDX_DIM
OFF_WI = OFF_KI + IDX_DIM
OFF_XR = OFF_WI + IDX_HEADS
OFF_RG = OFF_XR + RNN_WIDTH
OFF_GA = OFF_RG + RNN_WIDTH
OFF_GR = OFF_GA + D_MODEL
IN_COLS = OFF_GR + D_MODEL
PA_COLS = OFF_KI
PC_MAIN = IN_COLS - OFF_XR
PC_COLS = 6400


def _rel_bucket_np(n):
    n = np.maximum(n, 0)
    max_exact = REL_BUCKETS // 2
    nf = np.maximum(n, 1).astype(np.float32)
    large = max_exact + (np.log(nf / np.float32(max_exact)) / np.float32(np.log(REL_MAX_DIST / max_exact))
                         * np.float32(REL_BUCKETS - max_exact)).astype(np.int32)
    large = np.minimum(large, REL_BUCKETS - 1)
    return np.where(n < max_exact, n, large).astype(np.int32)


def _norm_proj_kernel(x_ref, g_ref, w_ref, o_ref, u_ref):
    @pl.when(pl.program_id(1) == 0)
    def _():
        x = x_ref[...]
        ms = jnp.mean(x * x, axis=-1, keepdims=True)
        u_ref[...] = (x * lax.rsqrt(ms + EPS) * g_ref[...]).astype(BF16)

    o_ref[...] = jnp.dot(u_ref[...], w_ref[...], preferred_element_type=F32).astype(o_ref.dtype)


def _norm_proj(x2, g, w, out_dtype, tm, tn):
    m, d = x2.shape
    n = w.shape[1]
    return pl.pallas_call(
        _norm_proj_kernel,
        grid=(m // tm, n // tn),
        in_specs=[
            pl.BlockSpec((tm, d), lambda i, j: (i, 0)),
            pl.BlockSpec((1, d), lambda i, j: (0, 0)),
            pl.BlockSpec((d, tn), lambda i, j: (0, j)),
        ],
        out_specs=pl.BlockSpec((tm, tn), lambda i, j: (i, j)),
        out_shape=jax.ShapeDtypeStruct((m, n), out_dtype),
        scratch_shapes=[pltpu.VMEM((tm, d), BF16)],
        compiler_params=pltpu.CompilerParams(dimension_semantics=("arbitrary", "arbitrary")),
        name="norm_proj",
    )(x2, g, w)


def _dsa_kernel(relb_ref, qit_ref, wit_ref, ki_ref, k_ref, qt_ref, vt_ref, bkt_ref, o_ref,
                keys_ref, bias_ref, m_ref, l_ref, acc_ref):
    b = pl.program_id(0)
    j = pl.program_id(1)

    @pl.when((b == 0) & (j == 0))
    def _():
        for d in range(2):
            bk = bkt_ref[d]
            for h in range(ATTN_HEADS):
                tile = jnp.zeros((QB, QB), F32)
                for bb in range(REL_BUCKETS):
                    tile = jnp.where(bk == bb, relb_ref[bb, h], tile)
                bias_ref[d, h] = tile

    t0 = j * QB
    ntile = (j + 2) // 2
    q_pos = t0 + lax.broadcasted_iota(I32, (KT, QB), 1)
    row_iota = lax.broadcasted_iota(I32, (KT, QB), 0)

    w_scale = np.float32(IDX_HEADS ** -0.5) * np.float32(IDX_DIM ** -0.5)
    wrow = wit_ref[...] * w_scale

    def score_tile(kt, carry):
        r0 = pl.multiple_of(kt * KT, KT)
        ki_t = ki_ref[pl.ds(r0, KT), :]
        acc = jnp.zeros((KT, QB), F32)
        for c in range(4):
            dots = jnp.dot(ki_t, qit_ref[:, c * 512:(c + 1) * 512], preferred_element_type=F32)
            for hh in range(4):
                h = c * 4 + hh
                acc = acc + jnp.maximum(dots[:, hh * QB:(hh + 1) * QB], 0.0) * wrow[:, h * QB:(h + 1) * QB]
        bits = pltpu.bitcast(acc, I32)
        key = bits ^ ((bits >> 31) & jnp.int32(0x7FFFFFFF))
        key = jnp.where(r0 + row_iota <= q_pos, key, jnp.int32(INT_MIN))
        keys_ref[pl.ds(r0, KT), :] = key
        return carry

    lax.fori_loop(0, ntile, score_tile, 0)

    nchunk = ntile * (KT // 64)

    def count_ge(cand):
        def body(i, acc8):
            r = pl.multiple_of(i * 64, 64)
            blk = keys_ref[pl.ds(r, 64), :]
            c = jnp.where(blk >= cand, jnp.int32(1), jnp.int32(0))
            for rr in range(8):
                acc8 = acc8 + c[rr * 8:(rr + 1) * 8, :]
            return acc8
        acc8 = lax.fori_loop(0, nchunk, body, jnp.zeros((8, QB), I32))
        return jnp.sum(acc8.astype(F32), axis=0, keepdims=True).astype(I32)

    c0 = count_ge(jnp.zeros((1, QB), I32))
    thr0 = jnp.where(c0 >= TOPK_MAX, jnp.int32(0), jnp.int32(INT_MIN))

    def bit_step(i, thr):
        cand = thr | (jnp.int32(1) << (30 - i))
        c = count_ge(cand)
        return jnp.where(c >= TOPK_MAX, cand, thr)

    thr = lax.fori_loop(0, 31, bit_step, thr0)
    thr = jnp.maximum(thr, jnp.int32(INT_MIN + 1))

    att_scale = np.float32(HEAD_DIM ** -0.5)
    rep = ATTN_HEADS // KV_HEADS
    nfar = jnp.maximum((j - 1) // 2, 0)

    for g in range(KV_HEADS):
        m_ref[...] = jnp.full((1, rep * QB), NEG_BIG, F32)
        l_ref[...] = jnp.zeros((1, rep * QB), F32)
        acc_ref[...] = jnp.zeros((HEAD_DIM, rep * QB), F32)
        qtg = qt_ref[g]

        def attend(kt, near):
            r0 = pl.multiple_of(kt * KT, KT)
            k_t = k_ref[pl.ds(r0, KT), g * HEAD_DIM:(g + 1) * HEAD_DIM]
            lg = jnp.dot(k_t, qtg, preferred_element_type=F32) * att_scale
            sel = keys_ref[pl.ds(r0, KT), :] >= thr
            cols = []
            for r in range(rep):
                h = g * rep + r
                far_bias = relb_ref[REL_BUCKETS - 1, h]
                blk = lg[:, r * QB:(r + 1) * QB]
                if near:
                    subs = []
                    for sub in range(KT // QB):
                        dblk = j - (kt * (KT // QB) + sub)
                        bias = jnp.where(dblk == 0, bias_ref[0, h],
                                         jnp.where(dblk == 1, bias_ref[1, h], far_bias))
                        subs.append(blk[sub * QB:(sub + 1) * QB, :] + bias)
                    blk = jnp.concatenate(subs, axis=0)
                else:
                    blk = blk + far_bias
                cols.append(jnp.where(sel, blk, NEG_BIG))
            s = jnp.concatenate(cols, axis=1)
            m_old = m_ref[...]
            m_new = jnp.maximum(m_old, jnp.max(s, axis=0, keepdims=True))
            alpha = jnp.exp(m_old - m_new)
            p = jnp.exp(s - m_new)
            l_ref[...] = alpha * l_ref[...] + jnp.sum(p, axis=0, keepdims=True)
            pv = jnp.dot(vt_ref[g, kt], p.astype(BF16), preferred_element_type=F32)
            acc_ref[...] = acc_ref[...] * alpha + pv
            m_ref[...] = m_new

        def far_body(kt, carry):
            attend(kt, False)
            return carry

        def near_body(kt, carry):
            attend(kt, True)
            return carry

        lax.fori_loop(0, nfar, far_body, 0)
        lax.fori_loop(nfar, ntile, near_body, 0)

        out_t = acc_ref[...] / l_ref[...]
        for r in range(rep):
            h = g * rep + r
            o_ref[:, h * HEAD_DIM:(h + 1) * HEAD_DIM] = out_t[:, r * QB:(r + 1) * QB].T.astype(o_ref.dtype)


def _dsa_attention(rel_bias, qit, wit, ki, k, qt, vt, bkt, batch, seq):
    nblk = seq // QB
    rep = ATTN_HEADS // KV_HEADS
    return pl.pallas_call(
        _dsa_kernel,
        grid=(batch, nblk),
        in_specs=[
            pl.BlockSpec(memory_space=pltpu.SMEM),
            pl.BlockSpec((None, None, IDX_DIM, IDX_HEADS * QB), lambda b, j: (b, j, 0, 0)),
            pl.BlockSpec((None, None, 1, IDX_HEADS * QB), lambda b, j: (b, j, 0, 0)),
            pl.BlockSpec((None, seq, IDX_DIM), lambda b, j: (b, 0, 0)),
            pl.BlockSpec((None, seq, KV_WIDTH), lambda b, j: (b, 0, 0)),
            pl.BlockSpec((None, None, KV_HEADS, HEAD_DIM, rep * QB), lambda b, j: (b, j, 0, 0, 0)),
            pl.BlockSpec((None, KV_HEADS, seq // KT, HEAD_DIM, KT), lambda b, j: (b, 0, 0, 0, 0)),
            pl.BlockSpec((2, QB, QB), lambda b, j: (0, 0, 0)),
        ],
        out_specs=pl.BlockSpec((None, QB, ATTN_WIDTH), lambda b, j: (b, j, 0)),
        out_shape=jax.ShapeDtypeStruct((batch, seq, ATTN_WIDTH), BF16),
        scratch_shapes=[
            pltpu.VMEM((seq, QB), I32),
            pltpu.VMEM((2, ATTN_HEADS, QB, QB), F32),
            pltpu.VMEM((1, rep * QB), F32),
            pltpu.VMEM((1, rep * QB), F32),
            pltpu.VMEM((HEAD_DIM, rep * QB), F32),
        ],
        compiler_params=pltpu.CompilerParams(dimension_semantics=("arbitrary", "arbitrary")),
        name="dsa_attention",
    )(rel_bias, qit, wit, ki, k, qt, vt, bkt)


RNN_TS = 256


def _gelu_tanh(x):
    c = np.float32(np.sqrt(2.0 / np.pi))
    return x * (0.5 * (1.0 + jnp.tanh(c * (x + np.float32(0.044715) * (x * x * x)))))


def _softplus(z):
    return jnp.maximum(z, 0.0) + jnp.log1p(jnp.exp(-jnp.abs(z)))


def _rglru_kernel(xr_ref, gate_ref, cw_ref, cb_ref, wa_ref, ba_ref, wx_ref, bx_ref, lam_ref, o_ref,
                  xext_ref, h_ref):
    i = pl.program_id(1)
    ts = RNN_TS

    @pl.when(i == 0)
    def _():
        xext_ref[0:8, :] = jnp.zeros((8, RNN_WIDTH), F32)
        h_ref[...] = jnp.zeros((1, RNN_WIDTH), F32)

    @pl.when(i > 0)
    def _():
        xext_ref[0:8, :] = xext_ref[ts:ts + 8, :]

    xext_ref[8:8 + ts, :] = xr_ref[...]

    row = lax.broadcasted_iota(I32, (ts, RNN_BLOCK_W), 0)
    for n in range(RNN_BLOCKS):
        cs = slice(n * RNN_BLOCK_W, (n + 1) * RNN_BLOCK_W)
        xc = cb_ref[:, cs]
        for jj in range(CONV_WIDTH):
            off = 8 - (CONV_WIDTH - 1) + jj
            xc = xc + xext_ref[off:off + ts, cs] * cw_ref[jj:jj + 1, cs]
        xcb = xc.astype(BF16)
        r = jax.nn.sigmoid(jnp.dot(xcb, wa_ref[n], preferred_element_type=F32) + ba_ref[n:n + 1, :])
        gi = jax.nn.sigmoid(jnp.dot(xcb, wx_ref[n], preferred_element_type=F32) + bx_ref[n:n + 1, :])
        log_a = (-LRU_C * r) * _softplus(-lam_ref[:, cs])
        a = jnp.exp(log_a)
        mult = jnp.sqrt(1.0 - jnp.exp(2.0 * log_a))
        bt = mult * (gi * xc)
        k = 1
        while k < ts:
            a_s = pltpu.roll(a, k, 0)
            b_s = pltpu.roll(bt, k, 0)
            keep = row >= k
            bt = jnp.where(keep, a * b_s + bt, bt)
            a = jnp.where(keep, a * a_s, a)
            k *= 2
        h = bt + a * h_ref[:, cs]
        h_ref[:, cs] = h[ts - 1:ts, :]
        o_ref[:, cs] = (h * _gelu_tanh(gate_ref[:, cs])).astype(o_ref.dtype)


def _rglru(pc, conv_w, conv_b, wa, ba, wx, bx, lam, batch, seq):
    nts = seq // RNN_TS
    full = lambda shape: pl.BlockSpec(shape, lambda b, i: (0,) * len(shape))
    return pl.pallas_call(
        _rglru_kernel,
        grid=(batch, nts),
        in_specs=[
            pl.BlockSpec((RNN_TS, RNN_WIDTH), lambda b, i: (b * nts + i, 0)),
            pl.BlockSpec((RNN_TS, RNN_WIDTH), lambda b, i: (b * nts + i, 1)),
            full((CONV_WIDTH, RNN_WIDTH)),
            full((1, RNN_WIDTH)),
            full((RNN_BLOCKS, RNN_BLOCK_W, RNN_BLOCK_W)),
            full((RNN_BLOCKS, RNN_BLOCK_W)),
            full((RNN_BLOCKS, RNN_BLOCK_W, RNN_BLOCK_W)),
            full((RNN_BLOCKS, RNN_BLOCK_W)),
            full((1, RNN_WIDTH)),
        ],
        out_specs=pl.BlockSpec((RNN_TS, RNN_WIDTH), lambda b, i: (b * nts + i, 0)),
        out_shape=jax.ShapeDtypeStruct((batch * seq, RNN_WIDTH), BF16),
        scratch_shapes=[pltpu.VMEM((RNN_TS + 8, RNN_WIDTH), F32), pltpu.VMEM((1, RNN_WIDTH), F32)],
        compiler_params=pltpu.CompilerParams(dimension_semantics=("arbitrary", "arbitrary")),
        name="rglru",
    )(pc, pc, conv_w, conv_b, wa, ba, wx, bx, lam)


MERGE_TM = 256


def _merge_kernel(attn_ref, rnn_ref, ga_ref, gr_ref, x_ref, wpa_ref, wpr_ref, wo_ref, g2_ref,
                  wrh_ref, wrl_ref, br_ref, x1_ref, u2_ref, lgt_ref):
    pa = jnp.dot(attn_ref[...], wpa_ref[...], preferred_element_type=F32)
    pr = jnp.dot(rnn_ref[...], wpr_ref[...], preferred_element_type=F32)
    merged = jax.nn.sigmoid(ga_ref[...]) * pa + jax.nn.sigmoid(gr_ref[...]) * pr
    x1 = x_ref[...] + jnp.dot(merged.astype(BF16), wo_ref[...], preferred_element_type=F32)
    x1_ref[...] = x1
    ms = jnp.mean(x1 * x1, axis=-1, keepdims=True)
    u2 = x1 * lax.rsqrt(ms + EPS) * g2_ref[...]
    u2_ref[...] = u2
    hi = u2.astype(BF16)
    lo = (u2 - hi.astype(F32)).astype(BF16)
    lg = (jnp.dot(hi, wrh_ref[...], preferred_element_type=F32)
          + jnp.dot(lo, wrh_ref[...], preferred_element_type=F32)
          + jnp.dot(hi, wrl_ref[...], preferred_element_type=F32)) + br_ref[...]
    lgt_ref[...] = lg.T


def _merge(attn, rnn, pc, x2, wpa, wpr, wo, g2, wrh, wrl, br):
    m = x2.shape[0]
    tm = MERGE_TM
    const = lambda shape: pl.BlockSpec(shape, lambda i: (0,) * len(shape), pipeline_mode=pl.Buffered(1))
    return pl.pallas_call(
        _merge_kernel,
        grid=(m // tm,),
        in_specs=[
            pl.BlockSpec((tm, ATTN_WIDTH), lambda i: (i, 0)),
            pl.BlockSpec((tm, RNN_WIDTH), lambda i: (i, 0)),
            pl.BlockSpec((tm, D_MODEL), lambda i: (i, 1)),
            pl.BlockSpec((tm, D_MODEL), lambda i: (i, 2)),
            pl.BlockSpec((tm, D_MODEL), lambda i: (i, 0)),
            const((ATTN_WIDTH, D_MODEL)),
            const((RNN_WIDTH, D_MODEL)),
            const((D_MODEL, D_MODEL)),
            const((1, D_MODEL)),
            const((D_MODEL, LANES)),
            const((D_MODEL, LANES)),
            const((1, LANES)),
        ],
        out_specs=[
            pl.BlockSpec((tm, D_MODEL), lambda i: (i, 0)),
            pl.BlockSpec((tm, D_MODEL), lambda i: (i, 0)),
            pl.BlockSpec((LANES, tm), lambda i: (0, i)),
        ],
        out_shape=[
            jax.ShapeDtypeStruct((m, D_MODEL), F32),
            jax.ShapeDtypeStruct((m, D_MODEL), F32),
            jax.ShapeDtypeStruct((LANES, m), F32),
        ],
        compiler_params=pltpu.CompilerParams(dimension_semantics=("arbitrary",)),
        name="merge_outproj",
    )(attn, rnn, pc, pc, x2, wpa, wpr, wo, g2, wrh, wrl, br)


ROUTE_CHUNK = 256


def _first_index_of_max(v, ridx, n):
    vmax = jnp.max(v, axis=0, keepdims=True)
    idx = jnp.min(jnp.where(v == vmax, ridx, jnp.int32(n)).astype(F32), axis=0, keepdims=True)
    return vmax, idx.astype(I32)


def _route_kernel(lgt_ref, dest_ref, gw_ref, meta_ref, cum_ref):
    t = lgt_ref.shape[1]
    eg = EXPERTS_PER_GROUP
    ridx8 = lax.broadcasted_iota(I32, (eg, t), 0)
    gl = lgt_ref[0:N_GROUPS, :]
    gmax, g_sel = _first_index_of_max(gl, ridx8, N_GROUPS)
    p_sel = 1.0 / jnp.sum(jnp.exp(gl - gmax), axis=0, keepdims=True)
    el = lgt_ref[N_GROUPS:N_GROUPS + eg, :]
    for g in range(1, N_GROUPS):
        el = jnp.where(g_sel == g, lgt_ref[N_GROUPS + g * eg:N_GROUPS + (g + 1) * eg, :], el)
    v0, i0 = _first_index_of_max(el, ridx8, eg)
    el1 = jnp.where(ridx8 == i0, -jnp.inf, el)
    v1, i1 = _first_index_of_max(el1, ridx8, eg)
    e1 = jnp.exp(v1 - v0)
    den = 1.0 + e1
    gw_ref[0:1, :] = p_sel * (1.0 / den)
    gw_ref[1:2, :] = p_sel * (e1 / den)
    ex0 = g_sel * eg + i0
    ex1 = g_sel * eg + i1

    eidx = lax.broadcasted_iota(I32, (N_EXPERTS, ROUTE_CHUNK), 0)
    ui = lax.broadcasted_iota(I32, (ROUTE_CHUNK, ROUTE_CHUNK), 0)
    uj = lax.broadcasted_iota(I32, (ROUTE_CHUNK, ROUTE_CHUNK), 1)
    upper = jnp.where(ui < uj, 1.0, 0.0).astype(BF16)
    run = jnp.zeros((N_EXPERTS, 1), F32)
    for c in range(t // ROUTE_CHUNK):
        cs = slice(c * ROUTE_CHUNK, (c + 1) * ROUTE_CHUNK)
        hit = jnp.where(eidx == ex0[:, cs], 1.0, jnp.where(eidx == ex1[:, cs], 1.0, 0.0))
        cum_ref[:, cs] = jnp.dot(hit.astype(BF16), upper, preferred_element_type=F32) + run
        run = run + jnp.sum(hit, axis=1, keepdims=True)

    counts = run.astype(I32)
    padded = ((counts + (MOE_BLOCK - 1)) >> MOE_SHIFT) << MOE_SHIFT
    pe = jnp.broadcast_to(padded, (N_EXPERTS, LANES))
    erow = lax.broadcasted_iota(I32, (N_EXPERTS, LANES), 0)
    k = 1
    while k < N_EXPERTS:
        pe = pe + jnp.where(erow >= k, pltpu.roll(pe, k, 0), 0)
        k *= 2
    pends = pe[:, 0:1]
    pstarts = pends - padded

    eidx_t = lax.broadcasted_iota(I32, (N_EXPERTS, t), 0)
    slot = cum_ref[...] + pstarts.astype(F32)
    dest_ref[0:1, :] = jnp.sum(jnp.where(eidx_t == ex0, slot, 0.0), axis=0, keepdims=True).astype(I32)
    dest_ref[1:2, :] = jnp.sum(jnp.where(eidx_t == ex1, slot, 0.0), axis=0, keepdims=True).astype(I32)

    nb = meta_ref.shape[1]
    blk_start = lax.broadcasted_iota(I32, (N_EXPERTS, nb), 1) * MOE_BLOCK
    be = jnp.sum(jnp.where(pends <= blk_start, 1.0, 0.0), axis=0, keepdims=True).astype(I32)
    meta_ref[0:1, :] = jnp.minimum(be, N_EXPERTS - 1)
    meta_ref[1:2, :] = jnp.broadcast_to(pends[N_EXPERTS - 1:N_EXPERTS, :] >> MOE_SHIFT, (1, nb))
    fill_start = pstarts + counts
    fill_len = pends - fill_start
    own = lax.broadcasted_iota(I32, (N_EXPERTS, nb), 0) == lax.broadcasted_iota(I32, (N_EXPERTS, nb), 1)
    meta_ref[2:3, :] = jnp.sum(jnp.where(own, fill_start.astype(F32), 0.0), axis=0, keepdims=True).astype(I32)
    meta_ref[3:4, :] = jnp.sum(jnp.where(own, fill_len.astype(F32), 0.0), axis=0, keepdims=True).astype(I32)


def _route(lgt, nblk_pad):
    t = lgt.shape[1]
    return pl.pallas_call(
        _route_kernel,
        out_shape=[
            jax.ShapeDtypeStruct((2, t), I32),
            jax.ShapeDtypeStruct((2, t), F32),
            jax.ShapeDtypeStruct((4, nblk_pad), I32),
        ],
        scratch_shapes=[pltpu.VMEM((N_EXPERTS, t), F32)],
        name="route",
    )(lgt)


DISP_TOK = 512
DISP_WIN = 32


def _dispatch_kernel(dest_ref, meta_ref, u_ref, xs_ref, zero_ref, sem):
    base = pl.program_id(0) * DISP_TOK
    nblk = xs_ref.shape[0] // MOE_BLOCK

    @pl.when(pl.program_id(0) == 0)
    def _():
        zero_ref[...] = jnp.zeros(zero_ref.shape, zero_ref.dtype)

        def pad_copy(e, r):
            return pltpu.make_async_copy(zero_ref.at[pl.ds(0, 1)], xs_ref.at[pl.ds(meta_ref[2, e] + r, 1)],
                                         sem.at[2])

        def pad_expert(e, carry):
            @pl.when(e < N_EXPERTS)
            def _():
                lax.fori_loop(0, meta_ref[3, e], lambda r, c: (pad_copy(e, r).start(), c)[1], 0)

            @pl.when(e > 0)
            def _():
                lax.fori_loop(0, meta_ref[3, e - 1], lambda r, c: (pad_copy(e - 1, r).wait(), c)[1], 0)
            return carry

        lax.fori_loop(0, N_EXPERTS + 1, pad_expert, 0)

        def tail_copy(bk):
            return pltpu.make_async_copy(zero_ref, xs_ref.at[pl.ds(pl.multiple_of(bk * MOE_BLOCK, MOE_BLOCK),
                                                                   MOE_BLOCK)], sem.at[3])

        lax.fori_loop(meta_ref[1, 0], nblk, lambda bk, c: (tail_copy(bk).start(), c)[1], 0)
        lax.fori_loop(meta_ref[1, 0], nblk, lambda bk, c: (tail_copy(bk).wait(), c)[1], 0)

    def copy(tl, kk):
        d = dest_ref[kk, tl]
        return pltpu.make_async_copy(u_ref.at[pl.ds(base + tl, 1)], xs_ref.at[pl.ds(d, 1)], sem.at[kk])

    def window(w, carry):
        for u in range(DISP_WIN):
            for kk in range(TOP_K_IN_GROUP):
                copy(w * DISP_WIN + u, kk).start()
        for u in range(DISP_WIN):
            for kk in range(TOP_K_IN_GROUP):
                copy(w * DISP_WIN + u, kk).wait()
        return carry

    lax.fori_loop(0, DISP_TOK // DISP_WIN, window, 0)


def _dispatch(dest, meta, u2, cap):
    t, d = u2.shape
    return pl.pallas_call(
        _dispatch_kernel,
        grid=(t // DISP_TOK,),
        in_specs=[
            pl.BlockSpec((TOP_K_IN_GROUP, DISP_TOK), lambda i: (0, i), memory_space=pltpu.SMEM),
            pl.BlockSpec(memory_space=pltpu.SMEM),
            pl.BlockSpec(memory_space=pl.ANY),
        ],
        out_specs=pl.BlockSpec(memory_space=pl.ANY),
        out_shape=jax.ShapeDtypeStruct((cap, d), u2.dtype),
        scratch_shapes=[pltpu.VMEM((MOE_BLOCK, d), u2.dtype), pltpu.SemaphoreType.DMA((TOP_K_IN_GROUP + 2,))],
        compiler_params=pltpu.CompilerParams(dimension_semantics=("arbitrary",), has_side_effects=True),
        name="dispatch",
    )(dest, meta, u2)


def _expert_kernel(be_ref, nact_ref, xs_ref, wg_ref, wu_ref, wd_ref, ys_ref, wgb_ref, wub_ref, wdb_ref):
    i = pl.program_id(0)
    changed = jnp.logical_or(i == 0, be_ref[i] != be_ref[jnp.maximum(i - 1, 0)])

    @pl.when(jnp.logical_and(changed, i < nact_ref[0]))
    def _():
        wgb_ref[...] = wg_ref[...].astype(BF16)
        wub_ref[...] = wu_ref[...].astype(BF16)
        wdb_ref[...] = wd_ref[...].astype(BF16)

    @pl.when(i < nact_ref[0])
    def _():
        xb = xs_ref[...].astype(BF16)
        hg = jnp.dot(xb, wgb_ref[...], preferred_element_type=F32)
        hu = jnp.dot(xb, wub_ref[...], preferred_element_type=F32)
        h = (hg * jax.nn.sigmoid(hg)) * hu
        ys_ref[...] = jnp.dot(h.astype(BF16), wdb_ref[...], preferred_element_type=F32)

    @pl.when(i >= nact_ref[0])
    def _():
        ys_ref[...] = jnp.zeros(ys_ref.shape, ys_ref.dtype)


def _experts(block_expert, nact, xs, w_gate, w_up, w_down):
    cap, d = xs.shape
    nblk = cap // MOE_BLOCK
    row = lambda i, be, na: (jnp.minimum(i, jnp.maximum(na[0] - 1, 0)), 0)
    wsel = lambda i, be, na: (be[jnp.minimum(i, jnp.maximum(na[0] - 1, 0))], 0, 0)
    grid_spec = pltpu.PrefetchScalarGridSpec(
        num_scalar_prefetch=2,
        grid=(nblk,),
        in_specs=[
            pl.BlockSpec((MOE_BLOCK, d), row),
            pl.BlockSpec((None, d, EXPERT_FF), wsel),
            pl.BlockSpec((None, d, EXPERT_FF), wsel),
            pl.BlockSpec((None, EXPERT_FF, d), wsel),
        ],
        out_specs=pl.BlockSpec((MOE_BLOCK, d), lambda i, be, na: (i, 0)),
        scratch_shapes=[
            pltpu.VMEM((d, EXPERT_FF), BF16),
            pltpu.VMEM((d, EXPERT_FF), BF16),
            pltpu.VMEM((EXPERT_FF, d), BF16),
        ],
    )
    return pl.pallas_call(
        _expert_kernel,
        grid_spec=grid_spec,
        out_shape=jax.ShapeDtypeStruct((cap, d), F32),
        compiler_params=pltpu.CompilerParams(dimension_semantics=("arbitrary",)),
        name="experts",
    )(block_expert, nact, xs, w_gate, w_up, w_down)


COMB_TOK = 64


def _combine_kernel(dest_ref, ys_ref, x1_ref, gw_ref, gf_ref, o_ref, buf_ref, sem):
    i = pl.program_id(0)
    n = pl.num_programs(0)

    def copy(step, slot, tl, kk):
        d = dest_ref[kk, step * COMB_TOK + tl]
        return pltpu.make_async_copy(ys_ref.at[pl.ds(d, 1)], buf_ref.at[slot, kk, pl.ds(tl, 1)], sem.at[slot])

    def start_all(step, slot):
        def body(tl, carry):
            for kk in range(TOP_K_IN_GROUP):
                copy(step, slot, tl, kk).start()
            return carry
        lax.fori_loop(0, COMB_TOK, body, 0)

    def wait_all(step, slot):
        def body(tl, carry):
            for kk in range(TOP_K_IN_GROUP):
                copy(step, slot, tl, kk).wait()
            return carry
        lax.fori_loop(0, COMB_TOK, body, 0)

    @pl.when(i == 0)
    def _():
        start_all(0, 0)

    @pl.when(i + 1 < n)
    def _():
        start_all(i + 1, (i + 1) % 2)

    slot = i % 2
    wait_all(i, slot)
    y = gw_ref[:, 0:1] * buf_ref[slot, 0] + gw_ref[:, 1:2] * buf_ref[slot, 1]
    x = x1_ref[...] + y
    ms = jnp.mean(x * x, axis=-1, keepdims=True)
    o_ref[...] = x * lax.rsqrt(ms + EPS) * gf_ref[...]


def _combine(dest, ys, x1, gw_t, gf):
    t, d = x1.shape
    return pl.pallas_call(
        _combine_kernel,
        grid=(t // COMB_TOK,),
        in_specs=[
            pl.BlockSpec(memory_space=pltpu.SMEM),
            pl.BlockSpec(memory_space=pl.ANY),
            pl.BlockSpec((COMB_TOK, d), lambda i: (i, 0)),
            pl.BlockSpec((COMB_TOK, TOP_K_IN_GROUP), lambda i: (i, 0)),
            pl.BlockSpec((1, d), lambda i: (0, 0)),
        ],
        out_specs=pl.BlockSpec((COMB_TOK, d), lambda i: (i, 0)),
        out_shape=jax.ShapeDtypeStruct((t, d), F32),
        scratch_shapes=[
            pltpu.VMEM((2, TOP_K_IN_GROUP, COMB_TOK, d), F32),
            pltpu.SemaphoreType.DMA((2,)),
        ],
        compiler_params=pltpu.CompilerParams(dimension_semantics=("arbitrary",)),
        name="combine",
    )(dest, ys, x1, gw_t, gf)


def kernel(x, norm1_g, w_in, conv_w, conv_b, lru_wa, lru_ba, lru_wx, lru_bx, lru_lambda, w_proj_attn,
           w_proj_rnn, w_out, rel_bias, norm2_g, w_group, b_group, w_expert_router, b_expert_router,
           w_gate, w_up, w_down, norm_f_g):
    batch, seq, d = x.shape
    tokens = batch * seq
    nblk = seq // QB
    rep = ATTN_HEADS // KV_HEADS
    x2 = x.reshape(tokens, d)

    w = w_in[0]
    w_a = w[:, :PA_COLS].astype(BF16)
    w_c = jnp.concatenate(
        [w[:, OFF_XR:], w[:, OFF_KI:OFF_XR], jnp.zeros((d, PC_COLS - PC_MAIN - (OFF_XR - OFF_KI)), w.dtype)],
        axis=1).astype(BF16)
    g1 = norm1_g[0].reshape(1, d)
    pa = _norm_proj(x2, g1, w_a, BF16, 1024, 512)
    pc = _norm_proj(x2, g1, w_c, F32, 1024, 1280)

    q = pa[:, OFF_Q:OFF_K].reshape(batch, nblk, QB, KV_HEADS, rep, HEAD_DIM)
    qt = q.transpose(0, 1, 3, 5, 4, 2).reshape(batch, nblk, KV_HEADS, HEAD_DIM, rep * QB)
    k = pa[:, OFF_K:OFF_V].reshape(batch, seq, KV_WIDTH)
    v = pa[:, OFF_V:OFF_QI].reshape(batch, seq // KT, KT, KV_HEADS, HEAD_DIM)
    vt = v.transpose(0, 3, 1, 4, 2)
    qi = pa[:, OFF_QI:OFF_KI].reshape(batch, nblk, QB, IDX_HEADS, IDX_DIM)
    qit = qi.transpose(0, 1, 4, 3, 2).reshape(batch, nblk, IDX_DIM, IDX_HEADS * QB)
    ki = pc[:, PC_MAIN:PC_MAIN + IDX_DIM].astype(BF16).reshape(batch, seq, IDX_DIM)
    wi = pc[:, PC_MAIN + IDX_DIM:PC_MAIN + IDX_DIM + IDX_HEADS].reshape(batch, nblk, QB, IDX_HEADS)
    wit = wi.transpose(0, 1, 3, 2).reshape(batch, nblk, 1, IDX_HEADS * QB)
    ks = np.arange(QB)[:, None]
    qs = np.arange(QB)[None, :]
    bkt = jnp.asarray(np.stack([_rel_bucket_np(qs - ks + QB * dd) for dd in range(2)]))
    attn = _dsa_attention(rel_bias, qit, wit, ki, k, qt, vt, bkt, batch, seq).reshape(tokens, ATTN_WIDTH)

    rnn = _rglru(pc, conv_w[0], conv_b[0].reshape(1, RNN_WIDTH), lru_wa[0].astype(BF16), lru_ba[0],
                 lru_wx[0].astype(BF16), lru_bx[0], lru_lambda[0].reshape(1, RNN_WIDTH), batch, seq)

    w_r = jnp.concatenate([w_group[0], w_expert_router[0],
                           jnp.zeros((d, LANES - N_GROUPS - N_EXPERTS), F32)], axis=1)
    w_rh = w_r.astype(BF16)
    w_rl = (w_r - w_rh.astype(F32)).astype(BF16)
    b_r = jnp.concatenate([b_group[0], b_expert_router[0],
                           jnp.zeros((LANES - N_GROUPS - N_EXPERTS,), F32)]).reshape(1, LANES)
    x1, u2, lgt = _merge(attn, rnn, pc, x2, w_proj_attn[0].astype(BF16), w_proj_rnn[0].astype(BF16),
                         w_out[0].astype(BF16), norm2_g[0].reshape(1, d), w_rh, w_rl, b_r)

    n_slots = tokens * TOP_K_IN_GROUP
    cap = -(-(n_slots + N_EXPERTS * (MOE_BLOCK - 1)) // MOE_BLOCK) * MOE_BLOCK
    nblk_moe = cap // MOE_BLOCK
    nblk_pad = -(-nblk_moe // LANES) * LANES
    dest, gw, meta = _route(lgt, nblk_pad)

    xs = _dispatch(dest, meta, u2, cap)
    ys = _experts(meta[0, :nblk_moe], meta[1, :1], xs, w_gate[0], w_up[0], w_down[0])
    out = _combine(dest, ys, x1, gw.T, norm_f_g.reshape(1, d))
    return out.reshape(batch, seq, d)
```

```python
import functools

import numpy as np
import jax
import jax.numpy as jnp
from jax import lax
from jax.experimental import pallas as pl
from jax.experimental.pallas import tpu as pltpu

D_MODEL = 2048
ATTN_HEADS = 8
KV_HEADS = 2
HEAD_DIM = 128
ATTN_WIDTH = ATTN_HEADS * HEAD_DIM
KV_WIDTH = KV_HEADS * HEAD_DIM
IDX_HEADS = 16
IDX_DIM = 64
TOPK_MAX = 256
RNN_WIDTH = 1024
RNN_BLOCKS = 8
RNN_BLOCK_W = RNN_WIDTH // RNN_BLOCKS
CONV_WIDTH = 4
LRU_C = 8.0
REL_BUCKETS = 32
REL_MAX_DIST = 128
N_GROUPS = 8
EXPERTS_PER_GROUP = 8
N_EXPERTS = N_GROUPS * EXPERTS_PER_GROUP
TOP_K_IN_GROUP = 2
EXPERT_FF = 512
MOE_BLOCK = 128
MOE_SHIFT = 7
EPS = 1e-6

LANES = 128
QB = 128
KT = 256
NEG_BIG = -1e30
INT_MIN = -(2 ** 31)
F32 = jnp.float32
BF16 = jnp.bfloat16
I32 = jnp.int32

OFF_Q = 0
OFF_K = OFF_Q + ATTN_WIDTH
OFF_V = OFF_K + KV_WIDTH
OFF_QI = OFF_V + KV_WIDTH
OFF_KI = OFF_QI + IDX_HEADS * IDX_DIM
OFF_WI = OFF_KI + IDX_DIM
OFF_XR = OFF_WI + IDX_HEADS
OFF_RG = OFF_XR + RNN_WIDTH
OFF_GA = OFF_RG + RNN_WIDTH
OFF_GR = OFF_GA + D_MODEL
IN_COLS = OFF_GR + D_MODEL
PA_COLS = OFF_KI
PC_MAIN = IN_COLS - OFF_XR
PC_COLS = 6400


def _rel_bucket_np(n):
    n = np.maximum(n, 0)
    max_exact = REL_BUCKETS // 2
    nf = np.maximum(n, 1).astype(np.float32)
    large = max_exact + (np.log(nf / np.float32(max_exact)) / np.float32(np.log(REL_MAX_DIST / max_exact))
                         * np.float32(REL_BUCKETS - max_exact)).astype(np.int32)
    large = np.minimum(large, REL_BUCKETS - 1)
    return np.where(n < max_exact, n, large).astype(np.int32)


def _norm_proj_kernel(x_ref, g_ref, w_ref, o_ref, u_ref):
    @pl.when(pl.program_id(1) == 0)
    def _():
        x = x_ref[...]
        ms = jnp.mean(x * x, axis=-1, keepdims=True)
        u_ref[...] = (x * lax.rsqrt(ms + EPS) * g_ref[...]).astype(BF16)

    o_ref[...] = jnp.dot(u_ref[...], w_ref[...], preferred_element_type=F32).astype(o_ref.dtype)


def _norm_proj(x2, g, w, out_dtype, tm, tn):
    m, d = x2.shape
    n = w.shape[1]
    return pl.pallas_call(
        _norm_proj_kernel,
        grid=(m // tm, n // tn),
        in_specs=[
            pl.BlockSpec((tm, d), lambda i, j: (i, 0)),
            pl.BlockSpec((1, d), lambda i, j: (0, 0)),
            pl.BlockSpec((d, tn), lambda i, j: (0, j)),
        ],
        out_specs=pl.BlockSpec((tm, tn), lambda i, j: (i, j)),
        out_shape=jax.ShapeDtypeStruct((m, n), out_dtype),
        scratch_shapes=[pltpu.VMEM((tm, d), BF16)],
        compiler_params=pltpu.CompilerParams(dimension_semantics=("arbitrary", "arbitrary")),
        name="norm_proj",
    )(x2, g, w)


def _dsa_kernel(relb_ref, qit_ref, wit_ref, ki_ref, k_ref, qt_ref, vt_ref, bkt_ref, o_ref,
                keys_ref, bias_ref, m_ref, l_ref, acc_ref):
    b = pl.program_id(0)
    j = pl.program_id(1)

    @pl.when((b == 0) & (j == 0))
    def _():
        for d in range(2):
            bk = bkt_ref[d]
            for h in range(ATTN_HEADS):
                tile = jnp.zeros((QB, QB), F32)
                for bb in range(REL_BUCKETS):
                    tile = jnp.where(bk == bb, relb_ref[bb, h], tile)
                bias_ref[d, h] = tile

    t0 = j * QB
    ntile = (j + 2) // 2
    q_pos = t0 + lax.broadcasted_iota(I32, (KT, QB), 1)
    row_iota = lax.broadcasted_iota(I32, (KT, QB), 0)

    w_scale = np.float32(IDX_HEADS ** -0.5) * np.float32(IDX_DIM ** -0.5)
    wrow = wit_ref[...] * w_scale

    def score_tile(kt, carry):
        r0 = pl.multiple_of(kt * KT, KT)
        ki_t = ki_ref[pl.ds(r0, KT), :]
        acc = jnp.zeros((KT, QB), F32)
        for c in range(4):
            dots = jnp.dot(ki_t, qit_ref[:, c * 512:(c + 1) * 512], preferred_element_type=F32)
            for hh in range(4):
                h = c * 4 + hh
                acc = acc + jnp.maximum(dots[:, hh * QB:(hh + 1) * QB], 0.0) * wrow[:, h * QB:(h + 1) * QB]
        bits = pltpu.bitcast(acc, I32)
        key = bits ^ ((bits >> 31) & jnp.int32(0x7FFFFFFF))
        key = jnp.where(r0 + row_iota <= q_pos, key, jnp.int32(INT_MIN))
        keys_ref[pl.ds(r0, KT), :] = key
        return carry

    lax.fori_loop(0, ntile, score_tile, 0)

    nchunk = ntile * (KT // 64)

    def count_ge(cand):
        def body(i, acc8):
            r = pl.multiple_of(i * 64, 64)
            blk = keys_ref[pl.ds(r, 64), :]
            c = jnp.where(blk >= cand, jnp.int32(1), jnp.int32(0))
            for rr in range(8):
                acc8 = acc8 + c[rr * 8:(rr + 1) * 8, :]
            return acc8
        acc8 = lax.fori_loop(0, nchunk, body, jnp.zeros((8, QB), I32))
        return jnp.sum(acc8.astype(F32), axis=0, keepdims=True).astype(I32)

    c0 = count_ge(jnp.zeros((1, QB), I32))
    thr0 = jnp.where(c0 >= TOPK_MAX, jnp.int32(0), jnp.int32(INT_MIN))

    def bit_step(i, thr):
        cand = thr | (jnp.int32(1) << (30 - i))
        c = count_ge(cand)
        return jnp.where(c >= TOPK_MAX, cand, thr)

    thr = lax.fori_loop(0, 31, bit_step, thr0)
    thr = jnp.maximum(thr, jnp.int32(INT_MIN + 1))

    att_scale = np.float32(HEAD_DIM ** -0.5)
    rep = ATTN_HEADS // KV_HEADS
    nfar = jnp.maximum((j - 1) // 2, 0)

    for g in range(KV_HEADS):
        m_ref[...] = jnp.full((1, rep * QB), NEG_BIG, F32)
        l_ref[...] = jnp.zeros((1, rep * QB), F32)
        acc_ref[...] = jnp.zeros((HEAD_DIM, rep * QB), F32)
        qtg = qt_ref[g]

        def attend(kt, near):
            r0 = pl.multiple_of(kt * KT, KT)
            k_t = k_ref[pl.ds(r0, KT), g * HEAD_DIM:(g + 1) * HEAD_DIM]
            lg = jnp.dot(k_t, qtg, preferred_element_type=F32) * att_scale
            sel = keys_ref[pl.ds(r0, KT), :] >= thr
            cols = []
            for r in range(rep):
                h = g * rep + r
                far_bias = relb_ref[REL_BUCKETS - 1, h]
                blk = lg[:, r * QB:(r + 1) * QB]
                if near:
                    subs = []
                    for sub in range(KT // QB):
                        dblk = j - (kt * (KT // QB) + sub)
                        bias = jnp.where(dblk == 0, bias_ref[0, h],
                                         jnp.where(dblk == 1, bias_ref[1, h], far_bias))
                        subs.append(blk[sub * QB:(sub + 1) * QB, :] + bias)
                    blk = jnp.concatenate(subs, axis=0)
                else:
                    blk = blk + far_bias
                cols.append(jnp.where(sel, blk, NEG_BIG))
            s = jnp.concatenate(cols, axis=1)
            m_old = m_ref[...]
            m_new = jnp.maximum(m_old, jnp.max(s, axis=0, keepdims=True))
            alpha = jnp.exp(m_old - m_new)
            p = jnp.exp(s - m_new)
            l_ref[...] = alpha * l_ref[...] + jnp.sum(p, axis=0, keepdims=True)
            pv = jnp.dot(vt_ref[g, kt], p.astype(BF16), preferred_element_type=F32)
            acc_ref[...] = acc_ref[...] * alpha + pv
            m_ref[...] = m_new

        def far_body(kt, carry):
            attend(kt, False)
            return carry

        def near_body(kt, carry):
            attend(kt, True)
            return carry

        lax.fori_loop(0, nfar, far_body, 0)
        lax.fori_loop(nfar, ntile, near_body, 0)

        out_t = acc_ref[...] / l_ref[...]
        for r in range(rep):
            h = g * rep + r
            o_ref[:, h * HEAD_DIM:(h + 1) * HEAD_DIM] = out_t[:, r * QB:(r + 1) * QB].T.astype(o_ref.dtype)


def _dsa_attention(rel_bias, qit, wit, ki, k, qt, vt, bkt, batch, seq):
    nblk = seq // QB
    rep = ATTN_HEADS // KV_HEADS
    return pl.pallas_call(
        _dsa_kernel,
        grid=(batch, nblk),
        in_specs=[
            pl.BlockSpec(memory_space=pltpu.SMEM),
            pl.BlockSpec((None, None, IDX_DIM, IDX_HEADS * QB), lambda b, j: (b, j, 0, 0)),
            pl.BlockSpec((None, None, 1, IDX_HEADS * QB), lambda b, j: (b, j, 0, 0)),
            pl.BlockSpec((None, seq, IDX_DIM), lambda b, j: (b, 0, 0)),
            pl.BlockSpec((None, seq, KV_WIDTH), lambda b, j: (b, 0, 0)),
            pl.BlockSpec((None, None, KV_HEADS, HEAD_DIM, rep * QB), lambda b, j: (b, j, 0, 0, 0)),
            pl.BlockSpec((None, KV_HEADS, seq // KT, HEAD_DIM, KT), lambda b, j: (b, 0, 0, 0, 0)),
            pl.BlockSpec((2, QB, QB), lambda b, j: (0, 0, 0)),
        ],
        out_specs=pl.BlockSpec((None, QB, ATTN_WIDTH), lambda b, j: (b, j, 0)),
        out_shape=jax.ShapeDtypeStruct((batch, seq, ATTN_WIDTH), BF16),
        scratch_shapes=[
            pltpu.VMEM((seq, QB), I32),
            pltpu.VMEM((2, ATTN_HEADS, QB, QB), F32),
            pltpu.VMEM((1, rep * QB), F32),
            pltpu.VMEM((1, rep * QB), F32),
            pltpu.VMEM((HEAD_DIM, rep * QB), F32),
        ],
        compiler_params=pltpu.CompilerParams(dimension_semantics=("arbitrary", "arbitrary")),
        name="dsa_attention",
    )(rel_bias, qit, wit, ki, k, qt, vt, bkt)


RNN_TS = 256


def _gelu_tanh(x):
    c = np.float32(np.sqrt(2.0 / np.pi))
    return x * (0.5 * (1.0 + jnp.tanh(c * (x + np.float32(0.044715) * (x * x * x)))))


def _softplus(z):
    return jnp.maximum(z, 0.0) + jnp.log1p(jnp.exp(-jnp.abs(z)))


def _rglru_kernel(xr_ref, gate_ref, cw_ref, cb_ref, wa_ref, ba_ref, wx_ref, bx_ref, lam_ref, o_ref,
                  xext_ref, h_ref):
    i = pl.program_id(1)
    ts = RNN_TS

    @pl.when(i == 0)
    def _():
        xext_ref[0:8, :] = jnp.zeros((8, RNN_WIDTH), F32)
        h_ref[...] = jnp.zeros((1, RNN_WIDTH), F32)

    @pl.when(i > 0)
    def _():
        xext_ref[0:8, :] = xext_ref[ts:ts + 8, :]

    xext_ref[8:8 + ts, :] = xr_ref[...]

    row = lax.broadcasted_iota(I32, (ts, RNN_BLOCK_W), 0)
    for n in range(RNN_BLOCKS):
        cs = slice(n * RNN_BLOCK_W, (n + 1) * RNN_BLOCK_W)
        xc = cb_ref[:, cs]
        for jj in range(CONV_WIDTH):
            off = 8 - (CONV_WIDTH - 1) + jj
            xc = xc + xext_ref[off:off + ts, cs] * cw_ref[jj:jj + 1, cs]
        xcb = xc.astype(BF16)
        r = jax.nn.sigmoid(jnp.dot(xcb, wa_ref[n], preferred_element_type=F32) + ba_ref[n:n + 1, :])
        gi = jax.nn.sigmoid(jnp.dot(xcb, wx_ref[n], preferred_element_type=F32) + bx_ref[n:n + 1, :])
        log_a = (-LRU_C * r) * _softplus(-lam_ref[:, cs])
        a = jnp.exp(log_a)
        mult = jnp.sqrt(1.0 - jnp.exp(2.0 * log_a))
        bt = mult * (gi * xc)
        k = 1
        while k < ts:
            a_s = pltpu.roll(a, k, 0)
            b_s = pltpu.roll(bt, k, 0)
            keep = row >= k
            bt = jnp.where(keep, a * b_s + bt, bt)
            a = jnp.where(keep, a * a_s, a)
            k *= 2
        h = bt + a * h_ref[:, cs]
        h_ref[:, cs] = h[ts - 1:ts, :]
        o_ref[:, cs] = (h * _gelu_tanh(gate_ref[:, cs])).astype(o_ref.dtype)


def _rglru(pc, conv_w, conv_b, wa, ba, wx, bx, lam, batch, seq):
    nts = seq // RNN_TS
    full = lambda shape: pl.BlockSpec(shape, lambda b, i: (0,) * len(shape))
    return pl.pallas_call(
        _rglru_kernel,
        grid=(batch, nts),
        in_specs=[
            pl.BlockSpec((RNN_TS, RNN_WIDTH), lambda b, i: (b * nts + i, 0)),
            pl.BlockSpec((RNN_TS, RNN_WIDTH), lambda b, i: (b * nts + i, 1)),
            full((CONV_WIDTH, RNN_WIDTH)),
            full((1, RNN_WIDTH)),
            full((RNN_BLOCKS, RNN_BLOCK_W, RNN_BLOCK_W)),
            full((RNN_BLOCKS, RNN_BLOCK_W)),
            full((RNN_BLOCKS, RNN_BLOCK_W, RNN_BLOCK_W)),
            full((RNN_BLOCKS, RNN_BLOCK_W)),
            full((1, RNN_WIDTH)),
        ],
        out_specs=pl.BlockSpec((RNN_TS, RNN_WIDTH), lambda b, i: (b * nts + i, 0)),
        out_shape=jax.ShapeDtypeStruct((batch * seq, RNN_WIDTH), BF16),
        scratch_shapes=[pltpu.VMEM((RNN_TS + 8, RNN_WIDTH), F32), pltpu.VMEM((1, RNN_WIDTH), F32)],
        compiler_params=pltpu.CompilerParams(dimension_semantics=("arbitrary", "arbitrary")),
        name="rglru",
    )(pc, pc, conv_w, conv_b, wa, ba, wx, bx, lam)


MERGE_TM = 256


def _merge_kernel(attn_ref, rnn_ref, ga_ref, gr_ref, x_ref, wpa_ref, wpr_ref, wo_ref, g2_ref,
                  wrh_ref, wrl_ref, br_ref, x1_ref, u2_ref, lgt_ref):
    pa = jnp.dot(attn_ref[...], wpa_ref[...], preferred_element_type=F32)
    pr = jnp.dot(rnn_ref[...], wpr_ref[...], preferred_element_type=F32)
    merged = jax.nn.sigmoid(ga_ref[...]) * pa + jax.nn.sigmoid(gr_ref[...]) * pr
    x1 = x_ref[...] + jnp.dot(merged.astype(BF16), wo_ref[...], preferred_element_type=F32)
    x1_ref[...] = x1
    ms = jnp.mean(x1 * x1, axis=-1, keepdims=True)
    u2 = x1 * lax.rsqrt(ms + EPS) * g2_ref[...]
    u2_ref[...] = u2
    hi = u2.astype(BF16)
    lo = (u2 - hi.astype(F32)).astype(BF16)
    lg = (jnp.dot(hi, wrh_ref[...], preferred_element_type=F32)
          + jnp.dot(lo, wrh_ref[...], preferred_element_type=F32)
          + jnp.dot(hi, wrl_ref[...], preferred_element_type=F32)) + br_ref[...]
    lgt_ref[...] = lg.T


def _merge(attn, rnn, pc, x2, wpa, wpr, wo, g2, wrh, wrl, br):
    m = x2.shape[0]
    tm = MERGE_TM
    const = lambda shape: pl.BlockSpec(shape, lambda i: (0,) * len(shape), pipeline_mode=pl.Buffered(1))
    return pl.pallas_call(
        _merge_kernel,
        grid=(m // tm,),
        in_specs=[
            pl.BlockSpec((tm, ATTN_WIDTH), lambda i: (i, 0)),
            pl.BlockSpec((tm, RNN_WIDTH), lambda i: (i, 0)),
            pl.BlockSpec((tm, D_MODEL), lambda i: (i, 1)),
            pl.BlockSpec((tm, D_MODEL), lambda i: (i, 2)),
            pl.BlockSpec((tm, D_MODEL), lambda i: (i, 0)),
            const((ATTN_WIDTH, D_MODEL)),
            const((RNN_WIDTH, D_MODEL)),
            const((D_MODEL, D_MODEL)),
            const((1, D_MODEL)),
            const((D_MODEL, LANES)),
            const((D_MODEL, LANES)),
            const((1, LANES)),
        ],
        out_specs=[
            pl.BlockSpec((tm, D_MODEL), lambda i: (i, 0)),
            pl.BlockSpec((tm, D_MODEL), lambda i: (i, 0)),
            pl.BlockSpec((LANES, tm), lambda i: (0, i)),
        ],
        out_shape=[
            jax.ShapeDtypeStruct((m, D_MODEL), F32),
            jax.ShapeDtypeStruct((m, D_MODEL), F32),
            jax.ShapeDtypeStruct((LANES, m), F32),
        ],
        compiler_params=pltpu.CompilerParams(dimension_semantics=("arbitrary",)),
        name="merge_outproj",
    )(attn, rnn, pc, pc, x2, wpa, wpr, wo, g2, wrh, wrl, br)


ROUTE_CHUNK = 256


def _first_index_of_max(v, ridx, n):
    vmax = jnp.max(v, axis=0, keepdims=True)
    idx = jnp.min(jnp.where(v == vmax, ridx, jnp.int32(n)).astype(F32), axis=0, keepdims=True)
    return vmax, idx.astype(I32)


def _route_kernel(lgt_ref, dest_ref, gw_ref, meta_ref, cum_ref):
    t = lgt_ref.shape[1]
    eg = EXPERTS_PER_GROUP
    ridx8 = lax.broadcasted_iota(I32, (eg, t), 0)
    gl = lgt_ref[0:N_GROUPS, :]
    gmax, g_sel = _first_index_of_max(gl, ridx8, N_GROUPS)
    p_sel = 1.0 / jnp.sum(jnp.exp(gl - gmax), axis=0, keepdims=True)
    el = lgt_ref[N_GROUPS:N_GROUPS + eg, :]
    for g in range(1, N_GROUPS):
        el = jnp.where(g_sel == g, lgt_ref[N_GROUPS + g * eg:N_GROUPS + (g + 1) * eg, :], el)
    v0, i0 = _first_index_of_max(el, ridx8, eg)
    el1 = jnp.where(ridx8 == i0, -jnp.inf, el)
    v1, i1 = _first_index_of_max(el1, ridx8, eg)
    e1 = jnp.exp(v1 - v0)
    den = 1.0 + e1
    gw_ref[0:1, :] = p_sel * (1.0 / den)
    gw_ref[1:2, :] = p_sel * (e1 / den)
    ex0 = g_sel * eg + i0
    ex1 = g_sel * eg + i1

    eidx = lax.broadcasted_iota(I32, (N_EXPERTS, ROUTE_CHUNK), 0)
    ui = lax.broadcasted_iota(I32, (ROUTE_CHUNK, ROUTE_CHUNK), 0)
    uj = lax.broadcasted_iota(I32, (ROUTE_CHUNK, ROUTE_CHUNK), 1)
    upper = jnp.where(ui < uj, 1.0, 0.0).astype(BF16)
    run = jnp.zeros((N_EXPERTS, 1), F32)
    for c in range(t // ROUTE_CHUNK):
        cs = slice(c * ROUTE_CHUNK, (c + 1) * ROUTE_CHUNK)
        hit = jnp.where(eidx == ex0[:, cs], 1.0, jnp.where(eidx == ex1[:, cs], 1.0, 0.0))
        cum_ref[:, cs] = jnp.dot(hit.astype(BF16), upper, preferred_element_type=F32) + run
        run = run + jnp.sum(hit, axis=1, keepdims=True)

    counts = run.astype(I32)
    padded = ((counts + (MOE_BLOCK - 1)) >> MOE_SHIFT) << MOE_SHIFT
    pe = jnp.broadcast_to(padded, (N_EXPERTS, LANES))
    erow = lax.broadcasted_iota(I32, (N_EXPERTS, LANES), 0)
    k = 1
    while k < N_EXPERTS:
        pe = pe + jnp.where(erow >= k, pltpu.roll(pe, k, 0), 0)
        k *= 2
    pends = pe[:, 0:1]
    pstarts = pends - padded

    eidx_t = lax.broadcasted_iota(I32, (N_EXPERTS, t), 0)
    slot = cum_ref[...] + pstarts.astype(F32)
    dest_ref[0:1, :] = jnp.sum(jnp.where(eidx_t == ex0, slot, 0.0), axis=0, keepdims=True).astype(I32)
    dest_ref[1:2, :] = jnp.sum(jnp.where(eidx_t == ex1, slot, 0.0), axis=0, keepdims=True).astype(I32)

    nb = meta_ref.shape[1]
    blk_start = lax.broadcasted_iota(I32, (N_EXPERTS, nb), 1) * MOE_BLOCK
    be = jnp.sum(jnp.where(pends <= blk_start, 1.0, 0.0), axis=0, keepdims=True).astype(I32)
    meta_ref[0:1, :] = jnp.minimum(be, N_EXPERTS - 1)
    meta_ref[1:2, :] = jnp.broadcast_to(pends[N_EXPERTS - 1:N_EXPERTS, :] >> MOE_SHIFT, (1, nb))


def _route(lgt, nblk_pad):
    t = lgt.shape[1]
    return pl.pallas_call(
        _route_kernel,
        out_shape=[
            jax.ShapeDtypeStruct((2, t), I32),
            jax.ShapeDtypeStruct((2, t), F32),
            jax.ShapeDtypeStruct((2, nblk_pad), I32),
        ],
        scratch_shapes=[pltpu.VMEM((N_EXPERTS, t), F32)],
        name="route",
    )(lgt)


def _expert_kernel(be_ref, nact_ref, dest_ref, u_ref, wg_ref, wu_ref, wd_ref, ys_ref,
                   tok_ref, xbuf_ref, wgb_ref, wub_ref, wdb_ref, sem):
    i = pl.program_id(0)
    nact = nact_ref[0]
    ntok = u_ref.shape[0]
    nslot = tok_ref.shape[0]

    def row_copy(blk, slot, r):
        tok = tok_ref[blk * MOE_BLOCK + r]
        return pltpu.make_async_copy(u_ref.at[pl.ds(tok, 1)], xbuf_ref.at[slot, pl.ds(r, 1)], sem.at[slot])

    def start_rows(blk, slot):
        def body(r, carry):
            row_copy(blk, slot, r).start()
            return carry
        lax.fori_loop(0, MOE_BLOCK, body, 0, unroll=8)

    def wait_rows(blk, slot):
        def body(r, carry):
            row_copy(blk, slot, r).wait()
            return carry
        lax.fori_loop(0, MOE_BLOCK, body, 0, unroll=8)

    @pl.when(i == 0)
    def _():
        def clear(p, carry):
            tok_ref[p] = 0
            return carry
        lax.fori_loop(0, nslot, clear, 0, unroll=8)

        def put(t, carry):
            for kk in range(TOP_K_IN_GROUP):
                tok_ref[dest_ref[kk * ntok + t]] = t
            return carry
        lax.fori_loop(0, ntok, put, 0, unroll=8)
        start_rows(0, 0)

    @pl.when(i + 1 < nact)
    def _():
        start_rows(i + 1, (i + 1) % 2)

    changed = jnp.logical_or(i == 0, be_ref[i] != be_ref[jnp.maximum(i - 1, 0)])

    @pl.when(jnp.logical_and(changed, i < nact))
    def _():
        wgb_ref[...] = wg_ref[...].astype(BF16)
        wub_ref[...] = wu_ref[...].astype(BF16)
        wdb_ref[...] = wd_ref[...].astype(BF16)

    @pl.when(i < nact)
    def _():
        slot = i % 2
        wait_rows(i, slot)
        xb = xbuf_ref[slot].astype(BF16)
        hg = jnp.dot(xb, wgb_ref[...], preferred_element_type=F32)
        hu = jnp.dot(xb, wub_ref[...], preferred_element_type=F32)
        h = (hg * jax.nn.sigmoid(hg)) * hu
        ys_ref[...] = jnp.dot(h.astype(BF16), wdb_ref[...], preferred_element_type=F32)

    @pl.when(i >= nact)
    def _():
        ys_ref[...] = jnp.zeros(ys_ref.shape, ys_ref.dtype)


def _experts(block_expert, nact, dest_flat, u2, w_gate, w_up, w_down, cap):
    d = u2.shape[1]
    nblk = cap // MOE_BLOCK
    wsel = lambda i, be, na, de: (be[jnp.minimum(i, jnp.maximum(na[0] - 1, 0))], 0, 0)
    grid_spec = pltpu.PrefetchScalarGridSpec(
        num_scalar_prefetch=3,
        grid=(nblk,),
        in_specs=[
            pl.BlockSpec(memory_space=pl.ANY),
            pl.BlockSpec((None, d, EXPERT_FF), wsel),
            pl.BlockSpec((None, d, EXPERT_FF), wsel),
            pl.BlockSpec((None, EXPERT_FF, d), wsel),
        ],
        out_specs=pl.BlockSpec((MOE_BLOCK, d), lambda i, be, na, de: (i, 0)),
        scratch_shapes=[
            pltpu.SMEM((cap,), I32),
            pltpu.VMEM((2, MOE_BLOCK, d), u2.dtype),
            pltpu.VMEM((d, EXPERT_FF), BF16),
            pltpu.VMEM((d, EXPERT_FF), BF16),
            pltpu.VMEM((EXPERT_FF, d), BF16),
            pltpu.SemaphoreType.DMA((2,)),
        ],
    )
    return pl.pallas_call(
        _expert_kernel,
        grid_spec=grid_spec,
        out_shape=jax.ShapeDtypeStruct((cap, d), F32),
        compiler_params=pltpu.CompilerParams(dimension_semantics=("arbitrary",)),
        name="experts",
    )(block_expert, nact, dest_flat, u2, w_gate, w_up, w_down)


COMB_TOK = 64


def _combine_kernel(dest_ref, ys_ref, x1_ref, gw_ref, gf_ref, o_ref, buf_ref, sem):
    i = pl.program_id(0)
    n = pl.num_programs(0)
    ntok = n * COMB_TOK

    def copy(step, slot, tl, kk):
        d = dest_ref[kk * ntok + step * COMB_TOK + tl]
        return pltpu.make_async_copy(ys_ref.at[pl.ds(d, 1)], buf_ref.at[slot, kk, pl.ds(tl, 1)], sem.at[slot])

    def start_all(step, slot):
        def body(tl, carry):
            for kk in range(TOP_K_IN_GROUP):
                copy(step, slot, tl, kk).start()
            return carry
        lax.fori_loop(0, COMB_TOK, body, 0)

    def wait_all(step, slot):
        def body(tl, carry):
            for kk in range(TOP_K_IN_GROUP):
                copy(step, slot, tl, kk).wait()
            return carry
        lax.fori_loop(0, COMB_TOK, body, 0)

    @pl.when(i == 0)
    def _():
        start_all(0, 0)

    @pl.when(i + 1 < n)
    def _():
        start_all(i + 1, (i + 1) % 2)

    slot = i % 2
    wait_all(i, slot)
    y = gw_ref[:, 0:1] * buf_ref[slot, 0] + gw_ref[:, 1:2] * buf_ref[slot, 1]
    x = x1_ref[...] + y
    ms = jnp.mean(x * x, axis=-1, keepdims=True)
    o_ref[...] = x * lax.rsqrt(ms + EPS) * gf_ref[...]


def _combine(dest, ys, x1, gw_t, gf):
    t, d = x1.shape
    return pl.pallas_call(
        _combine_kernel,
        grid=(t // COMB_TOK,),
        in_specs=[
            pl.BlockSpec(memory_space=pltpu.SMEM),
            pl.BlockSpec(memory_space=pl.ANY),
            pl.BlockSpec((COMB_TOK, d), lambda i: (i, 0)),
            pl.BlockSpec((COMB_TOK, TOP_K_IN_GROUP), lambda i: (i, 0)),
            pl.BlockSpec((1, d), lambda i: (0, 0)),
        ],
        out_specs=pl.BlockSpec((COMB_TOK, d), lambda i: (i, 0)),
        out_shape=jax.ShapeDtypeStruct((t, d), F32),
        scratch_shapes=[
            pltpu.VMEM((2, TOP_K_IN_GROUP, COMB_TOK, d), F32),
            pltpu.SemaphoreType.DMA((2,)),
        ],
        compiler_params=pltpu.CompilerParams(dimension_semantics=("arbitrary",)),
        name="combine",
    )(dest, ys, x1, gw_t, gf)


def kernel(x, norm1_g, w_in, conv_w, conv_b, lru_wa, lru_ba, lru_wx, lru_bx, lru_lambda, w_proj_attn,
           w_proj_rnn, w_out, rel_bias, norm2_g, w_group, b_group, w_expert_router, b_expert_router,
           w_gate, w_up, w_down, norm_f_g):
    batch, seq, d = x.shape
    tokens = batch * seq
    nblk = seq // QB
    rep = ATTN_HEADS // KV_HEADS
    x2 = x.reshape(tokens, d)

    w = w_in[0]
    w_a = w[:, :PA_COLS].astype(BF16)
    w_c = jnp.concatenate(
        [w[:, OFF_XR:], w[:, OFF_KI:OFF_XR], jnp.zeros((d, PC_COLS - PC_MAIN - (OFF_XR - OFF_KI)), w.dtype)],
        axis=1).astype(BF16)
    g1 = norm1_g[0].reshape(1, d)
    pa = _norm_proj(x2, g1, w_a, BF16, 1024, 512)
    pc = _norm_proj(x2, g1, w_c, F32, 1024, 1280)

    q = pa[:, OFF_Q:OFF_K].reshape(batch, nblk, QB, KV_HEADS, rep, HEAD_DIM)
    qt = q.transpose(0, 1, 3, 5, 4, 2).reshape(batch, nblk, KV_HEADS, HEAD_DIM, rep * QB)
    k = pa[:, OFF_K:OFF_V].reshape(batch, seq, KV_WIDTH)
    v = pa[:, OFF_V:OFF_QI].reshape(batch, seq // KT, KT, KV_HEADS, HEAD_DIM)
    vt = v.transpose(0, 3, 1, 4, 2)
    qi = pa[:, OFF_QI:OFF_KI].reshape(batch, nblk, QB, IDX_HEADS, IDX_DIM)
    qit = qi.transpose(0, 1, 4, 3, 2).reshape(batch, nblk, IDX_DIM, IDX_HEADS * QB)
    ki = pc[:, PC_MAIN:PC_MAIN + IDX_DIM].astype(BF16).reshape(batch, seq, IDX_DIM)
    wi = pc[:, PC_MAIN + IDX_DIM:PC_MAIN + IDX_DIM + IDX_HEADS].reshape(batch, nblk, QB, IDX_HEADS)
    wit = wi.transpose(0, 1, 3, 2).reshape(batch, nblk, 1, IDX_HEADS * QB)
    ks = np.arange(QB)[:, None]
    qs = np.arange(QB)[None, :]
    bkt = jnp.asarray(np.stack([_rel_bucket_np(qs - ks + QB * dd) for dd in range(2)]))
    attn = _dsa_attention(rel_bias, qit, wit, ki, k, qt, vt, bkt, batch, seq).reshape(tokens, ATTN_WIDTH)

    rnn = _rglru(pc, conv_w[0], conv_b[0].reshape(1, RNN_WIDTH), lru_wa[0].astype(BF16), lru_ba[0],
                 lru_wx[0].astype(BF16), lru_bx[0], lru_lambda[0].reshape(1, RNN_WIDTH), batch, seq)

    w_r = jnp.concatenate([w_group[0], w_expert_router[0],
                           jnp.zeros((d, LANES - N_GROUPS - N_EXPERTS), F32)], axis=1)
    w_rh = w_r.astype(BF16)
    w_rl = (w_r - w_rh.astype(F32)).astype(BF16)
    b_r = jnp.concatenate([b_group[0], b_expert_router[0],
                           jnp.zeros((LANES - N_GROUPS - N_EXPERTS,), F32)]).reshape(1, LANES)
    x1, u2, lgt = _merge(attn, rnn, pc, x2, w_proj_attn[0].astype(BF16), w_proj_rnn[0].astype(BF16),
                         w_out[0].astype(BF16), norm2_g[0].reshape(1, d), w_rh, w_rl, b_r)

    n_slots = tokens * TOP_K_IN_GROUP
    cap = -(-(n_slots + N_EXPERTS * (MOE_BLOCK - 1)) // MOE_BLOCK) * MOE_BLOCK
    nblk_moe = cap // MOE_BLOCK
    nblk_pad = -(-nblk_moe // LANES) * LANES
    dest, gw, meta = _route(lgt, nblk_pad)

    dest_flat = dest.reshape(-1)
    ys = _experts(meta[0, :nblk_moe], meta[1, :1], dest_flat, u2, w_gate[0], w_up[0], w_down[0], cap)
    out = _combine(dest_flat, ys, x1, gw.T, norm_f_g.reshape(1, d))
    return out.reshape(batch, seq, d)
```

```python
import functools

import numpy as np
import jax
import jax.numpy as jnp
from jax import lax
from jax.experimental import pallas as pl
from jax.experimental.pallas import tpu as pltpu

D_MODEL = 2048
ATTN_HEADS = 8
KV_HEADS = 2
HEAD_DIM = 128
ATTN_WIDTH = ATTN_HEADS * HEAD_DIM
KV_WIDTH = KV_HEADS * HEAD_DIM
IDX_HEADS = 16
IDX_DIM = 64
TOPK_MAX = 256
RNN_WIDTH = 1024
RNN_BLOCKS = 8
RNN_BLOCK_W = RNN_WIDTH // RNN_BLOCKS
CONV_WIDTH = 4
LRU_C = 8.0
REL_BUCKETS = 32
REL_MAX_DIST = 128
N_GROUPS = 8
EXPERTS_PER_GROUP = 8
N_EXPERTS = N_GROUPS * EXPERTS_PER_GROUP
TOP_K_IN_GROUP = 2
EXPERT_FF = 512
MOE_BLOCK = 128
MOE_SHIFT = 7
EPS = 1e-6

LANES = 128
QB = 128
KT = 256
NEG_BIG = -1e30
INT_MIN = -(2 ** 31)
LOG2E = np.float32(np.log2(np.e))
F32 = jnp.float32
BF16 = jnp.bfloat16
I32 = jnp.int32

OFF_Q = 0
OFF_K = OFF_Q + ATTN_WIDTH
OFF_V = OFF_K + KV_WIDTH
OFF_QI = OFF_V + KV_WIDTH
OFF_KI = OFF_QI + IDX_HEADS * IDX_DIM
OFF_WI = OFF_KI + IDX_DIM
OFF_XR = OFF_WI + IDX_HEADS
OFF_RG = OFF_XR + RNN_WIDTH
OFF_GA = OFF_RG + RNN_WIDTH
OFF_GR = OFF_GA + D_MODEL
IN_COLS = OFF_GR + D_MODEL
PA_COLS = OFF_KI
PC_MAIN = IN_COLS - OFF_XR
PC_COLS = 6400


def _rel_bucket_np(n):
    n = np.maximum(n, 0)
    max_exact = REL_BUCKETS // 2
    nf = np.maximum(n, 1).astype(np.float32)
    large = max_exact + (np.log(nf / np.float32(max_exact)) / np.float32(np.log(REL_MAX_DIST / max_exact))
                         * np.float32(REL_BUCKETS - max_exact)).astype(np.int32)
    large = np.minimum(large, REL_BUCKETS - 1)
    return np.where(n < max_exact, n, large).astype(np.int32)


def _norm_proj_kernel(x_ref, g_ref, w_ref, o_ref, u_ref):
    @pl.when(pl.program_id(1) == 0)
    def _():
        x = x_ref[...]
        ms = jnp.mean(x * x, axis=-1, keepdims=True)
        u_ref[...] = (x * lax.rsqrt(ms + EPS) * g_ref[...]).astype(BF16)

    o_ref[...] = jnp.dot(u_ref[...], w_ref[...], preferred_element_type=F32).astype(o_ref.dtype)


def _norm_proj(x2, g, w, out_dtype, tm, tn):
    m, d = x2.shape
    n = w.shape[1]
    return pl.pallas_call(
        _norm_proj_kernel,
        grid=(m // tm, n // tn),
        in_specs=[
            pl.BlockSpec((tm, d), lambda i, j: (i, 0)),
            pl.BlockSpec((1, d), lambda i, j: (0, 0)),
            pl.BlockSpec((d, tn), lambda i, j: (0, j)),
        ],
        out_specs=pl.BlockSpec((tm, tn), lambda i, j: (i, j)),
        out_shape=jax.ShapeDtypeStruct((m, n), out_dtype),
        scratch_shapes=[pltpu.VMEM((tm, d), BF16)],
        compiler_params=pltpu.CompilerParams(dimension_semantics=("arbitrary", "arbitrary")),
        name="norm_proj",
    )(x2, g, w)


def _dsa_kernel(relb_ref, qit_ref, wit_ref, ki_ref, k_ref, qt_ref, vt_ref, bkt_ref, o_ref,
                keys_ref, bias_ref, m_ref, l_ref, acc_ref):
    b = pl.program_id(0)
    j = pl.program_id(1)

    @pl.when((b == 0) & (j == 0))
    def _():
        for d in range(2):
            bk = bkt_ref[d]
            for h in range(ATTN_HEADS):
                tile = jnp.zeros((QB, QB), F32)
                for bb in range(REL_BUCKETS):
                    tile = jnp.where(bk == bb, relb_ref[bb, h] * LOG2E, tile)
                bias_ref[d, h] = tile

    t0 = j * QB
    ntile = (j + 2) // 2
    q_pos = t0 + lax.broadcasted_iota(I32, (KT, QB), 1)
    row_iota = lax.broadcasted_iota(I32, (KT, QB), 0)

    w_scale = np.float32(IDX_HEADS ** -0.5) * np.float32(IDX_DIM ** -0.5)
    wrow = wit_ref[...] * w_scale

    def score_tile(kt, carry):
        r0 = pl.multiple_of(kt * KT, KT)
        ki_t = ki_ref[pl.ds(r0, KT), :]
        acc = jnp.zeros((KT, QB), F32)
        for c in range(4):
            dots = jnp.dot(ki_t, qit_ref[:, c * 512:(c + 1) * 512], preferred_element_type=F32)
            for hh in range(4):
                h = c * 4 + hh
                acc = acc + jnp.maximum(dots[:, hh * QB:(hh + 1) * QB], 0.0) * wrow[:, h * QB:(h + 1) * QB]
        bits = pltpu.bitcast(acc, I32)
        key = bits ^ ((bits >> 31) & jnp.int32(0x7FFFFFFF))
        key = jnp.where(r0 + row_iota <= q_pos, key, jnp.int32(INT_MIN))
        keys_ref[pl.ds(r0, KT), :] = key
        return carry

    lax.fori_loop(0, ntile, score_tile, 0)

    n_acc = 4

    def count_ge(cand):
        cand8 = jnp.broadcast_to(cand, (8, QB))

        def body(kt, accs):
            r0 = pl.multiple_of(kt * KT, KT)
            accs = list(accs)
            for v in range(KT // 8):
                blk = keys_ref[pl.ds(r0 + v * 8, 8), :]
                accs[v % n_acc] = accs[v % n_acc] + jnp.where(blk >= cand8, jnp.int32(1), jnp.int32(0))
            return tuple(accs)

        accs = lax.fori_loop(0, ntile, body, (jnp.zeros((8, QB), I32),) * n_acc)
        acc8 = (accs[0] + accs[1]) + (accs[2] + accs[3])
        return jnp.sum(acc8.astype(F32), axis=0, keepdims=True).astype(I32)

    c0 = count_ge(jnp.zeros((1, QB), I32))
    thr0 = jnp.where(c0 >= TOPK_MAX, jnp.int32(0), jnp.int32(INT_MIN))

    def bit_step(i, thr):
        cand = thr | (jnp.int32(1) << (30 - i))
        c = count_ge(cand)
        return jnp.where(c >= TOPK_MAX, cand, thr)

    thr = lax.fori_loop(0, 31, bit_step, thr0)
    thr = jnp.maximum(thr, jnp.int32(INT_MIN + 1))

    qk_scale = np.float32(HEAD_DIM ** -0.5) * LOG2E
    rep = ATTN_HEADS // KV_HEADS
    nfar = jnp.maximum((j - 1) // 2, 0)

    m_ref[...] = jnp.full(m_ref.shape, NEG_BIG, F32)
    l_ref[...] = jnp.zeros(l_ref.shape, F32)
    acc_ref[...] = jnp.zeros(acc_ref.shape, F32)

    def attend(kt, g, near):
        r0 = pl.multiple_of(kt * KT, KT)
        k_t = k_ref[pl.ds(r0, KT), g * HEAD_DIM:(g + 1) * HEAD_DIM]
        lg = jnp.dot(k_t, qt_ref[g], preferred_element_type=F32) * qk_scale
        sel = keys_ref[pl.ds(r0, KT), :] >= thr
        cols = []
        for r in range(rep):
            h = g * rep + r
            far_bias = relb_ref[REL_BUCKETS - 1, h] * LOG2E
            blk = lg[:, r * QB:(r + 1) * QB]
            if near:
                subs = []
                for sub in range(KT // QB):
                    dblk = j - (kt * (KT // QB) + sub)
                    bias = jnp.where(dblk == 0, bias_ref[0, h],
                                     jnp.where(dblk == 1, bias_ref[1, h], far_bias))
                    subs.append(blk[sub * QB:(sub + 1) * QB, :] + bias)
                blk = jnp.concatenate(subs, axis=0)
            else:
                blk = blk + far_bias
            cols.append(jnp.where(sel, blk, NEG_BIG))
        s = jnp.concatenate(cols, axis=1)
        m_old = m_ref[g]
        m_new = jnp.maximum(m_old, jnp.max(s, axis=0, keepdims=True))
        alpha = jnp.exp2(m_old - m_new)
        p = jnp.exp2(s - m_new)
        l_ref[g] = alpha * l_ref[g] + jnp.sum(p, axis=0, keepdims=True)
        pv = jnp.dot(vt_ref[g, kt], p.astype(BF16), preferred_element_type=F32)
        acc_ref[g] = acc_ref[g] * alpha + pv
        m_ref[g] = m_new

    def far_body(kt, carry):
        for g in range(KV_HEADS):
            attend(kt, g, False)
        return carry

    def near_body(kt, carry):
        for g in range(KV_HEADS):
            attend(kt, g, True)
        return carry

    lax.fori_loop(0, nfar, far_body, 0)
    lax.fori_loop(nfar, ntile, near_body, 0)

    for g in range(KV_HEADS):
        out_t = acc_ref[g] / l_ref[g]
        for r in range(rep):
            h = g * rep + r
            o_ref[:, h * HEAD_DIM:(h + 1) * HEAD_DIM] = out_t[:, r * QB:(r + 1) * QB].T.astype(o_ref.dtype)


def _dsa_attention(rel_bias, qit, wit, ki, k, qt, vt, bkt, batch, seq):
    nblk = seq // QB
    rep = ATTN_HEADS // KV_HEADS
    return pl.pallas_call(
        _dsa_kernel,
        grid=(batch, nblk),
        in_specs=[
            pl.BlockSpec(memory_space=pltpu.SMEM),
            pl.BlockSpec((None, None, IDX_DIM, IDX_HEADS * QB), lambda b, j: (b, j, 0, 0)),
            pl.BlockSpec((None, None, 1, IDX_HEADS * QB), lambda b, j: (b, j, 0, 0)),
            pl.BlockSpec((None, seq, IDX_DIM), lambda b, j: (b, 0, 0)),
            pl.BlockSpec((None, seq, KV_WIDTH), lambda b, j: (b, 0, 0)),
            pl.BlockSpec((None, None, KV_HEADS, HEAD_DIM, rep * QB), lambda b, j: (b, j, 0, 0, 0)),
            pl.BlockSpec((None, KV_HEADS, seq // KT, HEAD_DIM, KT), lambda b, j: (b, 0, 0, 0, 0)),
            pl.BlockSpec((2, QB, QB), lambda b, j: (0, 0, 0)),
        ],
        out_specs=pl.BlockSpec((None, QB, ATTN_WIDTH), lambda b, j: (b, j, 0)),
        out_shape=jax.ShapeDtypeStruct((batch, seq, ATTN_WIDTH), BF16),
        scratch_shapes=[
            pltpu.VMEM((seq, QB), I32),
            pltpu.VMEM((2, ATTN_HEADS, QB, QB), F32),
            pltpu.VMEM((KV_HEADS, 1, rep * QB), F32),
            pltpu.VMEM((KV_HEADS, 1, rep * QB), F32),
            pltpu.VMEM((KV_HEADS, HEAD_DIM, rep * QB), F32),
        ],
        compiler_params=pltpu.CompilerParams(dimension_semantics=("arbitrary", "arbitrary")),
        name="dsa_attention",
    )(rel_bias, qit, wit, ki, k, qt, vt, bkt)


RNN_TS = 256


def _gelu_tanh(x):
    c = np.float32(np.sqrt(2.0 / np.pi))
    return x * (0.5 * (1.0 + jnp.tanh(c * (x + np.float32(0.044715) * (x * x * x)))))


def _softplus(z):
    return jnp.maximum(z, 0.0) + jnp.log1p(jnp.exp(-jnp.abs(z)))


def _rglru_kernel(xr_ref, gate_ref, cw_ref, cb_ref, wa_ref, ba_ref, wx_ref, bx_ref, lam_ref, o_ref,
                  xext_ref, h_ref):
    i = pl.program_id(1)
    ts = RNN_TS

    @pl.when(i == 0)
    def _():
        xext_ref[0:8, :] = jnp.zeros((8, RNN_WIDTH), F32)
        h_ref[...] = jnp.zeros((1, RNN_WIDTH), F32)

    @pl.when(i > 0)
    def _():
        xext_ref[0:8, :] = xext_ref[ts:ts + 8, :]

    xext_ref[8:8 + ts, :] = xr_ref[...]

    row = lax.broadcasted_iota(I32, (ts, RNN_BLOCK_W), 0)
    for n in range(RNN_BLOCKS):
        cs = slice(n * RNN_BLOCK_W, (n + 1) * RNN_BLOCK_W)
        xc = cb_ref[:, cs]
        for jj in range(CONV_WIDTH):
            off = 8 - (CONV_WIDTH - 1) + jj
            xc = xc + xext_ref[off:off + ts, cs] * cw_ref[jj:jj + 1, cs]
        xcb = xc.astype(BF16)
        r = jax.nn.sigmoid(jnp.dot(xcb, wa_ref[n], preferred_element_type=F32) + ba_ref[n:n + 1, :])
        gi = jax.nn.sigmoid(jnp.dot(xcb, wx_ref[n], preferred_element_type=F32) + bx_ref[n:n + 1, :])
        log_a = (-LRU_C * r) * _softplus(-lam_ref[:, cs])
        a = jnp.exp(log_a)
        mult = jnp.sqrt(1.0 - jnp.exp(2.0 * log_a))
        bt = mult * (gi * xc)
        k = 1
        while k < ts:
            a_s = pltpu.roll(a, k, 0)
            b_s = pltpu.roll(bt, k, 0)
            keep = row >= k
            bt = jnp.where(keep, a * b_s + bt, bt)
            a = jnp.where(keep, a * a_s, a)
            k *= 2
        h = bt + a * h_ref[:, cs]
        h_ref[:, cs] = h[ts - 1:ts, :]
        o_ref[:, cs] = (h * _gelu_tanh(gate_ref[:, cs])).astype(o_ref.dtype)


def _rglru(pc, conv_w, conv_b, wa, ba, wx, bx, lam, batch, seq):
    nts = seq // RNN_TS
    full = lambda shape: pl.BlockSpec(shape, lambda b, i: (0,) * len(shape))
    return pl.pallas_call(
        _rglru_kernel,
        grid=(batch, nts),
        in_specs=[
            pl.BlockSpec((RNN_TS, RNN_WIDTH), lambda b, i: (b * nts + i, 0)),
            pl.BlockSpec((RNN_TS, RNN_WIDTH), lambda b, i: (b * nts + i, 1)),
            full((CONV_WIDTH, RNN_WIDTH)),
            full((1, RNN_WIDTH)),
            full((RNN_BLOCKS, RNN_BLOCK_W, RNN_BLOCK_W)),
            full((RNN_BLOCKS, RNN_BLOCK_W)),
            full((RNN_BLOCKS, RNN_BLOCK_W, RNN_BLOCK_W)),
            full((RNN_BLOCKS, RNN_BLOCK_W)),
            full((1, RNN_WIDTH)),
        ],
        out_specs=pl.BlockSpec((RNN_TS, RNN_WIDTH), lambda b, i: (b * nts + i, 0)),
        out_shape=jax.ShapeDtypeStruct((batch * seq, RNN_WIDTH), BF16),
        scratch_shapes=[pltpu.VMEM((RNN_TS + 8, RNN_WIDTH), F32), pltpu.VMEM((1, RNN_WIDTH), F32)],
        compiler_params=pltpu.CompilerParams(dimension_semantics=("arbitrary", "arbitrary")),
        name="rglru",
    )(pc, pc, conv_w, conv_b, wa, ba, wx, bx, lam)


MERGE_TM = 256


def _merge_kernel(attn_ref, rnn_ref, ga_ref, gr_ref, x_ref, wpa_ref, wpr_ref, wo_ref, g2_ref,
                  wrh_ref, wrl_ref, br_ref, x1_ref, u2_ref, lgt_ref):
    pa = jnp.dot(attn_ref[...], wpa_ref[...], preferred_element_type=F32)
    pr = jnp.dot(rnn_ref[...], wpr_ref[...], preferred_element_type=F32)
    merged = jax.nn.sigmoid(ga_ref[...]) * pa + jax.nn.sigmoid(gr_ref[...]) * pr
    x1 = x_ref[...] + jnp.dot(merged.astype(BF16), wo_ref[...], preferred_element_type=F32)
    x1_ref[...] = x1
    ms = jnp.mean(x1 * x1, axis=-1, keepdims=True)
    u2 = x1 * lax.rsqrt(ms + EPS) * g2_ref[...]
    u2_ref[...] = u2
    hi = u2.astype(BF16)
    lo = (u2 - hi.astype(F32)).astype(BF16)
    lg = (jnp.dot(hi, wrh_ref[...], preferred_element_type=F32)
          + jnp.dot(lo, wrh_ref[...], preferred_element_type=F32)
          + jnp.dot(hi, wrl_ref[...], preferred_element_type=F32)) + br_ref[...]
    lgt_ref[...] = lg.T


def _merge(attn, rnn, pc, x2, wpa, wpr, wo, g2, wrh, wrl, br):
    m = x2.shape[0]
    tm = MERGE_TM
    const = lambda shape: pl.BlockSpec(shape, lambda i: (0,) * len(shape), pipeline_mode=pl.Buffered(1))
    return pl.pallas_call(
        _merge_kernel,
        grid=(m // tm,),
        in_specs=[
            pl.BlockSpec((tm, ATTN_WIDTH), lambda i: (i, 0)),
            pl.BlockSpec((tm, RNN_WIDTH), lambda i: (i, 0)),
            pl.BlockSpec((tm, D_MODEL), lambda i: (i, 1)),
            pl.BlockSpec((tm, D_MODEL), lambda i: (i, 2)),
            pl.BlockSpec((tm, D_MODEL), lambda i: (i, 0)),
            const((ATTN_WIDTH, D_MODEL)),
            const((RNN_WIDTH, D_MODEL)),
            const((D_MODEL, D_MODEL)),
            const((1, D_MODEL)),
            const((D_MODEL, LANES)),
            const((D_MODEL, LANES)),
            const((1, LANES)),
        ],
        out_specs=[
            pl.BlockSpec((tm, D_MODEL), lambda i: (i, 0)),
            pl.BlockSpec((tm, D_MODEL), lambda i: (i, 0)),
            pl.BlockSpec((LANES, tm), lambda i: (0, i)),
        ],
        out_shape=[
            jax.ShapeDtypeStruct((m, D_MODEL), F32),
            jax.ShapeDtypeStruct((m, D_MODEL), F32),
            jax.ShapeDtypeStruct((LANES, m), F32),
        ],
        compiler_params=pltpu.CompilerParams(dimension_semantics=("arbitrary",)),
        name="merge_outproj",
    )(attn, rnn, pc, pc, x2, wpa, wpr, wo, g2, wrh, wrl, br)


ROUTE_CHUNK = 256


def _first_index_of_max(v, ridx, n):
    vmax = jnp.max(v, axis=0, keepdims=True)
    idx = jnp.min(jnp.where(v == vmax, ridx, jnp.int32(n)).astype(F32), axis=0, keepdims=True)
    return vmax, idx.astype(I32)


def _route_kernel(lgt_ref, dest_ref, gw_ref, meta_ref, cum_ref):
    t = lgt_ref.shape[1]
    eg = EXPERTS_PER_GROUP
    ridx8 = lax.broadcasted_iota(I32, (eg, t), 0)
    gl = lgt_ref[0:N_GROUPS, :]
    gmax, g_sel = _first_index_of_max(gl, ridx8, N_GROUPS)
    p_sel = 1.0 / jnp.sum(jnp.exp(gl - gmax), axis=0, keepdims=True)
    el = lgt_ref[N_GROUPS:N_GROUPS + eg, :]
    for g in range(1, N_GROUPS):
        el = jnp.where(g_sel == g, lgt_ref[N_GROUPS + g * eg:N_GROUPS + (g + 1) * eg, :], el)
    v0, i0 = _first_index_of_max(el, ridx8, eg)
    el1 = jnp.where(ridx8 == i0, -jnp.inf, el)
    v1, i1 = _first_index_of_max(el1, ridx8, eg)
    e1 = jnp.exp(v1 - v0)
    den = 1.0 + e1
    gw_ref[0:1, :] = p_sel * (1.0 / den)
    gw_ref[1:2, :] = p_sel * (e1 / den)
    ex0 = g_sel * eg + i0
    ex1 = g_sel * eg + i1

    eidx = lax.broadcasted_iota(I32, (N_EXPERTS, ROUTE_CHUNK), 0)
    ui = lax.broadcasted_iota(I32, (ROUTE_CHUNK, ROUTE_CHUNK), 0)
    uj = lax.broadcasted_iota(I32, (ROUTE_CHUNK, ROUTE_CHUNK), 1)
    upper = jnp.where(ui < uj, 1.0, 0.0).astype(BF16)
    run = jnp.zeros((N_EXPERTS, 1), F32)
    for c in range(t // ROUTE_CHUNK):
        cs = slice(c * ROUTE_CHUNK, (c + 1) * ROUTE_CHUNK)
        hit = jnp.where(eidx == ex0[:, cs], 1.0, jnp.where(eidx == ex1[:, cs], 1.0, 0.0))
        cum_ref[:, cs] = jnp.dot(hit.astype(BF16), upper, preferred_element_type=F32) + run
        run = run + jnp.sum(hit, axis=1, keepdims=True)

    counts = run.astype(I32)
    padded = ((counts + (MOE_BLOCK - 1)) >> MOE_SHIFT) << MOE_SHIFT
    pe = jnp.broadcast_to(padded, (N_EXPERTS, LANES))
    erow = lax.broadcasted_iota(I32, (N_EXPERTS, LANES), 0)
    k = 1
    while k < N_EXPERTS:
        pe = pe + jnp.where(erow >= k, pltpu.roll(pe, k, 0), 0)
        k *= 2
    pends = pe[:, 0:1]
    pstarts = pends - padded

    eidx_t = lax.broadcasted_iota(I32, (N_EXPERTS, t), 0)
    slot = cum_ref[...] + pstarts.astype(F32)
    dest_ref[0:1, :] = jnp.sum(jnp.where(eidx_t == ex0, slot, 0.0), axis=0, keepdims=True).astype(I32)
    dest_ref[1:2, :] = jnp.sum(jnp.where(eidx_t == ex1, slot, 0.0), axis=0, keepdims=True).astype(I32)

    nb = meta_ref.shape[1]
    blk_start = lax.broadcasted_iota(I32, (N_EXPERTS, nb), 1) * MOE_BLOCK
    be = jnp.sum(jnp.where(pends <= blk_start, 1.0, 0.0), axis=0, keepdims=True).astype(I32)
    meta_ref[0:1, :] = jnp.minimum(be, N_EXPERTS - 1)
    meta_ref[1:2, :] = jnp.broadcast_to(pends[N_EXPERTS - 1:N_EXPERTS, :] >> MOE_SHIFT, (1, nb))


def _route(lgt, nblk_pad):
    t = lgt.shape[1]
    return pl.pallas_call(
        _route_kernel,
        out_shape=[
            jax.ShapeDtypeStruct((2, t), I32),
            jax.ShapeDtypeStruct((2, t), F32),
            jax.ShapeDtypeStruct((2, nblk_pad), I32),
        ],
        scratch_shapes=[pltpu.VMEM((N_EXPERTS, t), F32)],
        name="route",
    )(lgt)


def _expert_kernel(be_ref, nact_ref, dest_ref, u_ref, wg_ref, wu_ref, wd_ref, ys_ref,
                   tok_ref, xbuf_ref, wgb_ref, wub_ref, wdb_ref, sem):
    i = pl.program_id(0)
    nact = nact_ref[0]
    ntok = u_ref.shape[0]
    nslot = tok_ref.shape[0]

    def row_copy(blk, slot, r):
        tok = tok_ref[blk * MOE_BLOCK + r]
        return pltpu.make_async_copy(u_ref.at[pl.ds(tok, 1)], xbuf_ref.at[slot, pl.ds(r, 1)], sem.at[slot])

    def start_rows(blk, slot):
        def body(r, carry):
            row_copy(blk, slot, r).start()
            return carry
        lax.fori_loop(0, MOE_BLOCK, body, 0, unroll=8)

    def wait_rows(blk, slot):
        def body(r, carry):
            row_copy(blk, slot, r).wait()
            return carry
        lax.fori_loop(0, MOE_BLOCK, body, 0, unroll=8)

    @pl.when(i == 0)
    def _():
        def clear(p, carry):
            tok_ref[p] = 0
            return carry
        lax.fori_loop(0, nslot, clear, 0, unroll=8)

        def put(t, carry):
            for kk in range(TOP_K_IN_GROUP):
                tok_ref[dest_ref[kk * ntok + t]] = t
            return carry
        lax.fori_loop(0, ntok, put, 0, unroll=8)
        start_rows(0, 0)

    @pl.when(i + 1 < nact)
    def _():
        start_rows(i + 1, (i + 1) % 2)

    changed = jnp.logical_or(i == 0, be_ref[i] != be_ref[jnp.maximum(i - 1, 0)])

    @pl.when(jnp.logical_and(changed, i < nact))
    def _():
        wgb_ref[...] = wg_ref[...].astype(BF16)
        wub_ref[...] = wu_ref[...].astype(BF16)
        wdb_ref[...] = wd_ref[...].astype(BF16)

    @pl.when(i < nact)
    def _():
        slot = i % 2
        wait_rows(i, slot)
        xb = xbuf_ref[slot].astype(BF16)
        hg = jnp.dot(xb, wgb_ref[...], preferred_element_type=F32)
        hu = jnp.dot(xb, wub_ref[...], preferred_element_type=F32)
        h = (hg * jax.nn.sigmoid(hg)) * hu
        ys_ref[...] = jnp.dot(h.astype(BF16), wdb_ref[...], preferred_element_type=F32)

    @pl.when(i >= nact)
    def _():
        ys_ref[...] = jnp.zeros(ys_ref.shape, ys_ref.dtype)


def _experts(block_expert, nact, dest_flat, u2, w_gate, w_up, w_down, cap):
    d = u2.shape[1]
    nblk = cap // MOE_BLOCK
    wsel = lambda i, be, na, de: (be[jnp.minimum(i, jnp.maximum(na[0] - 1, 0))], 0, 0)
    grid_spec = pltpu.PrefetchScalarGridSpec(
        num_scalar_prefetch=3,
        grid=(nblk,),
        in_specs=[
            pl.BlockSpec(memory_space=pl.ANY),
            pl.BlockSpec((None, d, EXPERT_FF), wsel),
            pl.BlockSpec((None, d, EXPERT_FF), wsel),
            pl.BlockSpec((None, EXPERT_FF, d), wsel),
        ],
        out_specs=pl.BlockSpec((MOE_BLOCK, d), lambda i, be, na, de: (i, 0)),
        scratch_shapes=[
            pltpu.SMEM((cap,), I32),
            pltpu.VMEM((2, MOE_BLOCK, d), u2.dtype),
            pltpu.VMEM((d, EXPERT_FF), BF16),
            pltpu.VMEM((d, EXPERT_FF), BF16),
            pltpu.VMEM((EXPERT_FF, d), BF16),
            pltpu.SemaphoreType.DMA((2,)),
        ],
    )
    return pl.pallas_call(
        _expert_kernel,
        grid_spec=grid_spec,
        out_shape=jax.ShapeDtypeStruct((cap, d), F32),
        compiler_params=pltpu.CompilerParams(dimension_semantics=("arbitrary",)),
        name="experts",
    )(block_expert, nact, dest_flat, u2, w_gate, w_up, w_down)


COMB_TOK = 64


def _combine_kernel(dest_ref, ys_ref, x1_ref, gw_ref, gf_ref, o_ref, buf_ref, sem):
    i = pl.program_id(0)
    n = pl.num_programs(0)
    ntok = n * COMB_TOK

    def copy(step, slot, tl, kk):
        d = dest_ref[kk * ntok + step * COMB_TOK + tl]
        return pltpu.make_async_copy(ys_ref.at[pl.ds(d, 1)], buf_ref.at[slot, kk, pl.ds(tl, 1)], sem.at[slot])

    def start_all(step, slot):
        def body(tl, carry):
            for kk in range(TOP_K_IN_GROUP):
                copy(step, slot, tl, kk).start()
            return carry
        lax.fori_loop(0, COMB_TOK, body, 0)

    def wait_all(step, slot):
        def body(tl, carry):
            for kk in range(TOP_K_IN_GROUP):
                copy(step, slot, tl, kk).wait()
            return carry
        lax.fori_loop(0, COMB_TOK, body, 0)

    @pl.when(i == 0)
    def _():
        start_all(0, 0)

    @pl.when(i + 1 < n)
    def _():
        start_all(i + 1, (i + 1) % 2)

    slot = i % 2
    wait_all(i, slot)
    y = gw_ref[:, 0:1] * buf_ref[slot, 0] + gw_ref[:, 1:2] * buf_ref[slot, 1]
    x = x1_ref[...] + y
    ms = jnp.mean(x * x, axis=-1, keepdims=True)
    o_ref[...] = x * lax.rsqrt(ms + EPS) * gf_ref[...]


def _combine(dest, ys, x1, gw_t, gf):
    t, d = x1.shape
    return pl.pallas_call(
        _combine_kernel,
        grid=(t // COMB_TOK,),
        in_specs=[
            pl.BlockSpec(memory_space=pltpu.SMEM),
            pl.BlockSpec(memory_space=pl.ANY),
            pl.BlockSpec((COMB_TOK, d), lambda i: (i, 0)),
            pl.BlockSpec((COMB_TOK, TOP_K_IN_GROUP), lambda i: (i, 0)),
            pl.BlockSpec((1, d), lambda i: (0, 0)),
        ],
        out_specs=pl.BlockSpec((COMB_TOK, d), lambda i: (i, 0)),
        out_shape=jax.ShapeDtypeStruct((t, d), F32),
        scratch_shapes=[
            pltpu.VMEM((2, TOP_K_IN_GROUP, COMB_TOK, d), F32),
            pltpu.SemaphoreType.DMA((2,)),
        ],
        compiler_params=pltpu.CompilerParams(dimension_semantics=("arbitrary",)),
        name="combine",
    )(dest, ys, x1, gw_t, gf)


def kernel(x, norm1_g, w_in, conv_w, conv_b, lru_wa, lru_ba, lru_wx, lru_bx, lru_lambda, w_proj_attn,
           w_proj_rnn, w_out, rel_bias, norm2_g, w_group, b_group, w_expert_router, b_expert_router,
           w_gate, w_up, w_down, norm_f_g):
    batch, seq, d = x.shape
    tokens = batch * seq
    nblk = seq // QB
    rep = ATTN_HEADS // KV_HEADS
    x2 = x.reshape(tokens, d)

    w = w_in[0]
    w_a = w[:, :PA_COLS].astype(BF16)
    w_c = jnp.concatenate(
        [w[:, OFF_XR:], w[:, OFF_KI:OFF_XR], jnp.zeros((d, PC_COLS - PC_MAIN - (OFF_XR - OFF_KI)), w.dtype)],
        axis=1).astype(BF16)
    g1 = norm1_g[0].reshape(1, d)
    pa = _norm_proj(x2, g1, w_a, BF16, 1024, 512)
    pc = _norm_proj(x2, g1, w_c, F32, 1024, 1280)

    q = pa[:, OFF_Q:OFF_K].reshape(batch, nblk, QB, KV_HEADS, rep, HEAD_DIM)
    qt = q.transpose(0, 1, 3, 5, 4, 2).reshape(batch, nblk, KV_HEADS, HEAD_DIM, rep * QB)
    k = pa[:, OFF_K:OFF_V].reshape(batch, seq, KV_WIDTH)
    v = pa[:, OFF_V:OFF_QI].reshape(batch, seq // KT, KT, KV_HEADS, HEAD_DIM)
    vt = v.transpose(0, 3, 1, 4, 2)
    qi = pa[:, OFF_QI:OFF_KI].reshape(batch, nblk, QB, IDX_HEADS, IDX_DIM)
    qit = qi.transpose(0, 1, 4, 3, 2).reshape(batch, nblk, IDX_DIM, IDX_HEADS * QB)
    ki = pc[:, PC_MAIN:PC_MAIN + IDX_DIM].astype(BF16).reshape(batch, seq, IDX_DIM)
    wi = pc[:, PC_MAIN + IDX_DIM:PC_MAIN + IDX_DIM + IDX_HEADS].reshape(batch, nblk, QB, IDX_HEADS)
    wit = wi.transpose(0, 1, 3, 2).reshape(batch, nblk, 1, IDX_HEADS * QB)
    ks = np.arange(QB)[:, None]
    qs = np.arange(QB)[None, :]
    bkt = jnp.asarray(np.stack([_rel_bucket_np(qs - ks + QB * dd) for dd in range(2)]))
    attn = _dsa_attention(rel_bias, qit, wit, ki, k, qt, vt, bkt, batch, seq).reshape(tokens, ATTN_WIDTH)

    rnn = _rglru(pc, conv_w[0], conv_b[0].reshape(1, RNN_WIDTH), lru_wa[0].astype(BF16), lru_ba[0],
                 lru_wx[0].astype(BF16), lru_bx[0], lru_lambda[0].reshape(1, RNN_WIDTH), batch, seq)

    w_r = jnp.concatenate([w_group[0], w_expert_router[0],
                           jnp.zeros((d, LANES - N_GROUPS - N_EXPERTS), F32)], axis=1)
    w_rh = w_r.astype(BF16)
    w_rl = (w_r - w_rh.astype(F32)).astype(BF16)
    b_r = jnp.concatenate([b_group[0], b_expert_router[0],
                           jnp.zeros((LANES - N_GROUPS - N_EXPERTS,), F32)]).reshape(1, LANES)
    x1, u2, lgt = _merge(attn, rnn, pc, x2, w_proj_attn[0].astype(BF16), w_proj_rnn[0].astype(BF16),
                         w_out[0].astype(BF16), norm2_g[0].reshape(1, d), w_rh, w_rl, b_r)

    n_slots = tokens * TOP_K_IN_GROUP
    cap = -(-(n_slots + N_EXPERTS * (MOE_BLOCK - 1)) // MOE_BLOCK) * MOE_BLOCK
    nblk_moe = cap // MOE_BLOCK
    nblk_pad = -(-nblk_moe // LANES) * LANES
    dest, gw, meta = _route(lgt, nblk_pad)

    dest_flat = dest.reshape(-1)
    ys = _experts(meta[0, :nblk_moe], meta[1, :1], dest_flat, u2, w_gate[0], w_up[0], w_down[0], cap)
    out = _combine(dest_flat, ys, x1, gw.T, norm_f_g.reshape(1, d))
    return out.reshape(batch, seq, d)
```

```python
import functools

import numpy as np
import jax
import jax.numpy as jnp
from jax import lax
from jax.experimental import pallas as pl
from jax.experimental.pallas import tpu as pltpu

D_MODEL = 2048
ATTN_HEADS = 8
KV_HEADS = 2
HEAD_DIM = 128
ATTN_WIDTH = ATTN_HEADS * HEAD_DIM
KV_WIDTH = KV_HEADS * HEAD_DIM
IDX_HEADS = 16
IDX_DIM = 64
TOPK_MAX = 256
RNN_WIDTH = 1024
RNN_BLOCKS = 8
RNN_BLOCK_W = RNN_WIDTH // RNN_BLOCKS
CONV_WIDTH = 4
LRU_C = 8.0
REL_BUCKETS = 32
REL_MAX_DIST = 128
N_GROUPS = 8
EXPERTS_PER_GROUP = 8
N_EXPERTS = N_GROUPS * EXPERTS_PER_GROUP
TOP_K_IN_GROUP = 2
EXPERT_FF = 512
MOE_BLOCK = 128
MOE_SHIFT = 7
EPS = 1e-6

LANES = 128
QB = 128
KT = 256
NEG_BIG = -1e30
INT_MIN = -(2 ** 31)
LOG2E = np.float32(np.log2(np.e))
F32 = jnp.float32
BF16 = jnp.bfloat16
I32 = jnp.int32

OFF_Q = 0
OFF_K = OFF_Q + ATTN_WIDTH
OFF_V = OFF_K + KV_WIDTH
OFF_QI = OFF_V + KV_WIDTH
OFF_KI = OFF_QI + IDX_HEADS * IDX_DIM
OFF_WI = OFF_KI + IDX_DIM
OFF_XR = OFF_WI + IDX_HEADS
OFF_RG = OFF_XR + RNN_WIDTH
OFF_GA = OFF_RG + RNN_WIDTH
OFF_GR = OFF_GA + D_MODEL
IN_COLS = OFF_GR + D_MODEL
PA_COLS = OFF_KI
PC_MAIN = IN_COLS - OFF_XR
PC_COLS = 6400


def _rel_bucket_np(n):
    n = np.maximum(n, 0)
    max_exact = REL_BUCKETS // 2
    nf = np.maximum(n, 1).astype(np.float32)
    large = max_exact + (np.log(nf / np.float32(max_exact)) / np.float32(np.log(REL_MAX_DIST / max_exact))
                         * np.float32(REL_BUCKETS - max_exact)).astype(np.int32)
    large = np.minimum(large, REL_BUCKETS - 1)
    return np.where(n < max_exact, n, large).astype(np.int32)


def _norm_proj_kernel(x_ref, g_ref, w_ref, o_ref, u_ref):
    @pl.when(pl.program_id(1) == 0)
    def _():
        x = x_ref[...]
        ms = jnp.mean(x * x, axis=-1, keepdims=True)
        u_ref[...] = (x * lax.rsqrt(ms + EPS) * g_ref[...]).astype(BF16)

    o_ref[...] = jnp.dot(u_ref[...], w_ref[...], preferred_element_type=F32).astype(o_ref.dtype)


def _norm_proj(x2, g, w, out_dtype, tm, tn):
    m, d = x2.shape
    n = w.shape[1]
    return pl.pallas_call(
        _norm_proj_kernel,
        grid=(m // tm, n // tn),
        in_specs=[
            pl.BlockSpec((tm, d), lambda i, j: (i, 0)),
            pl.BlockSpec((1, d), lambda i, j: (0, 0)),
            pl.BlockSpec((d, tn), lambda i, j: (0, j)),
        ],
        out_specs=pl.BlockSpec((tm, tn), lambda i, j: (i, j)),
        out_shape=jax.ShapeDtypeStruct((m, n), out_dtype),
        scratch_shapes=[pltpu.VMEM((tm, d), BF16)],
        compiler_params=pltpu.CompilerParams(dimension_semantics=("arbitrary", "arbitrary")),
        name="norm_proj",
    )(x2, g, w)


def _dsa_kernel(relb_ref, qit_ref, wit_ref, ki_ref, k_ref, qt_ref, vt_ref, bkt_ref, o_ref,
                keys_ref, bias_ref, m_ref, l_ref, acc_ref):
    b = pl.program_id(0)
    j = pl.program_id(1)

    @pl.when((b == 0) & (j == 0))
    def _():
        for d in range(2):
            bk = bkt_ref[d]
            for h in range(ATTN_HEADS):
                tile = jnp.zeros((QB, QB), F32)
                for bb in range(REL_BUCKETS):
                    tile = jnp.where(bk == bb, relb_ref[bb, h] * LOG2E, tile)
                bias_ref[d, h] = tile

    t0 = j * QB
    ntile = (j + 2) // 2
    q_pos = t0 + lax.broadcasted_iota(I32, (KT, QB), 1)
    row_iota = lax.broadcasted_iota(I32, (KT, QB), 0)

    w_scale = np.float32(IDX_HEADS ** -0.5) * np.float32(IDX_DIM ** -0.5)
    wrow = wit_ref[...] * w_scale

    def score_tile(kt, carry):
        r0 = pl.multiple_of(kt * KT, KT)
        ki_t = ki_ref[pl.ds(r0, KT), :]
        acc = jnp.zeros((KT, QB), F32)
        for c in range(4):
            dots = jnp.dot(ki_t, qit_ref[:, c * 512:(c + 1) * 512], preferred_element_type=F32)
            for hh in range(4):
                h = c * 4 + hh
                acc = acc + jnp.maximum(dots[:, hh * QB:(hh + 1) * QB], 0.0) * wrow[:, h * QB:(h + 1) * QB]
        bits = pltpu.bitcast(acc, I32)
        key = bits ^ ((bits >> 31) & jnp.int32(0x7FFFFFFF))
        key = jnp.where(r0 + row_iota <= q_pos, key, jnp.int32(INT_MIN))
        keys_ref[pl.ds(r0, KT), :] = key
        return carry

    lax.fori_loop(0, ntile, score_tile, 0)

    n_acc = 4

    def count_ge(cand):
        cand8 = jnp.broadcast_to(cand, (8, QB))

        def body(kt, accs):
            r0 = pl.multiple_of(kt * KT, KT)
            accs = list(accs)
            for v in range(KT // 8):
                blk = keys_ref[pl.ds(r0 + v * 8, 8), :]
                accs[v % n_acc] = accs[v % n_acc] + jnp.where(blk >= cand8, jnp.int32(1), jnp.int32(0))
            return tuple(accs)

        accs = lax.fori_loop(0, ntile, body, (jnp.zeros((8, QB), I32),) * n_acc)
        acc8 = (accs[0] + accs[1]) + (accs[2] + accs[3])
        return jnp.sum(acc8.astype(F32), axis=0, keepdims=True).astype(I32)

    c0 = count_ge(jnp.zeros((1, QB), I32))
    thr0 = jnp.where(c0 >= TOPK_MAX, jnp.int32(0), jnp.int32(INT_MIN))

    def bit_step(i, thr):
        cand = thr | (jnp.int32(1) << (30 - i))
        c = count_ge(cand)
        return jnp.where(c >= TOPK_MAX, cand, thr)

    thr = lax.fori_loop(0, 31, bit_step, thr0)
    thr = jnp.maximum(thr, jnp.int32(INT_MIN + 1))

    qk_scale = np.float32(HEAD_DIM ** -0.5) * LOG2E
    rep = ATTN_HEADS // KV_HEADS
    nfar = jnp.maximum((j - 1) // 2, 0)

    m_ref[...] = jnp.full(m_ref.shape, NEG_BIG, F32)
    l_ref[...] = jnp.zeros(l_ref.shape, F32)
    acc_ref[...] = jnp.zeros(acc_ref.shape, F32)

    def attend(kt, g, near):
        r0 = pl.multiple_of(kt * KT, KT)
        k_t = k_ref[pl.ds(r0, KT), g * HEAD_DIM:(g + 1) * HEAD_DIM]
        lg = jnp.dot(k_t, qt_ref[g], preferred_element_type=F32) * qk_scale
        sel = keys_ref[pl.ds(r0, KT), :] >= thr
        cols = []
        for r in range(rep):
            h = g * rep + r
            far_bias = relb_ref[REL_BUCKETS - 1, h] * LOG2E
            blk = lg[:, r * QB:(r + 1) * QB]
            if near:
                subs = []
                for sub in range(KT // QB):
                    dblk = j - (kt * (KT // QB) + sub)
                    bias = jnp.where(dblk == 0, bias_ref[0, h],
                                     jnp.where(dblk == 1, bias_ref[1, h], far_bias))
                    subs.append(blk[sub * QB:(sub + 1) * QB, :] + bias)
                blk = jnp.concatenate(subs, axis=0)
            else:
                blk = blk + far_bias
            cols.append(jnp.where(sel, blk, NEG_BIG))
        s = jnp.concatenate(cols, axis=1)
        m_old = m_ref[g]
        m_new = jnp.maximum(m_old, jnp.max(s, axis=0, keepdims=True))
        alpha = jnp.exp2(m_old - m_new)
        p = jnp.exp2(s - m_new)
        l_ref[g] = alpha * l_ref[g] + jnp.sum(p, axis=0, keepdims=True)
        pv = jnp.dot(vt_ref[g, kt], p.astype(BF16), preferred_element_type=F32)
        acc_ref[g] = acc_ref[g] * alpha + pv
        m_ref[g] = m_new

    def far_body(kt, carry):
        for g in range(KV_HEADS):
            attend(kt, g, False)
        return carry

    def near_body(kt, carry):
        for g in range(KV_HEADS):
            attend(kt, g, True)
        return carry

    lax.fori_loop(0, nfar, far_body, 0)
    lax.fori_loop(nfar, ntile, near_body, 0)

    for g in range(KV_HEADS):
        out_t = acc_ref[g] / l_ref[g]
        for r in range(rep):
            h = g * rep + r
            o_ref[:, h * HEAD_DIM:(h + 1) * HEAD_DIM] = out_t[:, r * QB:(r + 1) * QB].T.astype(o_ref.dtype)


def _dsa_attention(rel_bias, qit, wit, ki, k, qt, vt, bkt, batch, seq):
    nblk = seq // QB
    rep = ATTN_HEADS // KV_HEADS
    return pl.pallas_call(
        _dsa_kernel,
        grid=(batch, nblk),
        in_specs=[
            pl.BlockSpec(memory_space=pltpu.SMEM),
            pl.BlockSpec((None, None, IDX_DIM, IDX_HEADS * QB), lambda b, j: (b, j, 0, 0)),
            pl.BlockSpec((None, None, 1, IDX_HEADS * QB), lambda b, j: (b, j, 0, 0)),
            pl.BlockSpec((None, seq, IDX_DIM), lambda b, j: (b, 0, 0)),
            pl.BlockSpec((None, seq, KV_WIDTH), lambda b, j: (b, 0, 0)),
            pl.BlockSpec((None, None, KV_HEADS, HEAD_DIM, rep * QB), lambda b, j: (b, j, 0, 0, 0)),
            pl.BlockSpec((None, KV_HEADS, seq // KT, HEAD_DIM, KT), lambda b, j: (b, 0, 0, 0, 0)),
            pl.BlockSpec((2, QB, QB), lambda b, j: (0, 0, 0)),
        ],
        out_specs=pl.BlockSpec((None, QB, ATTN_WIDTH), lambda b, j: (b, j, 0)),
        out_shape=jax.ShapeDtypeStruct((batch, seq, ATTN_WIDTH), BF16),
        scratch_shapes=[
            pltpu.VMEM((seq, QB), I32),
            pltpu.VMEM((2, ATTN_HEADS, QB, QB), F32),
            pltpu.VMEM((KV_HEADS, 1, rep * QB), F32),
            pltpu.VMEM((KV_HEADS, 1, rep * QB), F32),
            pltpu.VMEM((KV_HEADS, HEAD_DIM, rep * QB), F32),
        ],
        compiler_params=pltpu.CompilerParams(dimension_semantics=("arbitrary", "arbitrary")),
        name="dsa_attention",
    )(rel_bias, qit, wit, ki, k, qt, vt, bkt)


RNN_TS = 256


def _gelu_tanh(x):
    c = np.float32(np.sqrt(2.0 / np.pi))
    return x * (0.5 * (1.0 + jnp.tanh(c * (x + np.float32(0.044715) * (x * x * x)))))


def _softplus(z):
    return jnp.maximum(z, 0.0) + jnp.log1p(jnp.exp(-jnp.abs(z)))


def _rglru_kernel(xr_ref, gate_ref, cw_ref, cb_ref, wa_ref, ba_ref, wx_ref, bx_ref, lam_ref, o_ref,
                  xext_ref, h_ref):
    i = pl.program_id(1)
    ts = RNN_TS

    @pl.when(i == 0)
    def _():
        xext_ref[0:8, :] = jnp.zeros((8, RNN_WIDTH), F32)
        h_ref[...] = jnp.zeros((1, RNN_WIDTH), F32)

    @pl.when(i > 0)
    def _():
        xext_ref[0:8, :] = xext_ref[ts:ts + 8, :]

    xext_ref[8:8 + ts, :] = xr_ref[...]

    row = lax.broadcasted_iota(I32, (ts, RNN_BLOCK_W), 0)
    for n in range(RNN_BLOCKS):
        cs = slice(n * RNN_BLOCK_W, (n + 1) * RNN_BLOCK_W)
        xc = cb_ref[:, cs]
        for jj in range(CONV_WIDTH):
            off = 8 - (CONV_WIDTH - 1) + jj
            xc = xc + xext_ref[off:off + ts, cs] * cw_ref[jj:jj + 1, cs]
        xcb = xc.astype(BF16)
        r = jax.nn.sigmoid(jnp.dot(xcb, wa_ref[n], preferred_element_type=F32) + ba_ref[n:n + 1, :])
        gi = jax.nn.sigmoid(jnp.dot(xcb, wx_ref[n], preferred_element_type=F32) + bx_ref[n:n + 1, :])
        log_a = (-LRU_C * r) * _softplus(-lam_ref[:, cs])
        a = jnp.exp(log_a)
        mult = jnp.sqrt(1.0 - jnp.exp(2.0 * log_a))
        bt = mult * (gi * xc)
        k = 1
        while k < ts:
            a_s = pltpu.roll(a, k, 0)
            b_s = pltpu.roll(bt, k, 0)
            keep = row >= k
            bt = jnp.where(keep, a * b_s + bt, bt)
            a = jnp.where(keep, a * a_s, a)
            k *= 2
        h = bt + a * h_ref[:, cs]
        h_ref[:, cs] = h[ts - 1:ts, :]
        o_ref[:, cs] = (h * _gelu_tanh(gate_ref[:, cs])).astype(o_ref.dtype)


def _rglru(pc, conv_w, conv_b, wa, ba, wx, bx, lam, batch, seq):
    nts = seq // RNN_TS
    full = lambda shape: pl.BlockSpec(shape, lambda b, i: (0,) * len(shape))
    return pl.pallas_call(
        _rglru_kernel,
        grid=(batch, nts),
        in_specs=[
            pl.BlockSpec((RNN_TS, RNN_WIDTH), lambda b, i: (b * nts + i, 0)),
            pl.BlockSpec((RNN_TS, RNN_WIDTH), lambda b, i: (b * nts + i, 1)),
            full((CONV_WIDTH, RNN_WIDTH)),
            full((1, RNN_WIDTH)),
            full((RNN_BLOCKS, RNN_BLOCK_W, RNN_BLOCK_W)),
            full((RNN_BLOCKS, RNN_BLOCK_W)),
            full((RNN_BLOCKS, RNN_BLOCK_W, RNN_BLOCK_W)),
            full((RNN_BLOCKS, RNN_BLOCK_W)),
            full((1, RNN_WIDTH)),
        ],
        out_specs=pl.BlockSpec((RNN_TS, RNN_WIDTH), lambda b, i: (b * nts + i, 0)),
        out_shape=jax.ShapeDtypeStruct((batch * seq, RNN_WIDTH), BF16),
        scratch_shapes=[pltpu.VMEM((RNN_TS + 8, RNN_WIDTH), F32), pltpu.VMEM((1, RNN_WIDTH), F32)],
        compiler_params=pltpu.CompilerParams(dimension_semantics=("arbitrary", "arbitrary")),
        name="rglru",
    )(pc, pc, conv_w, conv_b, wa, ba, wx, bx, lam)


MERGE_TM = 256


def _merge_kernel(attn_ref, rnn_ref, ga_ref, gr_ref, x_ref, wpa_ref, wpr_ref, wo_ref, g2_ref,
                  wrh_ref, wrl_ref, br_ref, x1_ref, u2_ref, lgt_ref):
    pa = jnp.dot(attn_ref[...], wpa_ref[...], preferred_element_type=F32)
    pr = jnp.dot(rnn_ref[...], wpr_ref[...], preferred_element_type=F32)
    merged = jax.nn.sigmoid(ga_ref[...]) * pa + jax.nn.sigmoid(gr_ref[...]) * pr
    x1 = x_ref[...] + jnp.dot(merged.astype(BF16), wo_ref[...], preferred_element_type=F32)
    x1_ref[...] = x1
    ms = jnp.mean(x1 * x1, axis=-1, keepdims=True)
    u2 = x1 * lax.rsqrt(ms + EPS) * g2_ref[...]
    u2_ref[...] = u2
    hi = u2.astype(BF16)
    lo = (u2 - hi.astype(F32)).astype(BF16)
    lg = (jnp.dot(hi, wrh_ref[...], preferred_element_type=F32)
          + jnp.dot(lo, wrh_ref[...], preferred_element_type=F32)
          + jnp.dot(hi, wrl_ref[...], preferred_element_type=F32)) + br_ref[...]
    lgt_ref[...] = lg.T


def _merge(attn, rnn, pc, x2, wpa, wpr, wo, g2, wrh, wrl, br):
    m = x2.shape[0]
    tm = MERGE_TM
    const = lambda shape: pl.BlockSpec(shape, lambda i: (0,) * len(shape), pipeline_mode=pl.Buffered(1))
    return pl.pallas_call(
        _merge_kernel,
        grid=(m // tm,),
        in_specs=[
            pl.BlockSpec((tm, ATTN_WIDTH), lambda i: (i, 0)),
            pl.BlockSpec((tm, RNN_WIDTH), lambda i: (i, 0)),
            pl.BlockSpec((tm, D_MODEL), lambda i: (i, 1)),
            pl.BlockSpec((tm, D_MODEL), lambda i: (i, 2)),
            pl.BlockSpec((tm, D_MODEL), lambda i: (i, 0)),
            const((ATTN_WIDTH, D_MODEL)),
            const((RNN_WIDTH, D_MODEL)),
            const((D_MODEL, D_MODEL)),
            const((1, D_MODEL)),
            const((D_MODEL, LANES)),
            const((D_MODEL, LANES)),
            const((1, LANES)),
        ],
        out_specs=[
            pl.BlockSpec((tm, D_MODEL), lambda i: (i, 0)),
            pl.BlockSpec((tm, D_MODEL), lambda i: (i, 0)),
            pl.BlockSpec((LANES, tm), lambda i: (0, i)),
        ],
        out_shape=[
            jax.ShapeDtypeStruct((m, D_MODEL), F32),
            jax.ShapeDtypeStruct((m, D_MODEL), F32),
            jax.ShapeDtypeStruct((LANES, m), F32),
        ],
        compiler_params=pltpu.CompilerParams(dimension_semantics=("arbitrary",)),
        name="merge_outproj",
    )(attn, rnn, pc, pc, x2, wpa, wpr, wo, g2, wrh, wrl, br)


ROUTE_CHUNK = 256


def _first_index_of_max(v, ridx, n):
    vmax = jnp.max(v, axis=0, keepdims=True)
    idx = jnp.min(jnp.where(v == vmax, ridx, jnp.int32(n)).astype(F32), axis=0, keepdims=True)
    return vmax, idx.astype(I32)


def _route_kernel(lgt_ref, dest_ref, gw_ref, meta_ref, cum_ref):
    t = lgt_ref.shape[1]
    eg = EXPERTS_PER_GROUP
    ridx8 = lax.broadcasted_iota(I32, (eg, t), 0)
    gl = lgt_ref[0:N_GROUPS, :]
    gmax, g_sel = _first_index_of_max(gl, ridx8, N_GROUPS)
    p_sel = 1.0 / jnp.sum(jnp.exp(gl - gmax), axis=0, keepdims=True)
    el = lgt_ref[N_GROUPS:N_GROUPS + eg, :]
    for g in range(1, N_GROUPS):
        el = jnp.where(g_sel == g, lgt_ref[N_GROUPS + g * eg:N_GROUPS + (g + 1) * eg, :], el)
    v0, i0 = _first_index_of_max(el, ridx8, eg)
    el1 = jnp.where(ridx8 == i0, -jnp.inf, el)
    v1, i1 = _first_index_of_max(el1, ridx8, eg)
    e1 = jnp.exp(v1 - v0)
    den = 1.0 + e1
    gw_ref[0:1, :] = p_sel * (1.0 / den)
    gw_ref[1:2, :] = p_sel * (e1 / den)
    ex0 = g_sel * eg + i0
    ex1 = g_sel * eg + i1

    eidx = lax.broadcasted_iota(I32, (N_EXPERTS, ROUTE_CHUNK), 0)
    ui = lax.broadcasted_iota(I32, (ROUTE_CHUNK, ROUTE_CHUNK), 0)
    uj = lax.broadcasted_iota(I32, (ROUTE_CHUNK, ROUTE_CHUNK), 1)
    upper = jnp.where(ui < uj, 1.0, 0.0).astype(BF16)
    run = jnp.zeros((N_EXPERTS, 1), F32)
    for c in range(t // ROUTE_CHUNK):
        cs = slice(c * ROUTE_CHUNK, (c + 1) * ROUTE_CHUNK)
        hit = jnp.where(eidx == ex0[:, cs], 1.0, jnp.where(eidx == ex1[:, cs], 1.0, 0.0))
        cum_ref[:, cs] = jnp.dot(hit.astype(BF16), upper, preferred_element_type=F32) + run
        run = run + jnp.sum(hit, axis=1, keepdims=True)

    counts = run.astype(I32)
    padded = ((counts + (MOE_BLOCK - 1)) >> MOE_SHIFT) << MOE_SHIFT
    pe = jnp.broadcast_to(padded, (N_EXPERTS, LANES))
    erow = lax.broadcasted_iota(I32, (N_EXPERTS, LANES), 0)
    k = 1
    while k < N_EXPERTS:
        pe = pe + jnp.where(erow >= k, pltpu.roll(pe, k, 0), 0)
        k *= 2
    pends = pe[:, 0:1]
    pstarts = pends - padded

    eidx_t = lax.broadcasted_iota(I32, (N_EXPERTS, t), 0)
    slot = cum_ref[...] + pstarts.astype(F32)
    dest_ref[0:1, :] = jnp.sum(jnp.where(eidx_t == ex0, slot, 0.0), axis=0, keepdims=True).astype(I32)
    dest_ref[1:2, :] = jnp.sum(jnp.where(eidx_t == ex1, slot, 0.0), axis=0, keepdims=True).astype(I32)

    nb = meta_ref.shape[1]
    own = lax.broadcasted_iota(I32, (N_EXPERTS, nb), 0) == lax.broadcasted_iota(I32, (N_EXPERTS, nb), 1)
    first_blk = (pstarts >> MOE_SHIFT).astype(F32)
    n_blk = (padded >> MOE_SHIFT).astype(F32)
    meta_ref[0:1, :] = jnp.sum(jnp.where(own, first_blk, 0.0), axis=0, keepdims=True).astype(I32)
    meta_ref[1:2, :] = jnp.sum(jnp.where(own, n_blk, 0.0), axis=0, keepdims=True).astype(I32)
    meta_ref[2:3, :] = jnp.broadcast_to(pends[N_EXPERTS - 1:N_EXPERTS, :] >> MOE_SHIFT, (1, nb))


def _route(lgt):
    t = lgt.shape[1]
    return pl.pallas_call(
        _route_kernel,
        out_shape=[
            jax.ShapeDtypeStruct((2, t), I32),
            jax.ShapeDtypeStruct((2, t), F32),
            jax.ShapeDtypeStruct((3, LANES), I32),
        ],
        scratch_shapes=[pltpu.VMEM((N_EXPERTS, t), F32)],
        name="route",
    )(lgt)


def _expert_kernel(first_ref, nblk_ref, nact_ref, dest_ref, u_ref, wg_ref, wu_ref, wd_ref, ys_ref,
                   tok_ref, xbuf_ref, ybuf_ref, wgb_ref, wub_ref, wdb_ref, gsem, osem):
    e = pl.program_id(0)
    first = first_ref[e]
    nb = nblk_ref[e]
    ntok = u_ref.shape[0]
    nslot = tok_ref.shape[0]

    def row_copy(blk, slot, r):
        tok = tok_ref[blk * MOE_BLOCK + r]
        return pltpu.make_async_copy(u_ref.at[pl.ds(tok, 1)], xbuf_ref.at[slot, pl.ds(r, 1)], gsem.at[slot])

    def start_rows(blk, slot):
        def body(r, carry):
            row_copy(blk, slot, r).start()
            return carry
        lax.fori_loop(0, MOE_BLOCK, body, 0, unroll=8)

    def wait_rows(blk, slot):
        def body(r, carry):
            row_copy(blk, slot, r).wait()
            return carry
        lax.fori_loop(0, MOE_BLOCK, body, 0, unroll=8)

    def out_copy(blk, slot):
        rows = pl.ds(pl.multiple_of(blk * MOE_BLOCK, MOE_BLOCK), MOE_BLOCK)
        return pltpu.make_async_copy(ybuf_ref.at[slot], ys_ref.at[rows], osem.at[slot])

    @pl.when(e == 0)
    def _():
        def clear(p, carry):
            tok_ref[p] = 0
            return carry
        lax.fori_loop(0, nslot, clear, 0, unroll=8)

        def put(t, carry):
            for kk in range(TOP_K_IN_GROUP):
                tok_ref[dest_ref[kk * ntok + t]] = t
            return carry
        lax.fori_loop(0, ntok, put, 0, unroll=8)

    @pl.when(nb > 0)
    def _():
        start_rows(first, 0)

    wgb_ref[...] = wg_ref[...].astype(BF16)
    wub_ref[...] = wu_ref[...].astype(BF16)
    wdb_ref[...] = wd_ref[...].astype(BF16)

    def block(b, carry):
        slot = b % 2
        blk = first + b

        @pl.when(b + 1 < nb)
        def _():
            start_rows(blk + 1, 1 - slot)

        wait_rows(blk, slot)
        xb = xbuf_ref[slot].astype(BF16)
        hg = jnp.dot(xb, wgb_ref[...], preferred_element_type=F32)
        hu = jnp.dot(xb, wub_ref[...], preferred_element_type=F32)
        h = (hg * jax.nn.sigmoid(hg)) * hu
        y = jnp.dot(h.astype(BF16), wdb_ref[...], preferred_element_type=F32)

        @pl.when(b >= 2)
        def _():
            out_copy(blk - 2, slot).wait()

        ybuf_ref[slot] = y
        out_copy(blk, slot).start()
        return carry

    lax.fori_loop(0, nb, block, 0)

    @pl.when(nb >= 2)
    def _():
        out_copy(first + nb - 2, nb % 2).wait()

    @pl.when(nb >= 1)
    def _():
        out_copy(first + nb - 1, (nb - 1) % 2).wait()

    @pl.when(e == pl.num_programs(0) - 1)
    def _():
        ntotal = ys_ref.shape[0] // MOE_BLOCK
        ybuf_ref[0] = jnp.zeros(ybuf_ref.shape[1:], ybuf_ref.dtype)

        def fill_start(blk, carry):
            out_copy(blk, 0).start()
            return carry

        def fill_wait(blk, carry):
            out_copy(blk, 0).wait()
            return carry

        lax.fori_loop(nact_ref[0], ntotal, fill_start, 0)
        lax.fori_loop(nact_ref[0], ntotal, fill_wait, 0)


def _experts(first_blk, n_blk, nact, dest_flat, u2, w_gate, w_up, w_down, cap):
    d = u2.shape[1]
    wsel = lambda e, fb, nb, na, de: (e, 0, 0)
    grid_spec = pltpu.PrefetchScalarGridSpec(
        num_scalar_prefetch=4,
        grid=(N_EXPERTS,),
        in_specs=[
            pl.BlockSpec(memory_space=pl.ANY),
            pl.BlockSpec((None, d, EXPERT_FF), wsel),
            pl.BlockSpec((None, d, EXPERT_FF), wsel),
            pl.BlockSpec((None, EXPERT_FF, d), wsel),
        ],
        out_specs=pl.BlockSpec(memory_space=pl.ANY),
        scratch_shapes=[
            pltpu.SMEM((cap,), I32),
            pltpu.VMEM((2, MOE_BLOCK, d), u2.dtype),
            pltpu.VMEM((2, MOE_BLOCK, d), F32),
            pltpu.VMEM((d, EXPERT_FF), BF16),
            pltpu.VMEM((d, EXPERT_FF), BF16),
            pltpu.VMEM((EXPERT_FF, d), BF16),
            pltpu.SemaphoreType.DMA((2,)),
            pltpu.SemaphoreType.DMA((2,)),
        ],
    )
    return pl.pallas_call(
        _expert_kernel,
        grid_spec=grid_spec,
        out_shape=jax.ShapeDtypeStruct((cap, d), F32),
        compiler_params=pltpu.CompilerParams(dimension_semantics=("arbitrary",), has_side_effects=True),
        name="experts",
    )(first_blk, n_blk, nact, dest_flat, u2, w_gate, w_up, w_down)


COMB_TOK = 64


def _combine_kernel(dest_ref, ys_ref, x1_ref, gw_ref, gf_ref, o_ref, buf_ref, sem):
    i = pl.program_id(0)
    n = pl.num_programs(0)
    ntok = n * COMB_TOK

    def copy(step, slot, tl, kk):
        d = dest_ref[kk * ntok + step * COMB_TOK + tl]
        return pltpu.make_async_copy(ys_ref.at[pl.ds(d, 1)], buf_ref.at[slot, kk, pl.ds(tl, 1)], sem.at[slot])

    def start_all(step, slot):
        def body(tl, carry):
            for kk in range(TOP_K_IN_GROUP):
                copy(step, slot, tl, kk).start()
            return carry
        lax.fori_loop(0, COMB_TOK, body, 0)

    def wait_all(step, slot):
        def body(tl, carry):
            for kk in range(TOP_K_IN_GROUP):
                copy(step, slot, tl, kk).wait()
            return carry
        lax.fori_loop(0, COMB_TOK, body, 0)

    @pl.when(i == 0)
    def _():
        start_all(0, 0)

    @pl.when(i + 1 < n)
    def _():
        start_all(i + 1, (i + 1) % 2)

    slot = i % 2
    wait_all(i, slot)
    y = gw_ref[:, 0:1] * buf_ref[slot, 0] + gw_ref[:, 1:2] * buf_ref[slot, 1]
    x = x1_ref[...] + y
    ms = jnp.mean(x * x, axis=-1, keepdims=True)
    o_ref[...] = x * lax.rsqrt(ms + EPS) * gf_ref[...]


def _combine(dest, ys, x1, gw_t, gf):
    t, d = x1.shape
    return pl.pallas_call(
        _combine_kernel,
        grid=(t // COMB_TOK,),
        in_specs=[
            pl.BlockSpec(memory_space=pltpu.SMEM),
            pl.BlockSpec(memory_space=pl.ANY),
            pl.BlockSpec((COMB_TOK, d), lambda i: (i, 0)),
            pl.BlockSpec((COMB_TOK, TOP_K_IN_GROUP), lambda i: (i, 0)),
            pl.BlockSpec((1, d), lambda i: (0, 0)),
        ],
        out_specs=pl.BlockSpec((COMB_TOK, d), lambda i: (i, 0)),
        out_shape=jax.ShapeDtypeStruct((t, d), F32),
        scratch_shapes=[
            pltpu.VMEM((2, TOP_K_IN_GROUP, COMB_TOK, d), F32),
            pltpu.SemaphoreType.DMA((2,)),
        ],
        compiler_params=pltpu.CompilerParams(dimension_semantics=("arbitrary",)),
        name="combine",
    )(dest, ys, x1, gw_t, gf)


def kernel(x, norm1_g, w_in, conv_w, conv_b, lru_wa, lru_ba, lru_wx, lru_bx, lru_lambda, w_proj_attn,
           w_proj_rnn, w_out, rel_bias, norm2_g, w_group, b_group, w_expert_router, b_expert_router,
           w_gate, w_up, w_down, norm_f_g):
    batch, seq, d = x.shape
    tokens = batch * seq
    nblk = seq // QB
    rep = ATTN_HEADS // KV_HEADS
    x2 = x.reshape(tokens, d)

    w = w_in[0]
    w_a = w[:, :PA_COLS].astype(BF16)
    w_c = jnp.concatenate(
        [w[:, OFF_XR:], w[:, OFF_KI:OFF_XR], jnp.zeros((d, PC_COLS - PC_MAIN - (OFF_XR - OFF_KI)), w.dtype)],
        axis=1).astype(BF16)
    g1 = norm1_g[0].reshape(1, d)
    pa = _norm_proj(x2, g1, w_a, BF16, 1024, 512)
    pc = _norm_proj(x2, g1, w_c, F32, 1024, 1280)

    q = pa[:, OFF_Q:OFF_K].reshape(batch, nblk, QB, KV_HEADS, rep, HEAD_DIM)
    qt = q.transpose(0, 1, 3, 5, 4, 2).reshape(batch, nblk, KV_HEADS, HEAD_DIM, rep * QB)
    k = pa[:, OFF_K:OFF_V].reshape(batch, seq, KV_WIDTH)
    v = pa[:, OFF_V:OFF_QI].reshape(batch, seq // KT, KT, KV_HEADS, HEAD_DIM)
    vt = v.transpose(0, 3, 1, 4, 2)
    qi = pa[:, OFF_QI:OFF_KI].reshape(batch, nblk, QB, IDX_HEADS, IDX_DIM)
    qit = qi.transpose(0, 1, 4, 3, 2).reshape(batch, nblk, IDX_DIM, IDX_HEADS * QB)
    ki = pc[:, PC_MAIN:PC_MAIN + IDX_DIM].astype(BF16).reshape(batch, seq, IDX_DIM)
    wi = pc[:, PC_MAIN + IDX_DIM:PC_MAIN + IDX_DIM + IDX_HEADS].reshape(batch, nblk, QB, IDX_HEADS)
    wit = wi.transpose(0, 1, 3, 2).reshape(batch, nblk, 1, IDX_HEADS * QB)
    ks = np.arange(QB)[:, None]
    qs = np.arange(QB)[None, :]
    bkt = jnp.asarray(np.stack([_rel_bucket_np(qs - ks + QB * dd) for dd in range(2)]))
    attn = _dsa_attention(rel_bias, qit, wit, ki, k, qt, vt, bkt, batch, seq).reshape(tokens, ATTN_WIDTH)

    rnn = _rglru(pc, conv_w[0], conv_b[0].reshape(1, RNN_WIDTH), lru_wa[0].astype(BF16), lru_ba[0],
                 lru_wx[0].astype(BF16), lru_bx[0], lru_lambda[0].reshape(1, RNN_WIDTH), batch, seq)

    w_r = jnp.concatenate([w_group[0], w_expert_router[0],
                           jnp.zeros((d, LANES - N_GROUPS - N_EXPERTS), F32)], axis=1)
    w_rh = w_r.astype(BF16)
    w_rl = (w_r - w_rh.astype(F32)).astype(BF16)
    b_r = jnp.concatenate([b_group[0], b_expert_router[0],
                           jnp.zeros((LANES - N_GROUPS - N_EXPERTS,), F32)]).reshape(1, LANES)
    x1, u2, lgt = _merge(attn, rnn, pc, x2, w_proj_attn[0].astype(BF16), w_proj_rnn[0].astype(BF16),
                         w_out[0].astype(BF16), norm2_g[0].reshape(1, d), w_rh, w_rl, b_r)

    n_slots = tokens * TOP_K_IN_GROUP
    cap = -(-(n_slots + N_EXPERTS * (MOE_BLOCK - 1)) // MOE_BLOCK) * MOE_BLOCK
    dest, gw, meta = _route(lgt)

    dest_flat = dest.reshape(-1)
    ys = _experts(meta[0, :N_EXPERTS], meta[1, :N_EXPERTS], meta[2, :1], dest_flat, u2,
                  w_gate[0], w_up[0], w_down[0], cap)
    out = _combine(dest_flat, ys, x1, gw.T, norm_f_g.reshape(1, d))
    return out.reshape(batch, seq, d)
```

```python
import functools

import numpy as np
import jax
import jax.numpy as jnp
from jax import lax
from jax.experimental import pallas as pl
from jax.experimental.pallas import tpu as pltpu

D_MODEL = 2048
ATTN_HEADS = 8
KV_HEADS = 2
HEAD_DIM = 128
ATTN_WIDTH = ATTN_HEADS * HEAD_DIM
KV_WIDTH = KV_HEADS * HEAD_DIM
IDX_HEADS = 16
IDX_DIM = 64
TOPK_MAX = 256
RNN_WIDTH = 1024
RNN_BLOCKS = 8
RNN_BLOCK_W = RNN_WIDTH // RNN_BLOCKS
CONV_WIDTH = 4
LRU_C = 8.0
REL_BUCKETS = 32
REL_MAX_DIST = 128
N_GROUPS = 8
EXPERTS_PER_GROUP = 8
N_EXPERTS = N_GROUPS * EXPERTS_PER_GROUP
TOP_K_IN_GROUP = 2
EXPERT_FF = 512
MOE_BLOCK = 128
MOE_SHIFT = 7
EPS = 1e-6

LANES = 128
QB = 128
KT = 256
NEG_BIG = -1e30
INT_MIN = -(2 ** 31)
LOG2E = np.float32(np.log2(np.e))
F32 = jnp.float32
BF16 = jnp.bfloat16
I32 = jnp.int32

OFF_Q = 0
OFF_K = OFF_Q + ATTN_WIDTH
OFF_V = OFF_K + KV_WIDTH
OFF_QI = OFF_V + KV_WIDTH
OFF_KI = OFF_QI + IDX_HEADS * IDX_DIM
OFF_WI = OFF_KI + IDX_DIM
OFF_XR = OFF_WI + IDX_HEADS
OFF_RG = OFF_XR + RNN_WIDTH
OFF_GA = OFF_RG + RNN_WIDTH
OFF_GR = OFF_GA + D_MODEL
IN_COLS = OFF_GR + D_MODEL
PA_COLS = OFF_KI
PC_MAIN = IN_COLS - OFF_XR
PC_COLS = 6400


def _rel_bucket_np(n):
    n = np.maximum(n, 0)
    max_exact = REL_BUCKETS // 2
    nf = np.maximum(n, 1).astype(np.float32)
    large = max_exact + (np.log(nf / np.float32(max_exact)) / np.float32(np.log(REL_MAX_DIST / max_exact))
                         * np.float32(REL_BUCKETS - max_exact)).astype(np.int32)
    large = np.minimum(large, REL_BUCKETS - 1)
    return np.where(n < max_exact, n, large).astype(np.int32)


def _norm_proj_kernel(x_ref, g_ref, w_ref, o_ref, u_ref):
    @pl.when(pl.program_id(1) == 0)
    def _():
        x = x_ref[...]
        ms = jnp.mean(x * x, axis=-1, keepdims=True)
        u_ref[...] = (x * lax.rsqrt(ms + EPS) * g_ref[...]).astype(BF16)

    o_ref[...] = jnp.dot(u_ref[...], w_ref[...], preferred_element_type=F32).astype(o_ref.dtype)


def _norm_proj(x2, g, w, out_dtype, tm, tn):
    m, d = x2.shape
    n = w.shape[1]
    return pl.pallas_call(
        _norm_proj_kernel,
        grid=(m // tm, n // tn),
        in_specs=[
            pl.BlockSpec((tm, d), lambda i, j: (i, 0)),
            pl.BlockSpec((1, d), lambda i, j: (0, 0)),
            pl.BlockSpec((d, tn), lambda i, j: (0, j)),
        ],
        out_specs=pl.BlockSpec((tm, tn), lambda i, j: (i, j)),
        out_shape=jax.ShapeDtypeStruct((m, n), out_dtype),
        scratch_shapes=[pltpu.VMEM((tm, d), BF16)],
        compiler_params=pltpu.CompilerParams(dimension_semantics=("arbitrary", "arbitrary")),
        name="norm_proj",
    )(x2, g, w)


def _dsa_kernel(relb_ref, qit_ref, wit_ref, ki_ref, k_ref, qt_ref, vt_ref, bkt_ref, o_ref,
                keys_ref, bias_ref, m_ref, l_ref, acc_ref):
    b = pl.program_id(0)
    j = pl.program_id(1)

    @pl.when((b == 0) & (j == 0))
    def _():
        for d in range(2):
            bk = bkt_ref[d]
            for h in range(ATTN_HEADS):
                tile = jnp.zeros((QB, QB), F32)
                for bb in range(REL_BUCKETS):
                    tile = jnp.where(bk == bb, relb_ref[bb, h] * LOG2E, tile)
                bias_ref[d, h] = tile

    t0 = j * QB
    ntile = (j + 2) // 2
    q_pos = t0 + lax.broadcasted_iota(I32, (KT, QB), 1)
    row_iota = lax.broadcasted_iota(I32, (KT, QB), 0)

    w_scale = np.float32(IDX_HEADS ** -0.5) * np.float32(IDX_DIM ** -0.5)
    wrow = wit_ref[...] * w_scale

    def score_tile(kt, carry):
        r0 = pl.multiple_of(kt * KT, KT)
        ki_t = ki_ref[pl.ds(r0, KT), :]
        acc = jnp.zeros((KT, QB), F32)
        for c in range(4):
            dots = jnp.dot(ki_t, qit_ref[:, c * 512:(c + 1) * 512], preferred_element_type=F32)
            for hh in range(4):
                h = c * 4 + hh
                acc = acc + jnp.maximum(dots[:, hh * QB:(hh + 1) * QB], 0.0) * wrow[:, h * QB:(h + 1) * QB]
        bits = pltpu.bitcast(acc, I32)
        key = bits ^ ((bits >> 31) & jnp.int32(0x7FFFFFFF))
        key = jnp.where(r0 + row_iota <= q_pos, key, jnp.int32(INT_MIN))
        keys_ref[pl.ds(r0, KT), :] = key
        return carry

    lax.fori_loop(0, ntile, score_tile, 0)

    n_acc = 4

    def count_ge(cand):
        cand8 = jnp.broadcast_to(cand, (8, QB))

        def body(kt, accs):
            r0 = pl.multiple_of(kt * KT, KT)
            accs = list(accs)
            for v in range(KT // 8):
                blk = keys_ref[pl.ds(r0 + v * 8, 8), :]
                accs[v % n_acc] = accs[v % n_acc] + jnp.where(blk >= cand8, jnp.int32(1), jnp.int32(0))
            return tuple(accs)

        accs = lax.fori_loop(0, ntile, body, (jnp.zeros((8, QB), I32),) * n_acc)
        acc8 = (accs[0] + accs[1]) + (accs[2] + accs[3])
        return jnp.sum(acc8.astype(F32), axis=0, keepdims=True).astype(I32)

    c0 = count_ge(jnp.zeros((1, QB), I32))
    thr0 = jnp.where(c0 >= TOPK_MAX, jnp.int32(0), jnp.int32(INT_MIN))

    def bit_step(i, thr):
        cand = thr | (jnp.int32(1) << (30 - i))
        c = count_ge(cand)
        return jnp.where(c >= TOPK_MAX, cand, thr)

    thr = lax.fori_loop(0, 31, bit_step, thr0)
    thr = jnp.maximum(thr, jnp.int32(INT_MIN + 1))

    qk_scale = np.float32(HEAD_DIM ** -0.5) * LOG2E
    rep = ATTN_HEADS // KV_HEADS
    nfar = jnp.maximum((j - 1) // 2, 0)

    m_ref[...] = jnp.full(m_ref.shape, NEG_BIG, F32)
    l_ref[...] = jnp.zeros(l_ref.shape, F32)
    acc_ref[...] = jnp.zeros(acc_ref.shape, F32)

    def attend(kt, g, near):
        r0 = pl.multiple_of(kt * KT, KT)
        k_t = k_ref[pl.ds(r0, KT), g * HEAD_DIM:(g + 1) * HEAD_DIM]
        lg = jnp.dot(k_t, qt_ref[g], preferred_element_type=F32) * qk_scale
        sel = keys_ref[pl.ds(r0, KT), :] >= thr
        cols = []
        for r in range(rep):
            h = g * rep + r
            far_bias = relb_ref[REL_BUCKETS - 1, h] * LOG2E
            blk = lg[:, r * QB:(r + 1) * QB]
            if near:
                subs = []
                for sub in range(KT // QB):
                    dblk = j - (kt * (KT // QB) + sub)
                    bias = jnp.where(dblk == 0, bias_ref[0, h],
                                     jnp.where(dblk == 1, bias_ref[1, h], far_bias))
                    subs.append(blk[sub * QB:(sub + 1) * QB, :] + bias)
                blk = jnp.concatenate(subs, axis=0)
            else:
                blk = blk + far_bias
            cols.append(jnp.where(sel, blk, NEG_BIG))
        s = jnp.concatenate(cols, axis=1)
        m_old = m_ref[g]
        m_new = jnp.maximum(m_old, jnp.max(s, axis=0, keepdims=True))
        alpha = jnp.exp2(m_old - m_new)
        p = jnp.exp2(s - m_new)
        l_ref[g] = alpha * l_ref[g] + jnp.sum(p, axis=0, keepdims=True)
        pv = jnp.dot(vt_ref[g, kt], p.astype(BF16), preferred_element_type=F32)
        acc_ref[g] = acc_ref[g] * alpha + pv
        m_ref[g] = m_new

    def far_body(kt, carry):
        for g in range(KV_HEADS):
            attend(kt, g, False)
        return carry

    def near_body(kt, carry):
        for g in range(KV_HEADS):
            attend(kt, g, True)
        return carry

    lax.fori_loop(0, nfar, far_body, 0)
    lax.fori_loop(nfar, ntile, near_body, 0)

    for g in range(KV_HEADS):
        out_t = acc_ref[g] / l_ref[g]
        for r in range(rep):
            h = g * rep + r
            o_ref[:, h * HEAD_DIM:(h + 1) * HEAD_DIM] = out_t[:, r * QB:(r + 1) * QB].T.astype(o_ref.dtype)


def _dsa_attention(rel_bias, qit, wit, ki, k, qt, vt, bkt, batch, seq):
    nblk = seq // QB
    rep = ATTN_HEADS // KV_HEADS
    return pl.pallas_call(
        _dsa_kernel,
        grid=(batch, nblk),
        in_specs=[
            pl.BlockSpec(memory_space=pltpu.SMEM),
            pl.BlockSpec((None, None, IDX_DIM, IDX_HEADS * QB), lambda b, j: (b, j, 0, 0)),
            pl.BlockSpec((None, None, 1, IDX_HEADS * QB), lambda b, j: (b, j, 0, 0)),
            pl.BlockSpec((None, seq, IDX_DIM), lambda b, j: (b, 0, 0)),
            pl.BlockSpec((None, seq, KV_WIDTH), lambda b, j: (b, 0, 0)),
            pl.BlockSpec((None, None, KV_HEADS, HEAD_DIM, rep * QB), lambda b, j: (b, j, 0, 0, 0)),
            pl.BlockSpec((None, KV_HEADS, seq // KT, HEAD_DIM, KT), lambda b, j: (b, 0, 0, 0, 0)),
            pl.BlockSpec((2, QB, QB), lambda b, j: (0, 0, 0)),
        ],
        out_specs=pl.BlockSpec((None, QB, ATTN_WIDTH), lambda b, j: (b, j, 0)),
        out_shape=jax.ShapeDtypeStruct((batch, seq, ATTN_WIDTH), BF16),
        scratch_shapes=[
            pltpu.VMEM((seq, QB), I32),
            pltpu.VMEM((2, ATTN_HEADS, QB, QB), F32),
            pltpu.VMEM((KV_HEADS, 1, rep * QB), F32),
            pltpu.VMEM((KV_HEADS, 1, rep * QB), F32),
            pltpu.VMEM((KV_HEADS, HEAD_DIM, rep * QB), F32),
        ],
        compiler_params=pltpu.CompilerParams(dimension_semantics=("arbitrary", "arbitrary")),
        name="dsa_attention",
    )(rel_bias, qit, wit, ki, k, qt, vt, bkt)


RNN_TS = 256


def _gelu_tanh(x):
    c = np.float32(np.sqrt(2.0 / np.pi))
    return x * (0.5 * (1.0 + jnp.tanh(c * (x + np.float32(0.044715) * (x * x * x)))))


def _softplus(z):
    return jnp.maximum(z, 0.0) + jnp.log1p(jnp.exp(-jnp.abs(z)))


def _rglru_kernel(xr_ref, gate_ref, cw_ref, cb_ref, wa_ref, ba_ref, wx_ref, bx_ref, lam_ref, o_ref,
                  xext_ref, h_ref):
    i = pl.program_id(1)
    ts = RNN_TS

    @pl.when(i == 0)
    def _():
        xext_ref[0:8, :] = jnp.zeros((8, RNN_WIDTH), F32)
        h_ref[...] = jnp.zeros((1, RNN_WIDTH), F32)

    @pl.when(i > 0)
    def _():
        xext_ref[0:8, :] = xext_ref[ts:ts + 8, :]

    xext_ref[8:8 + ts, :] = xr_ref[...]

    row = lax.broadcasted_iota(I32, (ts, RNN_BLOCK_W), 0)
    for n in range(RNN_BLOCKS):
        cs = slice(n * RNN_BLOCK_W, (n + 1) * RNN_BLOCK_W)
        xc = cb_ref[:, cs]
        for jj in range(CONV_WIDTH):
            off = 8 - (CONV_WIDTH - 1) + jj
            xc = xc + xext_ref[off:off + ts, cs] * cw_ref[jj:jj + 1, cs]
        xcb = xc.astype(BF16)
        r = jax.nn.sigmoid(jnp.dot(xcb, wa_ref[n], preferred_element_type=F32) + ba_ref[n:n + 1, :])
        gi = jax.nn.sigmoid(jnp.dot(xcb, wx_ref[n], preferred_element_type=F32) + bx_ref[n:n + 1, :])
        log_a = (-LRU_C * r) * _softplus(-lam_ref[:, cs])
        a = jnp.exp(log_a)
        mult = jnp.sqrt(1.0 - jnp.exp(2.0 * log_a))
        bt = mult * (gi * xc)
        k = 1
        while k < ts:
            a_s = pltpu.roll(a, k, 0)
            b_s = pltpu.roll(bt, k, 0)
            keep = row >= k
            bt = jnp.where(keep, a * b_s + bt, bt)
            a = jnp.where(keep, a * a_s, a)
            k *= 2
        h = bt + a * h_ref[:, cs]
        h_ref[:, cs] = h[ts - 1:ts, :]
        o_ref[:, cs] = (h * _gelu_tanh(gate_ref[:, cs])).astype(o_ref.dtype)


def _rglru(pc, conv_w, conv_b, wa, ba, wx, bx, lam, batch, seq):
    nts = seq // RNN_TS
    full = lambda shape: pl.BlockSpec(shape, lambda b, i: (0,) * len(shape))
    return pl.pallas_call(
        _rglru_kernel,
        grid=(batch, nts),
        in_specs=[
            pl.BlockSpec((RNN_TS, RNN_WIDTH), lambda b, i: (b * nts + i, 0)),
            pl.BlockSpec((RNN_TS, RNN_WIDTH), lambda b, i: (b * nts + i, 1)),
            full((CONV_WIDTH, RNN_WIDTH)),
            full((1, RNN_WIDTH)),
            full((RNN_BLOCKS, RNN_BLOCK_W, RNN_BLOCK_W)),
            full((RNN_BLOCKS, RNN_BLOCK_W)),
            full((RNN_BLOCKS, RNN_BLOCK_W, RNN_BLOCK_W)),
            full((RNN_BLOCKS, RNN_BLOCK_W)),
            full((1, RNN_WIDTH)),
        ],
        out_specs=pl.BlockSpec((RNN_TS, RNN_WIDTH), lambda b, i: (b * nts + i, 0)),
        out_shape=jax.ShapeDtypeStruct((batch * seq, RNN_WIDTH), BF16),
        scratch_shapes=[pltpu.VMEM((RNN_TS + 8, RNN_WIDTH), F32), pltpu.VMEM((1, RNN_WIDTH), F32)],
        compiler_params=pltpu.CompilerParams(dimension_semantics=("arbitrary", "arbitrary")),
        name="rglru",
    )(pc, pc, conv_w, conv_b, wa, ba, wx, bx, lam)


MERGE_TM = 256
ROW_CHUNKS = D_MODEL // LANES


def _store_token_major(ref, base, val):
    n = val.shape[0]
    for c in range(ROW_CHUNKS):
        ref[pl.ds(base + c, n, stride=ROW_CHUNKS), :] = val[:, c * LANES:(c + 1) * LANES]


def _load_token_major(ref, base, n):
    return jnp.concatenate([ref[pl.ds(base + c, n, stride=ROW_CHUNKS), :] for c in range(ROW_CHUNKS)], axis=1)


def _merge_kernel(attn_ref, rnn_ref, ga_ref, gr_ref, x_ref, wpa_ref, wpr_ref, wo_ref, g2_ref,
                  wrh_ref, wrl_ref, br_ref, x1_ref, u2_ref, lgt_ref):
    pa = jnp.dot(attn_ref[...], wpa_ref[...], preferred_element_type=F32)
    pr = jnp.dot(rnn_ref[...], wpr_ref[...], preferred_element_type=F32)
    merged = jax.nn.sigmoid(ga_ref[...]) * pa + jax.nn.sigmoid(gr_ref[...]) * pr
    x1 = x_ref[...] + jnp.dot(merged.astype(BF16), wo_ref[...], preferred_element_type=F32)
    x1_ref[...] = x1
    ms = jnp.mean(x1 * x1, axis=-1, keepdims=True)
    u2 = x1 * lax.rsqrt(ms + EPS) * g2_ref[...]
    _store_token_major(u2_ref, 0, u2)
    hi = u2.astype(BF16)
    lo = (u2 - hi.astype(F32)).astype(BF16)
    lg = (jnp.dot(hi, wrh_ref[...], preferred_element_type=F32)
          + jnp.dot(lo, wrh_ref[...], preferred_element_type=F32)
          + jnp.dot(hi, wrl_ref[...], preferred_element_type=F32)) + br_ref[...]
    lgt_ref[...] = lg.T


def _merge(attn, rnn, pc, x2, wpa, wpr, wo, g2, wrh, wrl, br):
    m = x2.shape[0]
    tm = MERGE_TM
    const = lambda shape: pl.BlockSpec(shape, lambda i: (0,) * len(shape), pipeline_mode=pl.Buffered(1))
    return pl.pallas_call(
        _merge_kernel,
        grid=(m // tm,),
        in_specs=[
            pl.BlockSpec((tm, ATTN_WIDTH), lambda i: (i, 0)),
            pl.BlockSpec((tm, RNN_WIDTH), lambda i: (i, 0)),
            pl.BlockSpec((tm, D_MODEL), lambda i: (i, 1)),
            pl.BlockSpec((tm, D_MODEL), lambda i: (i, 2)),
            pl.BlockSpec((tm, D_MODEL), lambda i: (i, 0)),
            const((ATTN_WIDTH, D_MODEL)),
            const((RNN_WIDTH, D_MODEL)),
            const((D_MODEL, D_MODEL)),
            const((1, D_MODEL)),
            const((D_MODEL, LANES)),
            const((D_MODEL, LANES)),
            const((1, LANES)),
        ],
        out_specs=[
            pl.BlockSpec((tm, D_MODEL), lambda i: (i, 0)),
            pl.BlockSpec((tm * ROW_CHUNKS, LANES), lambda i: (i, 0)),
            pl.BlockSpec((LANES, tm), lambda i: (0, i)),
        ],
        out_shape=[
            jax.ShapeDtypeStruct((m, D_MODEL), F32),
            jax.ShapeDtypeStruct((m * ROW_CHUNKS, LANES), F32),
            jax.ShapeDtypeStruct((LANES, m), F32),
        ],
        compiler_params=pltpu.CompilerParams(dimension_semantics=("arbitrary",)),
        name="merge_outproj",
    )(attn, rnn, pc, pc, x2, wpa, wpr, wo, g2, wrh, wrl, br)


ROUTE_CHUNK = 256


def _first_index_of_max(v, ridx, n):
    vmax = jnp.max(v, axis=0, keepdims=True)
    idx = jnp.min(jnp.where(v == vmax, ridx, jnp.int32(n)).astype(F32), axis=0, keepdims=True)
    return vmax, idx.astype(I32)


def _route_kernel(lgt_ref, dest_ref, gw_ref, meta_ref, cum_ref):
    t = lgt_ref.shape[1]
    eg = EXPERTS_PER_GROUP
    ridx8 = lax.broadcasted_iota(I32, (eg, t), 0)
    gl = lgt_ref[0:N_GROUPS, :]
    gmax, g_sel = _first_index_of_max(gl, ridx8, N_GROUPS)
    p_sel = 1.0 / jnp.sum(jnp.exp(gl - gmax), axis=0, keepdims=True)
    el = lgt_ref[N_GROUPS:N_GROUPS + eg, :]
    for g in range(1, N_GROUPS):
        el = jnp.where(g_sel == g, lgt_ref[N_GROUPS + g * eg:N_GROUPS + (g + 1) * eg, :], el)
    v0, i0 = _first_index_of_max(el, ridx8, eg)
    el1 = jnp.where(ridx8 == i0, -jnp.inf, el)
    v1, i1 = _first_index_of_max(el1, ridx8, eg)
    e1 = jnp.exp(v1 - v0)
    den = 1.0 + e1
    gw_ref[0:1, :] = p_sel * (1.0 / den)
    gw_ref[1:2, :] = p_sel * (e1 / den)
    ex0 = g_sel * eg + i0
    ex1 = g_sel * eg + i1

    eidx = lax.broadcasted_iota(I32, (N_EXPERTS, ROUTE_CHUNK), 0)
    ui = lax.broadcasted_iota(I32, (ROUTE_CHUNK, ROUTE_CHUNK), 0)
    uj = lax.broadcasted_iota(I32, (ROUTE_CHUNK, ROUTE_CHUNK), 1)
    upper = jnp.where(ui < uj, 1.0, 0.0).astype(BF16)
    run = jnp.zeros((N_EXPERTS, 1), F32)
    for c in range(t // ROUTE_CHUNK):
        cs = slice(c * ROUTE_CHUNK, (c + 1) * ROUTE_CHUNK)
        hit = jnp.where(eidx == ex0[:, cs], 1.0, jnp.where(eidx == ex1[:, cs], 1.0, 0.0))
        cum_ref[:, cs] = jnp.dot(hit.astype(BF16), upper, preferred_element_type=F32) + run
        run = run + jnp.sum(hit, axis=1, keepdims=True)

    counts = run.astype(I32)
    padded = ((counts + (MOE_BLOCK - 1)) >> MOE_SHIFT) << MOE_SHIFT
    pe = jnp.broadcast_to(padded, (N_EXPERTS, LANES))
    erow = lax.broadcasted_iota(I32, (N_EXPERTS, LANES), 0)
    k = 1
    while k < N_EXPERTS:
        pe = pe + jnp.where(erow >= k, pltpu.roll(pe, k, 0), 0)
        k *= 2
    pends = pe[:, 0:1]
    pstarts = pends - padded

    eidx_t = lax.broadcasted_iota(I32, (N_EXPERTS, t), 0)
    slot = cum_ref[...] + pstarts.astype(F32)
    dest_ref[0:1, :] = jnp.sum(jnp.where(eidx_t == ex0, slot, 0.0), axis=0, keepdims=True).astype(I32)
    dest_ref[1:2, :] = jnp.sum(jnp.where(eidx_t == ex1, slot, 0.0), axis=0, keepdims=True).astype(I32)

    nb = meta_ref.shape[1]
    own = lax.broadcasted_iota(I32, (N_EXPERTS, nb), 0) == lax.broadcasted_iota(I32, (N_EXPERTS, nb), 1)
    first_blk = (pstarts >> MOE_SHIFT).astype(F32)
    n_blk = (padded >> MOE_SHIFT).astype(F32)
    meta_ref[0:1, :] = jnp.sum(jnp.where(own, first_blk, 0.0), axis=0, keepdims=True).astype(I32)
    meta_ref[1:2, :] = jnp.sum(jnp.where(own, n_blk, 0.0), axis=0, keepdims=True).astype(I32)
    meta_ref[2:3, :] = jnp.broadcast_to(pends[N_EXPERTS - 1:N_EXPERTS, :] >> MOE_SHIFT, (1, nb))


def _route(lgt):
    t = lgt.shape[1]
    return pl.pallas_call(
        _route_kernel,
        out_shape=[
            jax.ShapeDtypeStruct((2, t), I32),
            jax.ShapeDtypeStruct((2, t), F32),
            jax.ShapeDtypeStruct((3, LANES), I32),
        ],
        scratch_shapes=[pltpu.VMEM((N_EXPERTS, t), F32)],
        name="route",
    )(lgt)


def _expert_kernel(first_ref, nblk_ref, nact_ref, dest_ref, u_ref, wg_ref, wu_ref, wd_ref, ys_ref,
                   tok_ref, xbuf_ref, ybuf_ref, wgb_ref, wub_ref, wdb_ref, gsem, osem):
    e = pl.program_id(0)
    first = first_ref[e]
    nb = nblk_ref[e]
    ntok = u_ref.shape[0] // ROW_CHUNKS
    nslot = tok_ref.shape[0]
    blk_rows = MOE_BLOCK * ROW_CHUNKS

    def row_copy(blk, slot, r):
        tok = tok_ref[blk * MOE_BLOCK + r]
        src = u_ref.at[pl.ds(pl.multiple_of(tok * ROW_CHUNKS, ROW_CHUNKS), ROW_CHUNKS)]
        dst = xbuf_ref.at[pl.ds(pl.multiple_of(slot * blk_rows + r * ROW_CHUNKS, ROW_CHUNKS), ROW_CHUNKS)]
        return pltpu.make_async_copy(src, dst, gsem.at[slot])

    def start_rows(blk, slot):
        def body(r, carry):
            row_copy(blk, slot, r).start()
            return carry
        lax.fori_loop(0, MOE_BLOCK, body, 0, unroll=8)

    def wait_rows(blk, slot):
        def body(r, carry):
            row_copy(blk, slot, r).wait()
            return carry
        lax.fori_loop(0, MOE_BLOCK, body, 0, unroll=8)

    def out_copy(blk, slot):
        src = ybuf_ref.at[pl.ds(pl.multiple_of(slot * blk_rows, blk_rows), blk_rows)]
        dst = ys_ref.at[pl.ds(pl.multiple_of(blk * blk_rows, blk_rows), blk_rows)]
        return pltpu.make_async_copy(src, dst, osem.at[slot])

    @pl.when(e == 0)
    def _():
        def clear(p, carry):
            tok_ref[p] = 0
            return carry
        lax.fori_loop(0, nslot, clear, 0, unroll=8)

        def put(t, carry):
            for kk in range(TOP_K_IN_GROUP):
                tok_ref[dest_ref[kk * ntok + t]] = t
            return carry
        lax.fori_loop(0, ntok, put, 0, unroll=8)

    @pl.when(nb > 0)
    def _():
        start_rows(first, 0)

    wgb_ref[...] = wg_ref[...].astype(BF16)
    wub_ref[...] = wu_ref[...].astype(BF16)
    wdb_ref[...] = wd_ref[...].astype(BF16)

    def block(b, carry):
        slot = b % 2
        blk = first + b

        @pl.when(b + 1 < nb)
        def _():
            start_rows(blk + 1, 1 - slot)

        wait_rows(blk, slot)
        xb = _load_token_major(xbuf_ref, slot * blk_rows, MOE_BLOCK).astype(BF16)
        hg = jnp.dot(xb, wgb_ref[...], preferred_element_type=F32)
        hu = jnp.dot(xb, wub_ref[...], preferred_element_type=F32)
        h = (hg * jax.nn.sigmoid(hg)) * hu
        y = jnp.dot(h.astype(BF16), wdb_ref[...], preferred_element_type=F32)

        @pl.when(b >= 2)
        def _():
            out_copy(blk - 2, slot).wait()

        _store_token_major(ybuf_ref, slot * blk_rows, y)
        out_copy(blk, slot).start()
        return carry

    lax.fori_loop(0, nb, block, 0)

    @pl.when(nb >= 2)
    def _():
        out_copy(first + nb - 2, nb % 2).wait()

    @pl.when(nb >= 1)
    def _():
        out_copy(first + nb - 1, (nb - 1) % 2).wait()

    @pl.when(e == pl.num_programs(0) - 1)
    def _():
        ntotal = ys_ref.shape[0] // blk_rows
        ybuf_ref[0:blk_rows, :] = jnp.zeros((blk_rows, LANES), ybuf_ref.dtype)

        def fill_start(blk, carry):
            out_copy(blk, 0).start()
            return carry

        def fill_wait(blk, carry):
            out_copy(blk, 0).wait()
            return carry

        lax.fori_loop(nact_ref[0], ntotal, fill_start, 0)
        lax.fori_loop(nact_ref[0], ntotal, fill_wait, 0)


def _experts(first_blk, n_blk, nact, dest_flat, u2, w_gate, w_up, w_down, cap):
    d = D_MODEL
    wsel = lambda e, fb, nb, na, de: (e, 0, 0)
    grid_spec = pltpu.PrefetchScalarGridSpec(
        num_scalar_prefetch=4,
        grid=(N_EXPERTS,),
        in_specs=[
            pl.BlockSpec(memory_space=pl.ANY),
            pl.BlockSpec((None, d, EXPERT_FF), wsel),
            pl.BlockSpec((None, d, EXPERT_FF), wsel),
            pl.BlockSpec((None, EXPERT_FF, d), wsel),
        ],
        out_specs=pl.BlockSpec(memory_space=pl.ANY),
        scratch_shapes=[
            pltpu.SMEM((cap,), I32),
            pltpu.VMEM((2 * MOE_BLOCK * ROW_CHUNKS, LANES), F32),
            pltpu.VMEM((2 * MOE_BLOCK * ROW_CHUNKS, LANES), F32),
            pltpu.VMEM((d, EXPERT_FF), BF16),
            pltpu.VMEM((d, EXPERT_FF), BF16),
            pltpu.VMEM((EXPERT_FF, d), BF16),
            pltpu.SemaphoreType.DMA((2,)),
            pltpu.SemaphoreType.DMA((2,)),
        ],
    )
    return pl.pallas_call(
        _expert_kernel,
        grid_spec=grid_spec,
        out_shape=jax.ShapeDtypeStruct((cap * ROW_CHUNKS, LANES), F32),
        compiler_params=pltpu.CompilerParams(dimension_semantics=("arbitrary",), has_side_effects=True),
        name="experts",
    )(first_blk, n_blk, nact, dest_flat, u2, w_gate, w_up, w_down)


COMB_TOK = 64


def _combine_kernel(dest_ref, ys_ref, x1_ref, gw_ref, gf_ref, o_ref, buf_ref, sem):
    i = pl.program_id(0)
    n = pl.num_programs(0)
    ntok = n * COMB_TOK

    def buf_base(slot, kk):
        return (slot * TOP_K_IN_GROUP + kk) * (COMB_TOK * ROW_CHUNKS)

    def copy(step, slot, tl, kk):
        d = dest_ref[kk * ntok + step * COMB_TOK + tl]
        src = ys_ref.at[pl.ds(pl.multiple_of(d * ROW_CHUNKS, ROW_CHUNKS), ROW_CHUNKS)]
        dst = buf_ref.at[pl.ds(pl.multiple_of(buf_base(slot, kk) + tl * ROW_CHUNKS, ROW_CHUNKS), ROW_CHUNKS)]
        return pltpu.make_async_copy(src, dst, sem.at[slot])

    def start_all(step, slot):
        def body(tl, carry):
            for kk in range(TOP_K_IN_GROUP):
                copy(step, slot, tl, kk).start()
            return carry
        lax.fori_loop(0, COMB_TOK, body, 0)

    def wait_all(step, slot):
        def body(tl, carry):
            for kk in range(TOP_K_IN_GROUP):
                copy(step, slot, tl, kk).wait()
            return carry
        lax.fori_loop(0, COMB_TOK, body, 0)

    @pl.when(i == 0)
    def _():
        start_all(0, 0)

    @pl.when(i + 1 < n)
    def _():
        start_all(i + 1, (i + 1) % 2)

    slot = i % 2
    wait_all(i, slot)
    y0 = _load_token_major(buf_ref, buf_base(slot, 0), COMB_TOK)
    y1 = _load_token_major(buf_ref, buf_base(slot, 1), COMB_TOK)
    y = gw_ref[:, 0:1] * y0 + gw_ref[:, 1:2] * y1
    x = x1_ref[...] + y
    ms = jnp.mean(x * x, axis=-1, keepdims=True)
    o_ref[...] = x * lax.rsqrt(ms + EPS) * gf_ref[...]


def _combine(dest, ys, x1, gw_t, gf):
    t, d = x1.shape
    return pl.pallas_call(
        _combine_kernel,
        grid=(t // COMB_TOK,),
        in_specs=[
            pl.BlockSpec(memory_space=pltpu.SMEM),
            pl.BlockSpec(memory_space=pl.ANY),
            pl.BlockSpec((COMB_TOK, d), lambda i: (i, 0)),
            pl.BlockSpec((COMB_TOK, TOP_K_IN_GROUP), lambda i: (i, 0)),
            pl.BlockSpec((1, d), lambda i: (0, 0)),
        ],
        out_specs=pl.BlockSpec((COMB_TOK, d), lambda i: (i, 0)),
        out_shape=jax.ShapeDtypeStruct((t, d), F32),
        scratch_shapes=[
            pltpu.VMEM((2 * TOP_K_IN_GROUP * COMB_TOK * ROW_CHUNKS, LANES), F32),
            pltpu.SemaphoreType.DMA((2,)),
        ],
        compiler_params=pltpu.CompilerParams(dimension_semantics=("arbitrary",)),
        name="combine",
    )(dest, ys, x1, gw_t, gf)


def kernel(x, norm1_g, w_in, conv_w, conv_b, lru_wa, lru_ba, lru_wx, lru_bx, lru_lambda, w_proj_attn,
           w_proj_rnn, w_out, rel_bias, norm2_g, w_group, b_group, w_expert_router, b_expert_router,
           w_gate, w_up, w_down, norm_f_g):
    batch, seq, d = x.shape
    tokens = batch * seq
    nblk = seq // QB
    rep = ATTN_HEADS // KV_HEADS
    x2 = x.reshape(tokens, d)

    w = w_in[0]
    w_a = w[:, :PA_COLS].astype(BF16)
    w_c = jnp.concatenate(
        [w[:, OFF_XR:], w[:, OFF_KI:OFF_XR], jnp.zeros((d, PC_COLS - PC_MAIN - (OFF_XR - OFF_KI)), w.dtype)],
        axis=1).astype(BF16)
    g1 = norm1_g[0].reshape(1, d)
    pa = _norm_proj(x2, g1, w_a, BF16, 1024, 512)
    pc = _norm_proj(x2, g1, w_c, F32, 1024, 1280)

    q = pa[:, OFF_Q:OFF_K].reshape(batch, nblk, QB, KV_HEADS, rep, HEAD_DIM)
    qt = q.transpose(0, 1, 3, 5, 4, 2).reshape(batch, nblk, KV_HEADS, HEAD_DIM, rep * QB)
    k = pa[:, OFF_K:OFF_V].reshape(batch, seq, KV_WIDTH)
    v = pa[:, OFF_V:OFF_QI].reshape(batch, seq // KT, KT, KV_HEADS, HEAD_DIM)
    vt = v.transpose(0, 3, 1, 4, 2)
    qi = pa[:, OFF_QI:OFF_KI].reshape(batch, nblk, QB, IDX_HEADS, IDX_DIM)
    qit = qi.transpose(0, 1, 4, 3, 2).reshape(batch, nblk, IDX_DIM, IDX_HEADS * QB)
    ki = pc[:, PC_MAIN:PC_MAIN + IDX_DIM].astype(BF16).reshape(batch, seq, IDX_DIM)
    wi = pc[:, PC_MAIN + IDX_DIM:PC_MAIN + IDX_DIM + IDX_HEADS].reshape(batch, nblk, QB, IDX_HEADS)
    wit = wi.transpose(0, 1, 3, 2).reshape(batch, nblk, 1, IDX_HEADS * QB)
    ks = np.arange(QB)[:, None]
    qs = np.arange(QB)[None, :]
    bkt = jnp.asarray(np.stack([_rel_bucket_np(qs - ks + QB * dd) for dd in range(2)]))
    attn = _dsa_attention(rel_bias, qit, wit, ki, k, qt, vt, bkt, batch, seq).reshape(tokens, ATTN_WIDTH)

    rnn = _rglru(pc, conv_w[0], conv_b[0].reshape(1, RNN_WIDTH), lru_wa[0].astype(BF16), lru_ba[0],
                 lru_wx[0].astype(BF16), lru_bx[0], lru_lambda[0].reshape(1, RNN_WIDTH), batch, seq)

    w_r = jnp.concatenate([w_group[0], w_expert_router[0],
                           jnp.zeros((d, LANES - N_GROUPS - N_EXPERTS), F32)], axis=1)
    w_rh = w_r.astype(BF16)
    w_rl = (w_r - w_rh.astype(F32)).astype(BF16)
    b_r = jnp.concatenate([b_group[0], b_expert_router[0],
                           jnp.zeros((LANES - N_GROUPS - N_EXPERTS,), F32)]).reshape(1, LANES)
    x1, u2, lgt = _merge(attn, rnn, pc, x2, w_proj_attn[0].astype(BF16), w_proj_rnn[0].astype(BF16),
                         w_out[0].astype(BF16), norm2_g[0].reshape(1, d), w_rh, w_rl, b_r)

    n_slots = tokens * TOP_K_IN_GROUP
    cap = -(-(n_slots + N_EXPERTS * (MOE_BLOCK - 1)) // MOE_BLOCK) * MOE_BLOCK
    dest, gw, meta = _route(lgt)

    dest_flat = dest.reshape(-1)
    ys = _experts(meta[0, :N_EXPERTS], meta[1, :N_EXPERTS], meta[2, :1], dest_flat, u2,
                  w_gate[0], w_up[0], w_down[0], cap)
    out = _combine(dest_flat, ys, x1, gw.T, norm_f_g.reshape(1, d))
    return out.reshape(batch, seq, d)
```

```python
import functools

import numpy as np
import jax
import jax.numpy as jnp
from jax import lax
from jax.experimental import pallas as pl
from jax.experimental.pallas import tpu as pltpu

D_MODEL = 2048
ATTN_HEADS = 8
KV_HEADS = 2
HEAD_DIM = 128
ATTN_WIDTH = ATTN_HEADS * HEAD_DIM
KV_WIDTH = KV_HEADS * HEAD_DIM
IDX_HEADS = 16
IDX_DIM = 64
TOPK_MAX = 256
RNN_WIDTH = 1024
RNN_BLOCKS = 8
RNN_BLOCK_W = RNN_WIDTH // RNN_BLOCKS
CONV_WIDTH = 4
LRU_C = 8.0
REL_BUCKETS = 32
REL_MAX_DIST = 128
N_GROUPS = 8
EXPERTS_PER_GROUP = 8
N_EXPERTS = N_GROUPS * EXPERTS_PER_GROUP
TOP_K_IN_GROUP = 2
EXPERT_FF = 512
MOE_BLOCK = 128
MOE_SHIFT = 7
EPS = 1e-6

LANES = 128
QB = 128
KT = 256
NEG_BIG = -1e30
INT_MIN = -(2 ** 31)
LOG2E = np.float32(np.log2(np.e))
F32 = jnp.float32
BF16 = jnp.bfloat16
I32 = jnp.int32

OFF_Q = 0
OFF_K = OFF_Q + ATTN_WIDTH
OFF_V = OFF_K + KV_WIDTH
OFF_QI = OFF_V + KV_WIDTH
OFF_KI = OFF_QI + IDX_HEADS * IDX_DIM
OFF_WI = OFF_KI + IDX_DIM
OFF_XR = OFF_WI + IDX_HEADS
OFF_RG = OFF_XR + RNN_WIDTH
OFF_GA = OFF_RG + RNN_WIDTH
OFF_GR = OFF_GA + D_MODEL
IN_COLS = OFF_GR + D_MODEL
PA_COLS = OFF_KI
PC_MAIN = IN_COLS - OFF_XR
PC_COLS = 6400


def _rel_bucket_np(n):
    n = np.maximum(n, 0)
    max_exact = REL_BUCKETS // 2
    nf = np.maximum(n, 1).astype(np.float32)
    large = max_exact + (np.log(nf / np.float32(max_exact)) / np.float32(np.log(REL_MAX_DIST / max_exact))
                         * np.float32(REL_BUCKETS - max_exact)).astype(np.int32)
    large = np.minimum(large, REL_BUCKETS - 1)
    return np.where(n < max_exact, n, large).astype(np.int32)


def _norm_proj_kernel(x_ref, g_ref, w_ref, o_ref, u_ref):
    @pl.when(pl.program_id(1) == 0)
    def _():
        x = x_ref[...]
        ms = jnp.mean(x * x, axis=-1, keepdims=True)
        u_ref[...] = (x * lax.rsqrt(ms + EPS) * g_ref[...]).astype(BF16)

    o_ref[...] = jnp.dot(u_ref[...], w_ref[...], preferred_element_type=F32).astype(o_ref.dtype)


def _norm_proj(x2, g, w, out_dtype, tm, tn):
    m, d = x2.shape
    n = w.shape[1]
    return pl.pallas_call(
        _norm_proj_kernel,
        grid=(m // tm, n // tn),
        in_specs=[
            pl.BlockSpec((tm, d), lambda i, j: (i, 0)),
            pl.BlockSpec((1, d), lambda i, j: (0, 0)),
            pl.BlockSpec((d, tn), lambda i, j: (0, j)),
        ],
        out_specs=pl.BlockSpec((tm, tn), lambda i, j: (i, j)),
        out_shape=jax.ShapeDtypeStruct((m, n), out_dtype),
        scratch_shapes=[pltpu.VMEM((tm, d), BF16)],
        compiler_params=pltpu.CompilerParams(dimension_semantics=("arbitrary", "arbitrary")),
        name="norm_proj",
    )(x2, g, w)


def _dsa_kernel(relb_ref, qit_ref, wit_ref, ki_ref, k_ref, qt_ref, vt_ref, bkt_ref, o_ref,
                keys_ref, bias_ref, m_ref, l_ref, acc_ref):
    b = pl.program_id(0)
    j = pl.program_id(1)

    @pl.when((b == 0) & (j == 0))
    def _():
        for d in range(2):
            bk = bkt_ref[d]
            for h in range(ATTN_HEADS):
                tile = jnp.zeros((QB, QB), F32)
                for bb in range(REL_BUCKETS):
                    tile = jnp.where(bk == bb, relb_ref[bb, h] * LOG2E, tile)
                bias_ref[d, h] = tile

    t0 = j * QB
    ntile = (j + 2) // 2
    q_pos = t0 + lax.broadcasted_iota(I32, (KT, QB), 1)
    row_iota = lax.broadcasted_iota(I32, (KT, QB), 0)

    w_scale = np.float32(IDX_HEADS ** -0.5) * np.float32(IDX_DIM ** -0.5)
    wrow = wit_ref[...] * w_scale

    def score_tile(kt, carry):
        r0 = pl.multiple_of(kt * KT, KT)
        ki_t = ki_ref[pl.ds(r0, KT), :]
        acc = jnp.zeros((KT, QB), F32)
        for c in range(4):
            dots = jnp.dot(ki_t, qit_ref[:, c * 512:(c + 1) * 512], preferred_element_type=F32)
            for hh in range(4):
                h = c * 4 + hh
                acc = acc + jnp.maximum(dots[:, hh * QB:(hh + 1) * QB], 0.0) * wrow[:, h * QB:(h + 1) * QB]
        bits = pltpu.bitcast(acc, I32)
        key = bits ^ ((bits >> 31) & jnp.int32(0x7FFFFFFF))
        key = jnp.where(r0 + row_iota <= q_pos, key, jnp.int32(INT_MIN))
        keys_ref[pl.ds(r0, KT), :] = key
        return carry

    lax.fori_loop(0, ntile, score_tile, 0)

    n_acc = 4

    def count_ge(cand):
        cand8 = jnp.broadcast_to(cand, (8, QB))

        def body(kt, accs):
            r0 = pl.multiple_of(kt * KT, KT)
            accs = list(accs)
            for v in range(KT // 8):
                blk = keys_ref[pl.ds(r0 + v * 8, 8), :]
                accs[v % n_acc] = accs[v % n_acc] + jnp.where(blk >= cand8, jnp.int32(1), jnp.int32(0))
            return tuple(accs)

        accs = lax.fori_loop(0, ntile, body, (jnp.zeros((8, QB), I32),) * n_acc)
        acc8 = (accs[0] + accs[1]) + (accs[2] + accs[3])
        return jnp.sum(acc8.astype(F32), axis=0, keepdims=True).astype(I32)

    c0 = count_ge(jnp.zeros((1, QB), I32))
    thr0 = jnp.where(c0 >= TOPK_MAX, jnp.int32(0), jnp.int32(INT_MIN))

    def bit_step(i, thr):
        cand = thr | (jnp.int32(1) << (30 - i))
        c = count_ge(cand)
        return jnp.where(c >= TOPK_MAX, cand, thr)

    thr = lax.fori_loop(0, 31, bit_step, thr0)
    thr = jnp.maximum(thr, jnp.int32(INT_MIN + 1))

    qk_scale = np.float32(HEAD_DIM ** -0.5) * LOG2E
    rep = ATTN_HEADS // KV_HEADS
    nfar = jnp.maximum((j - 1) // 2, 0)

    m_ref[...] = jnp.full(m_ref.shape, NEG_BIG, F32)
    l_ref[...] = jnp.zeros(l_ref.shape, F32)
    acc_ref[...] = jnp.zeros(acc_ref.shape, F32)

    def attend(kt, g, near):
        r0 = pl.multiple_of(kt * KT, KT)
        k_t = k_ref[pl.ds(r0, KT), g * HEAD_DIM:(g + 1) * HEAD_DIM]
        lg = jnp.dot(k_t, qt_ref[g], preferred_element_type=F32) * qk_scale
        sel = keys_ref[pl.ds(r0, KT), :] >= thr
        cols = []
        for r in range(rep):
            h = g * rep + r
            far_bias = relb_ref[REL_BUCKETS - 1, h] * LOG2E
            blk = lg[:, r * QB:(r + 1) * QB]
            if near:
                subs = []
                for sub in range(KT // QB):
                    dblk = j - (kt * (KT // QB) + sub)
                    bias = jnp.where(dblk == 0, bias_ref[0, h],
                                     jnp.where(dblk == 1, bias_ref[1, h], far_bias))
                    subs.append(blk[sub * QB:(sub + 1) * QB, :] + bias)
                blk = jnp.concatenate(subs, axis=0)
            else:
                blk = blk + far_bias
            cols.append(jnp.where(sel, blk, NEG_BIG))
        s = jnp.concatenate(cols, axis=1)
        m_old = m_ref[g]
        m_new = jnp.maximum(m_old, jnp.max(s, axis=0, keepdims=True))
        alpha = jnp.exp2(m_old - m_new)
        p = jnp.exp2(s - m_new)
        l_ref[g] = alpha * l_ref[g] + jnp.sum(p, axis=0, keepdims=True)
        pv = jnp.dot(vt_ref[g, kt], p.astype(BF16), preferred_element_type=F32)
        acc_ref[g] = acc_ref[g] * alpha + pv
        m_ref[g] = m_new

    def far_body(kt, carry):
        for g in range(KV_HEADS):
            attend(kt, g, False)
        return carry

    def near_body(kt, carry):
        for g in range(KV_HEADS):
            attend(kt, g, True)
        return carry

    lax.fori_loop(0, nfar, far_body, 0)
    lax.fori_loop(nfar, ntile, near_body, 0)

    for g in range(KV_HEADS):
        out_t = acc_ref[g] / l_ref[g]
        for r in range(rep):
            h = g * rep + r
            o_ref[:, h * HEAD_DIM:(h + 1) * HEAD_DIM] = out_t[:, r * QB:(r + 1) * QB].T.astype(o_ref.dtype)


def _dsa_attention(rel_bias, qit, wit, ki, k, qt, vt, bkt, batch, seq):
    nblk = seq // QB
    rep = ATTN_HEADS // KV_HEADS
    return pl.pallas_call(
        _dsa_kernel,
        grid=(batch, nblk),
        in_specs=[
            pl.BlockSpec(memory_space=pltpu.SMEM),
            pl.BlockSpec((None, None, IDX_DIM, IDX_HEADS * QB), lambda b, j: (b, j, 0, 0)),
            pl.BlockSpec((None, None, 1, IDX_HEADS * QB), lambda b, j: (b, j, 0, 0)),
            pl.BlockSpec((None, seq, IDX_DIM), lambda b, j: (b, 0, 0)),
            pl.BlockSpec((None, seq, KV_WIDTH), lambda b, j: (b, 0, 0)),
            pl.BlockSpec((None, None, KV_HEADS, HEAD_DIM, rep * QB), lambda b, j: (b, j, 0, 0, 0)),
            pl.BlockSpec((None, KV_HEADS, seq // KT, HEAD_DIM, KT), lambda b, j: (b, 0, 0, 0, 0)),
            pl.BlockSpec((2, QB, QB), lambda b, j: (0, 0, 0)),
        ],
        out_specs=pl.BlockSpec((None, QB, ATTN_WIDTH), lambda b, j: (b, j, 0)),
        out_shape=jax.ShapeDtypeStruct((batch, seq, ATTN_WIDTH), BF16),
        scratch_shapes=[
            pltpu.VMEM((seq, QB), I32),
            pltpu.VMEM((2, ATTN_HEADS, QB, QB), F32),
            pltpu.VMEM((KV_HEADS, 1, rep * QB), F32),
            pltpu.VMEM((KV_HEADS, 1, rep * QB), F32),
            pltpu.VMEM((KV_HEADS, HEAD_DIM, rep * QB), F32),
        ],
        compiler_params=pltpu.CompilerParams(dimension_semantics=("arbitrary", "arbitrary")),
        name="dsa_attention",
    )(rel_bias, qit, wit, ki, k, qt, vt, bkt)


RNN_TS = 256


def _gelu_tanh(x):
    c = np.float32(np.sqrt(2.0 / np.pi))
    return x * (0.5 * (1.0 + jnp.tanh(c * (x + np.float32(0.044715) * (x * x * x)))))


def _softplus(z):
    return jnp.maximum(z, 0.0) + jnp.log1p(jnp.exp(-jnp.abs(z)))


def _rglru_kernel(xr_ref, gate_ref, cw_ref, cb_ref, wa_ref, ba_ref, wx_ref, bx_ref, lam_ref, o_ref,
                  xext_ref, h_ref):
    i = pl.program_id(1)
    ts = RNN_TS

    @pl.when(i == 0)
    def _():
        xext_ref[0:8, :] = jnp.zeros((8, RNN_WIDTH), F32)
        h_ref[...] = jnp.zeros((1, RNN_WIDTH), F32)

    @pl.when(i > 0)
    def _():
        xext_ref[0:8, :] = xext_ref[ts:ts + 8, :]

    xext_ref[8:8 + ts, :] = xr_ref[...]

    row = lax.broadcasted_iota(I32, (ts, RNN_BLOCK_W), 0)
    for n in range(RNN_BLOCKS):
        cs = slice(n * RNN_BLOCK_W, (n + 1) * RNN_BLOCK_W)
        xc = cb_ref[:, cs]
        for jj in range(CONV_WIDTH):
            off = 8 - (CONV_WIDTH - 1) + jj
            xc = xc + xext_ref[off:off + ts, cs] * cw_ref[jj:jj + 1, cs]
        xcb = xc.astype(BF16)
        r = jax.nn.sigmoid(jnp.dot(xcb, wa_ref[n], preferred_element_type=F32) + ba_ref[n:n + 1, :])
        gi = jax.nn.sigmoid(jnp.dot(xcb, wx_ref[n], preferred_element_type=F32) + bx_ref[n:n + 1, :])
        log_a = (-LRU_C * r) * _softplus(-lam_ref[:, cs])
        a = jnp.exp(log_a)
        mult = jnp.sqrt(1.0 - jnp.exp(2.0 * log_a))
        bt = mult * (gi * xc)
        k = 1
        while k < ts:
            a_s = pltpu.roll(a, k, 0)
            b_s = pltpu.roll(bt, k, 0)
            keep = row >= k
            bt = jnp.where(keep, a * b_s + bt, bt)
            a = jnp.where(keep, a * a_s, a)
            k *= 2
        h = bt + a * h_ref[:, cs]
        h_ref[:, cs] = h[ts - 1:ts, :]
        o_ref[:, cs] = (h * _gelu_tanh(gate_ref[:, cs])).astype(o_ref.dtype)


def _rglru(pc, conv_w, conv_b, wa, ba, wx, bx, lam, batch, seq):
    nts = seq // RNN_TS
    full = lambda shape: pl.BlockSpec(shape, lambda b, i: (0,) * len(shape))
    return pl.pallas_call(
        _rglru_kernel,
        grid=(batch, nts),
        in_specs=[
            pl.BlockSpec((RNN_TS, RNN_WIDTH), lambda b, i: (b * nts + i, 0)),
            pl.BlockSpec((RNN_TS, RNN_WIDTH), lambda b, i: (b * nts + i, 1)),
            full((CONV_WIDTH, RNN_WIDTH)),
            full((1, RNN_WIDTH)),
            full((RNN_BLOCKS, RNN_BLOCK_W, RNN_BLOCK_W)),
            full((RNN_BLOCKS, RNN_BLOCK_W)),
            full((RNN_BLOCKS, RNN_BLOCK_W, RNN_BLOCK_W)),
            full((RNN_BLOCKS, RNN_BLOCK_W)),
            full((1, RNN_WIDTH)),
        ],
        out_specs=pl.BlockSpec((RNN_TS, RNN_WIDTH), lambda b, i: (b * nts + i, 0)),
        out_shape=jax.ShapeDtypeStruct((batch * seq, RNN_WIDTH), BF16),
        scratch_shapes=[pltpu.VMEM((RNN_TS + 8, RNN_WIDTH), F32), pltpu.VMEM((1, RNN_WIDTH), F32)],
        compiler_params=pltpu.CompilerParams(dimension_semantics=("arbitrary", "arbitrary")),
        name="rglru",
    )(pc, pc, conv_w, conv_b, wa, ba, wx, bx, lam)


MERGE_TM = 256
ROW_CHUNKS = D_MODEL // LANES


def _store_token_major(ref, base, val):
    n = val.shape[0]
    for c in range(ROW_CHUNKS):
        ref[pl.ds(base + c, n, stride=ROW_CHUNKS), :] = val[:, c * LANES:(c + 1) * LANES]


def _load_token_major(ref, base, n):
    return jnp.concatenate([ref[pl.ds(base + c, n, stride=ROW_CHUNKS), :] for c in range(ROW_CHUNKS)], axis=1)


def _merge_kernel(attn_ref, rnn_ref, ga_ref, gr_ref, x_ref, wpa_ref, wpr_ref, wo_ref, g2_ref,
                  wrh_ref, wrl_ref, br_ref, x1_ref, u2_ref, lgt_ref):
    pa = jnp.dot(attn_ref[...], wpa_ref[...], preferred_element_type=F32)
    pr = jnp.dot(rnn_ref[...], wpr_ref[...], preferred_element_type=F32)
    merged = jax.nn.sigmoid(ga_ref[...]) * pa + jax.nn.sigmoid(gr_ref[...]) * pr
    x1 = x_ref[...] + jnp.dot(merged.astype(BF16), wo_ref[...], preferred_element_type=F32)
    x1_ref[...] = x1
    ms = jnp.mean(x1 * x1, axis=-1, keepdims=True)
    u2 = x1 * lax.rsqrt(ms + EPS) * g2_ref[...]
    _store_token_major(u2_ref, 0, u2)
    hi = u2.astype(BF16)
    lo = (u2 - hi.astype(F32)).astype(BF16)
    lg = (jnp.dot(hi, wrh_ref[...], preferred_element_type=F32)
          + jnp.dot(lo, wrh_ref[...], preferred_element_type=F32)
          + jnp.dot(hi, wrl_ref[...], preferred_element_type=F32)) + br_ref[...]
    lgt_ref[...] = lg.T


def _merge(attn, rnn, pc, x2, wpa, wpr, wo, g2, wrh, wrl, br):
    m = x2.shape[0]
    tm = MERGE_TM
    const = lambda shape: pl.BlockSpec(shape, lambda i: (0,) * len(shape), pipeline_mode=pl.Buffered(1))
    return pl.pallas_call(
        _merge_kernel,
        grid=(m // tm,),
        in_specs=[
            pl.BlockSpec((tm, ATTN_WIDTH), lambda i: (i, 0)),
            pl.BlockSpec((tm, RNN_WIDTH), lambda i: (i, 0)),
            pl.BlockSpec((tm, D_MODEL), lambda i: (i, 1)),
            pl.BlockSpec((tm, D_MODEL), lambda i: (i, 2)),
            pl.BlockSpec((tm, D_MODEL), lambda i: (i, 0)),
            const((ATTN_WIDTH, D_MODEL)),
            const((RNN_WIDTH, D_MODEL)),
            const((D_MODEL, D_MODEL)),
            const((1, D_MODEL)),
            const((D_MODEL, LANES)),
            const((D_MODEL, LANES)),
            const((1, LANES)),
        ],
        out_specs=[
            pl.BlockSpec((tm, D_MODEL), lambda i: (i, 0)),
            pl.BlockSpec((tm * ROW_CHUNKS, LANES), lambda i: (i, 0)),
            pl.BlockSpec((LANES, tm), lambda i: (0, i)),
        ],
        out_shape=[
            jax.ShapeDtypeStruct((m, D_MODEL), F32),
            jax.ShapeDtypeStruct((m * ROW_CHUNKS, LANES), F32),
            jax.ShapeDtypeStruct((LANES, m), F32),
        ],
        compiler_params=pltpu.CompilerParams(dimension_semantics=("arbitrary",)),
        name="merge_outproj",
    )(attn, rnn, pc, pc, x2, wpa, wpr, wo, g2, wrh, wrl, br)


ROUTE_CHUNK = 256


def _first_index_of_max(v, ridx, n):
    vmax = jnp.max(v, axis=0, keepdims=True)
    idx = jnp.min(jnp.where(v == vmax, ridx, jnp.int32(n)).astype(F32), axis=0, keepdims=True)
    return vmax, idx.astype(I32)


def _route_kernel(lgt_ref, dest_ref, gw_ref, meta_ref, cum_ref):
    t = lgt_ref.shape[1]
    eg = EXPERTS_PER_GROUP
    ridx8 = lax.broadcasted_iota(I32, (eg, t), 0)
    gl = lgt_ref[0:N_GROUPS, :]
    gmax, g_sel = _first_index_of_max(gl, ridx8, N_GROUPS)
    p_sel = 1.0 / jnp.sum(jnp.exp(gl - gmax), axis=0, keepdims=True)
    el = lgt_ref[N_GROUPS:N_GROUPS + eg, :]
    for g in range(1, N_GROUPS):
        el = jnp.where(g_sel == g, lgt_ref[N_GROUPS + g * eg:N_GROUPS + (g + 1) * eg, :], el)
    v0, i0 = _first_index_of_max(el, ridx8, eg)
    el1 = jnp.where(ridx8 == i0, -jnp.inf, el)
    v1, i1 = _first_index_of_max(el1, ridx8, eg)
    e1 = jnp.exp(v1 - v0)
    den = 1.0 + e1
    gw_ref[0:1, :] = p_sel * (1.0 / den)
    gw_ref[1:2, :] = p_sel * (e1 / den)
    ex0 = g_sel * eg + i0
    ex1 = g_sel * eg + i1

    eidx = lax.broadcasted_iota(I32, (N_EXPERTS, ROUTE_CHUNK), 0)
    ui = lax.broadcasted_iota(I32, (ROUTE_CHUNK, ROUTE_CHUNK), 0)
    uj = lax.broadcasted_iota(I32, (ROUTE_CHUNK, ROUTE_CHUNK), 1)
    upper = jnp.where(ui < uj, 1.0, 0.0).astype(BF16)
    run = jnp.zeros((N_EXPERTS, 1), F32)
    for c in range(t // ROUTE_CHUNK):
        cs = slice(c * ROUTE_CHUNK, (c + 1) * ROUTE_CHUNK)
        hit = jnp.where(eidx == ex0[:, cs], 1.0, jnp.where(eidx == ex1[:, cs], 1.0, 0.0))
        cum_ref[:, cs] = jnp.dot(hit.astype(BF16), upper, preferred_element_type=F32) + run
        run = run + jnp.sum(hit, axis=1, keepdims=True)

    counts = run.astype(I32)
    padded = ((counts + (MOE_BLOCK - 1)) >> MOE_SHIFT) << MOE_SHIFT
    pe = jnp.broadcast_to(padded, (N_EXPERTS, LANES))
    erow = lax.broadcasted_iota(I32, (N_EXPERTS, LANES), 0)
    k = 1
    while k < N_EXPERTS:
        pe = pe + jnp.where(erow >= k, pltpu.roll(pe, k, 0), 0)
        k *= 2
    pends = pe[:, 0:1]
    pstarts = pends - padded

    eidx_t = lax.broadcasted_iota(I32, (N_EXPERTS, t), 0)
    slot = cum_ref[...] + pstarts.astype(F32)
    dest_ref[0:1, :] = jnp.sum(jnp.where(eidx_t == ex0, slot, 0.0), axis=0, keepdims=True).astype(I32)
    dest_ref[1:2, :] = jnp.sum(jnp.where(eidx_t == ex1, slot, 0.0), axis=0, keepdims=True).astype(I32)

    nb = meta_ref.shape[1]
    own = lax.broadcasted_iota(I32, (N_EXPERTS, nb), 0) == lax.broadcasted_iota(I32, (N_EXPERTS, nb), 1)
    first_blk = (pstarts >> MOE_SHIFT).astype(F32)
    n_blk = (padded >> MOE_SHIFT).astype(F32)
    meta_ref[0:1, :] = jnp.sum(jnp.where(own, first_blk, 0.0), axis=0, keepdims=True).astype(I32)
    meta_ref[1:2, :] = jnp.sum(jnp.where(own, n_blk, 0.0), axis=0, keepdims=True).astype(I32)
    meta_ref[2:3, :] = jnp.broadcast_to(pends[N_EXPERTS - 1:N_EXPERTS, :] >> MOE_SHIFT, (1, nb))


def _route(lgt):
    t = lgt.shape[1]
    return pl.pallas_call(
        _route_kernel,
        out_shape=[
            jax.ShapeDtypeStruct((2, t), I32),
            jax.ShapeDtypeStruct((2, t), F32),
            jax.ShapeDtypeStruct((3, LANES), I32),
        ],
        scratch_shapes=[pltpu.VMEM((N_EXPERTS, t), F32)],
        name="route",
    )(lgt)


W_CHUNKS = 8


def _expert_kernel(first_ref, nblk_ref, nact_ref, dest_ref, u_ref, wg_ref, wu_ref, wd_ref, ys_ref,
                   tok_ref, xbuf_ref, ybuf_ref, wgf_ref, wuf_ref, wdf_ref, wgb_ref, wub_ref, wdb_ref,
                   gsem, osem, wsem):
    e = pl.program_id(0)
    first = first_ref[e]
    nb = nblk_ref[e]
    ntok = u_ref.shape[0] // ROW_CHUNKS
    nslot = tok_ref.shape[0]
    blk_rows = MOE_BLOCK * ROW_CHUNKS

    def row_copy(blk, slot, r):
        tok = tok_ref[blk * MOE_BLOCK + r]
        src = u_ref.at[pl.ds(pl.multiple_of(tok * ROW_CHUNKS, ROW_CHUNKS), ROW_CHUNKS)]
        dst = xbuf_ref.at[pl.ds(pl.multiple_of(slot * blk_rows + r * ROW_CHUNKS, ROW_CHUNKS), ROW_CHUNKS)]
        return pltpu.make_async_copy(src, dst, gsem.at[slot])

    def start_rows(blk, slot):
        def body(r, carry):
            row_copy(blk, slot, r).start()
            return carry
        lax.fori_loop(0, MOE_BLOCK, body, 0, unroll=8)

    def wait_rows(blk, slot):
        def body(r, carry):
            row_copy(blk, slot, r).wait()
            return carry
        lax.fori_loop(0, MOE_BLOCK, body, 0, unroll=8)

    def out_copy(blk, slot):
        src = ybuf_ref.at[pl.ds(pl.multiple_of(slot * blk_rows, blk_rows), blk_rows)]
        dst = ys_ref.at[pl.ds(pl.multiple_of(blk * blk_rows, blk_rows), blk_rows)]
        return pltpu.make_async_copy(src, dst, osem.at[slot])

    @pl.when(e == 0)
    def _():
        def clear(p, carry):
            tok_ref[p] = 0
            return carry
        lax.fori_loop(0, nslot, clear, 0, unroll=8)

        def put(t, carry):
            for kk in range(TOP_K_IN_GROUP):
                tok_ref[dest_ref[kk * ntok + t]] = t
            return carry
        lax.fori_loop(0, ntok, put, 0, unroll=8)

    def weight_copies(ex, wslot):
        copies = []
        for src, dst in ((wg_ref, wgf_ref), (wu_ref, wuf_ref), (wd_ref, wdf_ref)):
            rows = src.shape[1] // W_CHUNKS
            for c in range(W_CHUNKS):
                rs = pl.ds(c * rows, rows)
                copies.append(pltpu.make_async_copy(src.at[ex, rs], dst.at[wslot, rs], wsem.at[wslot]))
        return copies

    @pl.when(e == 0)
    def _():
        for cp in weight_copies(0, 0):
            cp.start()

    @pl.when(e + 1 < pl.num_programs(0))
    def _():
        for cp in weight_copies(e + 1, (e + 1) % 2):
            cp.start()

    @pl.when(nb > 0)
    def _():
        start_rows(first, 0)

    wslot = e % 2
    for cp in weight_copies(e, wslot):
        cp.wait()
    wgb_ref[...] = wgf_ref[wslot].astype(BF16)
    wub_ref[...] = wuf_ref[wslot].astype(BF16)
    wdb_ref[...] = wdf_ref[wslot].astype(BF16)

    def block(b, carry):
        slot = b % 2
        blk = first + b

        @pl.when(b + 1 < nb)
        def _():
            start_rows(blk + 1, 1 - slot)

        wait_rows(blk, slot)
        xb = _load_token_major(xbuf_ref, slot * blk_rows, MOE_BLOCK).astype(BF16)
        hg = jnp.dot(xb, wgb_ref[...], preferred_element_type=F32)
        hu = jnp.dot(xb, wub_ref[...], preferred_element_type=F32)
        h = (hg * jax.nn.sigmoid(hg)) * hu
        y = jnp.dot(h.astype(BF16), wdb_ref[...], preferred_element_type=F32)

        @pl.when(b >= 2)
        def _():
            out_copy(blk - 2, slot).wait()

        _store_token_major(ybuf_ref, slot * blk_rows, y)
        out_copy(blk, slot).start()
        return carry

    lax.fori_loop(0, nb, block, 0)

    @pl.when(nb >= 2)
    def _():
        out_copy(first + nb - 2, nb % 2).wait()

    @pl.when(nb >= 1)
    def _():
        out_copy(first + nb - 1, (nb - 1) % 2).wait()

    @pl.when(e == pl.num_programs(0) - 1)
    def _():
        ntotal = ys_ref.shape[0] // blk_rows
        ybuf_ref[0:blk_rows, :] = jnp.zeros((blk_rows, LANES), ybuf_ref.dtype)

        def fill_start(blk, carry):
            out_copy(blk, 0).start()
            return carry

        def fill_wait(blk, carry):
            out_copy(blk, 0).wait()
            return carry

        lax.fori_loop(nact_ref[0], ntotal, fill_start, 0)
        lax.fori_loop(nact_ref[0], ntotal, fill_wait, 0)


def _experts(first_blk, n_blk, nact, dest_flat, u2, w_gate, w_up, w_down, cap):
    d = D_MODEL
    grid_spec = pltpu.PrefetchScalarGridSpec(
        num_scalar_prefetch=4,
        grid=(N_EXPERTS,),
        in_specs=[pl.BlockSpec(memory_space=pl.ANY)] * 4,
        out_specs=pl.BlockSpec(memory_space=pl.ANY),
        scratch_shapes=[
            pltpu.SMEM((cap,), I32),
            pltpu.VMEM((2 * MOE_BLOCK * ROW_CHUNKS, LANES), F32),
            pltpu.VMEM((2 * MOE_BLOCK * ROW_CHUNKS, LANES), F32),
            pltpu.VMEM((2, d, EXPERT_FF), F32),
            pltpu.VMEM((2, d, EXPERT_FF), F32),
            pltpu.VMEM((2, EXPERT_FF, d), F32),
            pltpu.VMEM((d, EXPERT_FF), BF16),
            pltpu.VMEM((d, EXPERT_FF), BF16),
            pltpu.VMEM((EXPERT_FF, d), BF16),
            pltpu.SemaphoreType.DMA((2,)),
            pltpu.SemaphoreType.DMA((2,)),
            pltpu.SemaphoreType.DMA((2,)),
        ],
    )
    return pl.pallas_call(
        _expert_kernel,
        grid_spec=grid_spec,
        out_shape=jax.ShapeDtypeStruct((cap * ROW_CHUNKS, LANES), F32),
        compiler_params=pltpu.CompilerParams(dimension_semantics=("arbitrary",), has_side_effects=True),
        name="experts",
    )(first_blk, n_blk, nact, dest_flat, u2, w_gate, w_up, w_down)


COMB_TOK = 64


def _combine_kernel(dest_ref, ys_ref, x1_ref, gw_ref, gf_ref, o_ref, buf_ref, sem):
    i = pl.program_id(0)
    n = pl.num_programs(0)
    ntok = n * COMB_TOK

    def buf_base(slot, kk):
        return (slot * TOP_K_IN_GROUP + kk) * (COMB_TOK * ROW_CHUNKS)

    def copy(step, slot, tl, kk):
        d = dest_ref[kk * ntok + step * COMB_TOK + tl]
        src = ys_ref.at[pl.ds(pl.multiple_of(d * ROW_CHUNKS, ROW_CHUNKS), ROW_CHUNKS)]
        dst = buf_ref.at[pl.ds(pl.multiple_of(buf_base(slot, kk) + tl * ROW_CHUNKS, ROW_CHUNKS), ROW_CHUNKS)]
        return pltpu.make_async_copy(src, dst, sem.at[slot])

    def start_all(step, slot):
        def body(tl, carry):
            for kk in range(TOP_K_IN_GROUP):
                copy(step, slot, tl, kk).start()
            return carry
        lax.fori_loop(0, COMB_TOK, body, 0)

    def wait_all(step, slot):
        def body(tl, carry):
            for kk in range(TOP_K_IN_GROUP):
                copy(step, slot, tl, kk).wait()
            return carry
        lax.fori_loop(0, COMB_TOK, body, 0)

    @pl.when(i == 0)
    def _():
        start_all(0, 0)

    @pl.when(i + 1 < n)
    def _():
        start_all(i + 1, (i + 1) % 2)

    slot = i % 2
    wait_all(i, slot)
    y0 = _load_token_major(buf_ref, buf_base(slot, 0), COMB_TOK)
    y1 = _load_token_major(buf_ref, buf_base(slot, 1), COMB_TOK)
    y = gw_ref[:, 0:1] * y0 + gw_ref[:, 1:2] * y1
    x = x1_ref[...] + y
    ms = jnp.mean(x * x, axis=-1, keepdims=True)
    o_ref[...] = x * lax.rsqrt(ms + EPS) * gf_ref[...]


def _combine(dest, ys, x1, gw_t, gf):
    t, d = x1.shape
    return pl.pallas_call(
        _combine_kernel,
        grid=(t // COMB_TOK,),
        in_specs=[
            pl.BlockSpec(memory_space=pltpu.SMEM),
            pl.BlockSpec(memory_space=pl.ANY),
            pl.BlockSpec((COMB_TOK, d), lambda i: (i, 0)),
            pl.BlockSpec((COMB_TOK, TOP_K_IN_GROUP), lambda i: (i, 0)),
            pl.BlockSpec((1, d), lambda i: (0, 0)),
        ],
        out_specs=pl.BlockSpec((COMB_TOK, d), lambda i: (i, 0)),
        out_shape=jax.ShapeDtypeStruct((t, d), F32),
        scratch_shapes=[
            pltpu.VMEM((2 * TOP_K_IN_GROUP * COMB_TOK * ROW_CHUNKS, LANES), F32),
            pltpu.SemaphoreType.DMA((2,)),
        ],
        compiler_params=pltpu.CompilerParams(dimension_semantics=("arbitrary",)),
        name="combine",
    )(dest, ys, x1, gw_t, gf)


def kernel(x, norm1_g, w_in, conv_w, conv_b, lru_wa, lru_ba, lru_wx, lru_bx, lru_lambda, w_proj_attn,
           w_proj_rnn, w_out, rel_bias, norm2_g, w_group, b_group, w_expert_router, b_expert_router,
           w_gate, w_up, w_down, norm_f_g):
    batch, seq, d = x.shape
    tokens = batch * seq
    nblk = seq // QB
    rep = ATTN_HEADS // KV_HEADS
    x2 = x.reshape(tokens, d)

    w = w_in[0]
    w_a = w[:, :PA_COLS].astype(BF16)
    w_c = jnp.concatenate(
        [w[:, OFF_XR:], w[:, OFF_KI:OFF_XR], jnp.zeros((d, PC_COLS - PC_MAIN - (OFF_XR - OFF_KI)), w.dtype)],
        axis=1).astype(BF16)
    g1 = norm1_g[0].reshape(1, d)
    pa = _norm_proj(x2, g1, w_a, BF16, 1024, 512)
    pc = _norm_proj(x2, g1, w_c, F32, 1024, 1280)

    q = pa[:, OFF_Q:OFF_K].reshape(batch, nblk, QB, KV_HEADS, rep, HEAD_DIM)
    qt = q.transpose(0, 1, 3, 5, 4, 2).reshape(batch, nblk, KV_HEADS, HEAD_DIM, rep * QB)
    k = pa[:, OFF_K:OFF_V].reshape(batch, seq, KV_WIDTH)
    v = pa[:, OFF_V:OFF_QI].reshape(batch, seq // KT, KT, KV_HEADS, HEAD_DIM)
    vt = v.transpose(0, 3, 1, 4, 2)
    qi = pa[:, OFF_QI:OFF_KI].reshape(batch, nblk, QB, IDX_HEADS, IDX_DIM)
    qit = qi.transpose(0, 1, 4, 3, 2).reshape(batch, nblk, IDX_DIM, IDX_HEADS * QB)
    ki = pc[:, PC_MAIN:PC_MAIN + IDX_DIM].astype(BF16).reshape(batch, seq, IDX_DIM)
    wi = pc[:, PC_MAIN + IDX_DIM:PC_MAIN + IDX_DIM + IDX_HEADS].reshape(batch, nblk, QB, IDX_HEADS)
    wit = wi.transpose(0, 1, 3, 2).reshape(batch, nblk, 1, IDX_HEADS * QB)
    ks = np.arange(QB)[:, None]
    qs = np.arange(QB)[None, :]
    bkt = jnp.asarray(np.stack([_rel_bucket_np(qs - ks + QB * dd) for dd in range(2)]))
    attn = _dsa_attention(rel_bias, qit, wit, ki, k, qt, vt, bkt, batch, seq).reshape(tokens, ATTN_WIDTH)

    rnn = _rglru(pc, conv_w[0], conv_b[0].reshape(1, RNN_WIDTH), lru_wa[0].astype(BF16), lru_ba[0],
                 lru_wx[0].astype(BF16), lru_bx[0], lru_lambda[0].reshape(1, RNN_WIDTH), batch, seq)

    w_r = jnp.concatenate([w_group[0], w_expert_router[0],
                           jnp.zeros((d, LANES - N_GROUPS - N_EXPERTS), F32)], axis=1)
    w_rh = w_r.astype(BF16)
    w_rl = (w_r - w_rh.astype(F32)).astype(BF16)
    b_r = jnp.concatenate([b_group[0], b_expert_router[0],
                           jnp.zeros((LANES - N_GROUPS - N_EXPERTS,), F32)]).reshape(1, LANES)
    x1, u2, lgt = _merge(attn, rnn, pc, x2, w_proj_attn[0].astype(BF16), w_proj_rnn[0].astype(BF16),
                         w_out[0].astype(BF16), norm2_g[0].reshape(1, d), w_rh, w_rl, b_r)

    n_slots = tokens * TOP_K_IN_GROUP
    cap = -(-(n_slots + N_EXPERTS * (MOE_BLOCK - 1)) // MOE_BLOCK) * MOE_BLOCK
    dest, gw, meta = _route(lgt)

    dest_flat = dest.reshape(-1)
    ys = _experts(meta[0, :N_EXPERTS], meta[1, :N_EXPERTS], meta[2, :1], dest_flat, u2,
                  w_gate[0], w_up[0], w_down[0], cap)
    out = _combine(dest_flat, ys, x1, gw.T, norm_f_g.reshape(1, d))
    return out.reshape(batch, seq, d)
```

```python
import functools

import numpy as np
import jax
import jax.numpy as jnp
from jax import lax
from jax.experimental import pallas as pl
from jax.experimental.pallas import tpu as pltpu

D_MODEL = 2048
ATTN_HEADS = 8
KV_HEADS = 2
HEAD_DIM = 128
ATTN_WIDTH = ATTN_HEADS * HEAD_DIM
KV_WIDTH = KV_HEADS * HEAD_DIM
IDX_HEADS = 16
IDX_DIM = 64
TOPK_MAX = 256
RNN_WIDTH = 1024
RNN_BLOCKS = 8
RNN_BLOCK_W = RNN_WIDTH // RNN_BLOCKS
CONV_WIDTH = 4
LRU_C = 8.0
REL_BUCKETS = 32
REL_MAX_DIST = 128
N_GROUPS = 8
EXPERTS_PER_GROUP = 8
N_EXPERTS = N_GROUPS * EXPERTS_PER_GROUP
TOP_K_IN_GROUP = 2
EXPERT_FF = 512
MOE_BLOCK = 128
MOE_SHIFT = 7
EPS = 1e-6

LANES = 128
QB = 128
KT = 256
NEG_BIG = -1e30
INT_MIN = -(2 ** 31)
LOG2E = np.float32(np.log2(np.e))
F32 = jnp.float32
BF16 = jnp.bfloat16
I32 = jnp.int32

OFF_Q = 0
OFF_K = OFF_Q + ATTN_WIDTH
OFF_V = OFF_K + KV_WIDTH
OFF_QI = OFF_V + KV_WIDTH
OFF_KI = OFF_QI + IDX_HEADS * IDX_DIM
OFF_WI = OFF_KI + IDX_DIM
OFF_XR = OFF_WI + IDX_HEADS
OFF_RG = OFF_XR + RNN_WIDTH
OFF_GA = OFF_RG + RNN_WIDTH
OFF_GR = OFF_GA + D_MODEL
IN_COLS = OFF_GR + D_MODEL
PA_COLS = OFF_KI
PC_MAIN = IN_COLS - OFF_XR
PC_COLS = 6400


def _rel_bucket_np(n):
    n = np.maximum(n, 0)
    max_exact = REL_BUCKETS // 2
    nf = np.maximum(n, 1).astype(np.float32)
    large = max_exact + (np.log(nf / np.float32(max_exact)) / np.float32(np.log(REL_MAX_DIST / max_exact))
                         * np.float32(REL_BUCKETS - max_exact)).astype(np.int32)
    large = np.minimum(large, REL_BUCKETS - 1)
    return np.where(n < max_exact, n, large).astype(np.int32)


def _norm_proj_kernel(x_ref, g_ref, w_ref, o_ref, u_ref):
    @pl.when(pl.program_id(1) == 0)
    def _():
        x = x_ref[...]
        ms = jnp.mean(x * x, axis=-1, keepdims=True)
        u_ref[...] = (x * lax.rsqrt(ms + EPS) * g_ref[...]).astype(BF16)

    o_ref[...] = jnp.dot(u_ref[...], w_ref[...], preferred_element_type=F32).astype(o_ref.dtype)


def _norm_proj(x2, g, w, out_dtype, tm, tn):
    m, d = x2.shape
    n = w.shape[1]
    return pl.pallas_call(
        _norm_proj_kernel,
        grid=(m // tm, n // tn),
        in_specs=[
            pl.BlockSpec((tm, d), lambda i, j: (i, 0)),
            pl.BlockSpec((1, d), lambda i, j: (0, 0)),
            pl.BlockSpec((d, tn), lambda i, j: (0, j)),
        ],
        out_specs=pl.BlockSpec((tm, tn), lambda i, j: (i, j)),
        out_shape=jax.ShapeDtypeStruct((m, n), out_dtype),
        scratch_shapes=[pltpu.VMEM((tm, d), BF16)],
        compiler_params=pltpu.CompilerParams(dimension_semantics=("arbitrary", "arbitrary")),
        name="norm_proj",
    )(x2, g, w)


def _dsa_kernel(relb_ref, qit_ref, wit_ref, ki_ref, k_ref, qt_ref, vt_ref, bkt_ref, o_ref,
                keys_ref, bias_ref, m_ref, l_ref, acc_ref):
    b = pl.program_id(0)
    j = pl.program_id(1)

    @pl.when((b == 0) & (j == 0))
    def _():
        for d in range(2):
            bk = bkt_ref[d]
            for h in range(ATTN_HEADS):
                tile = jnp.zeros((QB, QB), F32)
                for bb in range(REL_BUCKETS):
                    tile = jnp.where(bk == bb, relb_ref[bb, h] * LOG2E, tile)
                bias_ref[d, h] = tile

    t0 = j * QB
    ntile = (j + 2) // 2
    q_pos = t0 + lax.broadcasted_iota(I32, (KT, QB), 1)
    row_iota = lax.broadcasted_iota(I32, (KT, QB), 0)

    w_scale = np.float32(IDX_HEADS ** -0.5) * np.float32(IDX_DIM ** -0.5)
    wrow = wit_ref[...] * w_scale

    def score_tile(kt, carry):
        r0 = pl.multiple_of(kt * KT, KT)
        ki_t = ki_ref[pl.ds(r0, KT), :]
        acc = jnp.zeros((KT, QB), F32)
        for c in range(4):
            dots = jnp.dot(ki_t, qit_ref[:, c * 512:(c + 1) * 512], preferred_element_type=F32)
            for hh in range(4):
                h = c * 4 + hh
                acc = acc + jnp.maximum(dots[:, hh * QB:(hh + 1) * QB], 0.0) * wrow[:, h * QB:(h + 1) * QB]
        bits = pltpu.bitcast(acc, I32)
        key = bits ^ ((bits >> 31) & jnp.int32(0x7FFFFFFF))
        key = jnp.where(r0 + row_iota <= q_pos, key, jnp.int32(INT_MIN))
        keys_ref[pl.ds(r0, KT), :] = key
        return carry

    lax.fori_loop(0, ntile, score_tile, 0)

    n_acc = 4

    def count_ge(cand):
        cand8 = jnp.broadcast_to(cand, (8, QB))

        def body(kt, accs):
            r0 = pl.multiple_of(kt * KT, KT)
            accs = list(accs)
            for v in range(KT // 8):
                blk = keys_ref[pl.ds(r0 + v * 8, 8), :]
                accs[v % n_acc] = accs[v % n_acc] + jnp.where(blk >= cand8, jnp.int32(1), jnp.int32(0))
            return tuple(accs)

        accs = lax.fori_loop(0, ntile, body, (jnp.zeros((8, QB), I32),) * n_acc)
        acc8 = (accs[0] + accs[1]) + (accs[2] + accs[3])
        return jnp.sum(acc8.astype(F32), axis=0, keepdims=True).astype(I32)

    c0 = count_ge(jnp.zeros((1, QB), I32))
    thr0 = jnp.where(c0 >= TOPK_MAX, jnp.int32(0), jnp.int32(INT_MIN))

    def bit_step(i, thr):
        cand = thr | (jnp.int32(1) << (30 - i))
        c = count_ge(cand)
        return jnp.where(c >= TOPK_MAX, cand, thr)

    thr = lax.fori_loop(0, 31, bit_step, thr0)
    thr = jnp.maximum(thr, jnp.int32(INT_MIN + 1))

    qk_scale = np.float32(HEAD_DIM ** -0.5) * LOG2E
    rep = ATTN_HEADS // KV_HEADS
    nfar = jnp.maximum((j - 1) // 2, 0)

    m_ref[...] = jnp.full(m_ref.shape, NEG_BIG, F32)
    l_ref[...] = jnp.zeros(l_ref.shape, F32)
    acc_ref[...] = jnp.zeros(acc_ref.shape, F32)

    def attend(kt, g, near):
        r0 = pl.multiple_of(kt * KT, KT)
        k_t = k_ref[pl.ds(r0, KT), g * HEAD_DIM:(g + 1) * HEAD_DIM]
        lg = jnp.dot(k_t, qt_ref[g], preferred_element_type=F32) * qk_scale
        sel = keys_ref[pl.ds(r0, KT), :] >= thr
        cols = []
        for r in range(rep):
            h = g * rep + r
            far_bias = relb_ref[REL_BUCKETS - 1, h] * LOG2E
            blk = lg[:, r * QB:(r + 1) * QB]
            if near:
                subs = []
                for sub in range(KT // QB):
                    dblk = j - (kt * (KT // QB) + sub)
                    bias = jnp.where(dblk == 0, bias_ref[0, h],
                                     jnp.where(dblk == 1, bias_ref[1, h], far_bias))
                    subs.append(blk[sub * QB:(sub + 1) * QB, :] + bias)
                blk = jnp.concatenate(subs, axis=0)
            else:
                blk = blk + far_bias
            cols.append(jnp.where(sel, blk, NEG_BIG))
        s = jnp.concatenate(cols, axis=1)
        m_old = m_ref[g]
        m_new = jnp.maximum(m_old, jnp.max(s, axis=0, keepdims=True))
        alpha = jnp.exp2(m_old - m_new)
        p = jnp.exp2(s - m_new)
        l_ref[g] = alpha * l_ref[g] + jnp.sum(p, axis=0, keepdims=True)
        pv = jnp.dot(vt_ref[g, kt], p.astype(BF16), preferred_element_type=F32)
        acc_ref[g] = acc_ref[g] * alpha + pv
        m_ref[g] = m_new

    def far_body(kt, carry):
        for g in range(KV_HEADS):
            attend(kt, g, False)
        return carry

    def near_body(kt, carry):
        for g in range(KV_HEADS):
            attend(kt, g, True)
        return carry

    lax.fori_loop(0, nfar, far_body, 0)
    lax.fori_loop(nfar, ntile, near_body, 0)

    for g in range(KV_HEADS):
        out_t = acc_ref[g] / l_ref[g]
        for r in range(rep):
            h = g * rep + r
            o_ref[:, h * HEAD_DIM:(h + 1) * HEAD_DIM] = out_t[:, r * QB:(r + 1) * QB].T.astype(o_ref.dtype)


def _dsa_attention(rel_bias, qit, wit, ki, k, qt, vt, bkt, batch, seq):
    nblk = seq // QB
    rep = ATTN_HEADS // KV_HEADS
    return pl.pallas_call(
        _dsa_kernel,
        grid=(batch, nblk),
        in_specs=[
            pl.BlockSpec(memory_space=pltpu.SMEM),
            pl.BlockSpec((None, None, IDX_DIM, IDX_HEADS * QB), lambda b, j: (b, j, 0, 0)),
            pl.BlockSpec((None, None, 1, IDX_HEADS * QB), lambda b, j: (b, j, 0, 0)),
            pl.BlockSpec((None, seq, IDX_DIM), lambda b, j: (b, 0, 0)),
            pl.BlockSpec((None, seq, KV_WIDTH), lambda b, j: (b, 0, 0)),
            pl.BlockSpec((None, None, KV_HEADS, HEAD_DIM, rep * QB), lambda b, j: (b, j, 0, 0, 0)),
            pl.BlockSpec((None, KV_HEADS, seq // KT, HEAD_DIM, KT), lambda b, j: (b, 0, 0, 0, 0)),
            pl.BlockSpec((2, QB, QB), lambda b, j: (0, 0, 0)),
        ],
        out_specs=pl.BlockSpec((None, QB, ATTN_WIDTH), lambda b, j: (b, j, 0)),
        out_shape=jax.ShapeDtypeStruct((batch, seq, ATTN_WIDTH), BF16),
        scratch_shapes=[
            pltpu.VMEM((seq, QB), I32),
            pltpu.VMEM((2, ATTN_HEADS, QB, QB), F32),
            pltpu.VMEM((KV_HEADS, 1, rep * QB), F32),
            pltpu.VMEM((KV_HEADS, 1, rep * QB), F32),
            pltpu.VMEM((KV_HEADS, HEAD_DIM, rep * QB), F32),
        ],
        compiler_params=pltpu.CompilerParams(dimension_semantics=("arbitrary", "arbitrary")),
        name="dsa_attention",
    )(rel_bias, qit, wit, ki, k, qt, vt, bkt)


RNN_TS = 256


def _gelu_tanh(x):
    c = np.float32(np.sqrt(2.0 / np.pi))
    return x * (0.5 * (1.0 + jnp.tanh(c * (x + np.float32(0.044715) * (x * x * x)))))


def _softplus(z):
    return jnp.maximum(z, 0.0) + jnp.log1p(jnp.exp(-jnp.abs(z)))


def _rglru_kernel(xr_ref, gate_ref, cw_ref, cb_ref, wa_ref, ba_ref, wx_ref, bx_ref, lam_ref, o_ref,
                  xext_ref, h_ref):
    i = pl.program_id(1)
    ts = RNN_TS

    @pl.when(i == 0)
    def _():
        xext_ref[0:8, :] = jnp.zeros((8, RNN_WIDTH), F32)
        h_ref[...] = jnp.zeros((1, RNN_WIDTH), F32)

    @pl.when(i > 0)
    def _():
        xext_ref[0:8, :] = xext_ref[ts:ts + 8, :]

    xext_ref[8:8 + ts, :] = xr_ref[...]

    row = lax.broadcasted_iota(I32, (ts, RNN_BLOCK_W), 0)
    for n in range(RNN_BLOCKS):
        cs = slice(n * RNN_BLOCK_W, (n + 1) * RNN_BLOCK_W)
        xc = cb_ref[:, cs]
        for jj in range(CONV_WIDTH):
            off = 8 - (CONV_WIDTH - 1) + jj
            xc = xc + xext_ref[off:off + ts, cs] * cw_ref[jj:jj + 1, cs]
        xcb = xc.astype(BF16)
        r = jax.nn.sigmoid(jnp.dot(xcb, wa_ref[n], preferred_element_type=F32) + ba_ref[n:n + 1, :])
        gi = jax.nn.sigmoid(jnp.dot(xcb, wx_ref[n], preferred_element_type=F32) + bx_ref[n:n + 1, :])
        log_a = (-LRU_C * r) * _softplus(-lam_ref[:, cs])
        a = jnp.exp(log_a)
        mult = jnp.sqrt(1.0 - jnp.exp(2.0 * log_a))
        bt = mult * (gi * xc)
        k = 1
        while k < ts:
            a_s = pltpu.roll(a, k, 0)
            b_s = pltpu.roll(bt, k, 0)
            keep = row >= k
            bt = jnp.where(keep, a * b_s + bt, bt)
            a = jnp.where(keep, a * a_s, a)
            k *= 2
        h = bt + a * h_ref[:, cs]
        h_ref[:, cs] = h[ts - 1:ts, :]
        o_ref[:, cs] = (h * _gelu_tanh(gate_ref[:, cs])).astype(o_ref.dtype)


def _rglru(pc, conv_w, conv_b, wa, ba, wx, bx, lam, batch, seq):
    nts = seq // RNN_TS
    full = lambda shape: pl.BlockSpec(shape, lambda b, i: (0,) * len(shape))
    return pl.pallas_call(
        _rglru_kernel,
        grid=(batch, nts),
        in_specs=[
            pl.BlockSpec((RNN_TS, RNN_WIDTH), lambda b, i: (b * nts + i, 0)),
            pl.BlockSpec((RNN_TS, RNN_WIDTH), lambda b, i: (b * nts + i, 1)),
            full((CONV_WIDTH, RNN_WIDTH)),
            full((1, RNN_WIDTH)),
            full((RNN_BLOCKS, RNN_BLOCK_W, RNN_BLOCK_W)),
            full((RNN_BLOCKS, RNN_BLOCK_W)),
            full((RNN_BLOCKS, RNN_BLOCK_W, RNN_BLOCK_W)),
            full((RNN_BLOCKS, RNN_BLOCK_W)),
            full((1, RNN_WIDTH)),
        ],
        out_specs=pl.BlockSpec((RNN_TS, RNN_WIDTH), lambda b, i: (b * nts + i, 0)),
        out_shape=jax.ShapeDtypeStruct((batch * seq, RNN_WIDTH), BF16),
        scratch_shapes=[pltpu.VMEM((RNN_TS + 8, RNN_WIDTH), F32), pltpu.VMEM((1, RNN_WIDTH), F32)],
        compiler_params=pltpu.CompilerParams(dimension_semantics=("arbitrary", "arbitrary")),
        name="rglru",
    )(pc, pc, conv_w, conv_b, wa, ba, wx, bx, lam)


MERGE_TM = 256
ROW_CHUNKS = D_MODEL // LANES


def _store_token_major(ref, base, val):
    n = val.shape[0]
    for c in range(ROW_CHUNKS):
        ref[pl.ds(base + c, n, stride=ROW_CHUNKS), :] = val[:, c * LANES:(c + 1) * LANES]


def _load_token_major(ref, base, n):
    return jnp.concatenate([ref[pl.ds(base + c, n, stride=ROW_CHUNKS), :] for c in range(ROW_CHUNKS)], axis=1)


def _merge_kernel(attn_ref, rnn_ref, ga_ref, gr_ref, x_ref, wpa_ref, wpr_ref, wo_ref, g2_ref,
                  wrh_ref, wrl_ref, br_ref, x1_ref, u2_ref, lgt_ref):
    pa = jnp.dot(attn_ref[...], wpa_ref[...], preferred_element_type=F32)
    pr = jnp.dot(rnn_ref[...], wpr_ref[...], preferred_element_type=F32)
    merged = jax.nn.sigmoid(ga_ref[...]) * pa + jax.nn.sigmoid(gr_ref[...]) * pr
    x1 = x_ref[...] + jnp.dot(merged.astype(BF16), wo_ref[...], preferred_element_type=F32)
    x1_ref[...] = x1
    ms = jnp.mean(x1 * x1, axis=-1, keepdims=True)
    u2 = x1 * lax.rsqrt(ms + EPS) * g2_ref[...]
    _store_token_major(u2_ref, 0, u2)
    hi = u2.astype(BF16)
    lo = (u2 - hi.astype(F32)).astype(BF16)
    lg = (jnp.dot(hi, wrh_ref[...], preferred_element_type=F32)
          + jnp.dot(lo, wrh_ref[...], preferred_element_type=F32)
          + jnp.dot(hi, wrl_ref[...], preferred_element_type=F32)) + br_ref[...]
    lgt_ref[...] = lg.T


def _merge(attn, rnn, pc, x2, wpa, wpr, wo, g2, wrh, wrl, br):
    m = x2.shape[0]
    tm = MERGE_TM
    const = lambda shape: pl.BlockSpec(shape, lambda i: (0,) * len(shape), pipeline_mode=pl.Buffered(1))
    return pl.pallas_call(
        _merge_kernel,
        grid=(m // tm,),
        in_specs=[
            pl.BlockSpec((tm, ATTN_WIDTH), lambda i: (i, 0)),
            pl.BlockSpec((tm, RNN_WIDTH), lambda i: (i, 0)),
            pl.BlockSpec((tm, D_MODEL), lambda i: (i, 1)),
            pl.BlockSpec((tm, D_MODEL), lambda i: (i, 2)),
            pl.BlockSpec((tm, D_MODEL), lambda i: (i, 0)),
            const((ATTN_WIDTH, D_MODEL)),
            const((RNN_WIDTH, D_MODEL)),
            const((D_MODEL, D_MODEL)),
            const((1, D_MODEL)),
            const((D_MODEL, LANES)),
            const((D_MODEL, LANES)),
            const((1, LANES)),
        ],
        out_specs=[
            pl.BlockSpec((tm, D_MODEL), lambda i: (i, 0)),
            pl.BlockSpec((tm * ROW_CHUNKS, LANES), lambda i: (i, 0)),
            pl.BlockSpec((LANES, tm), lambda i: (0, i)),
        ],
        out_shape=[
            jax.ShapeDtypeStruct((m, D_MODEL), F32),
            jax.ShapeDtypeStruct((m * ROW_CHUNKS, LANES), F32),
            jax.ShapeDtypeStruct((LANES, m), F32),
        ],
        compiler_params=pltpu.CompilerParams(dimension_semantics=("arbitrary",)),
        name="merge_outproj",
    )(attn, rnn, pc, pc, x2, wpa, wpr, wo, g2, wrh, wrl, br)


ROUTE_CHUNK = 256


def _first_index_of_max(v, ridx, n):
    vmax = jnp.max(v, axis=0, keepdims=True)
    idx = jnp.min(jnp.where(v == vmax, ridx, jnp.int32(n)).astype(F32), axis=0, keepdims=True)
    return vmax, idx.astype(I32)


def _route_kernel(lgt_ref, dest_ref, gw_ref, meta_ref, cum_ref):
    t = lgt_ref.shape[1]
    eg = EXPERTS_PER_GROUP
    ridx8 = lax.broadcasted_iota(I32, (eg, t), 0)
    gl = lgt_ref[0:N_GROUPS, :]
    gmax, g_sel = _first_index_of_max(gl, ridx8, N_GROUPS)
    p_sel = 1.0 / jnp.sum(jnp.exp(gl - gmax), axis=0, keepdims=True)
    el = lgt_ref[N_GROUPS:N_GROUPS + eg, :]
    for g in range(1, N_GROUPS):
        el = jnp.where(g_sel == g, lgt_ref[N_GROUPS + g * eg:N_GROUPS + (g + 1) * eg, :], el)
    v0, i0 = _first_index_of_max(el, ridx8, eg)
    el1 = jnp.where(ridx8 == i0, -jnp.inf, el)
    v1, i1 = _first_index_of_max(el1, ridx8, eg)
    e1 = jnp.exp(v1 - v0)
    den = 1.0 + e1
    gw_ref[0:1, :] = p_sel * (1.0 / den)
    gw_ref[1:2, :] = p_sel * (e1 / den)
    ex0 = g_sel * eg + i0
    ex1 = g_sel * eg + i1

    eidx = lax.broadcasted_iota(I32, (N_EXPERTS, ROUTE_CHUNK), 0)
    ui = lax.broadcasted_iota(I32, (ROUTE_CHUNK, ROUTE_CHUNK), 0)
    uj = lax.broadcasted_iota(I32, (ROUTE_CHUNK, ROUTE_CHUNK), 1)
    upper = jnp.where(ui < uj, 1.0, 0.0).astype(BF16)
    run = jnp.zeros((N_EXPERTS, 1), F32)
    for c in range(t // ROUTE_CHUNK):
        cs = slice(c * ROUTE_CHUNK, (c + 1) * ROUTE_CHUNK)
        hit = jnp.where(eidx == ex0[:, cs], 1.0, jnp.where(eidx == ex1[:, cs], 1.0, 0.0))
        cum_ref[:, cs] = jnp.dot(hit.astype(BF16), upper, preferred_element_type=F32) + run
        run = run + jnp.sum(hit, axis=1, keepdims=True)

    counts = run.astype(I32)
    padded = ((counts + (MOE_BLOCK - 1)) >> MOE_SHIFT) << MOE_SHIFT
    pe = jnp.broadcast_to(padded, (N_EXPERTS, LANES))
    erow = lax.broadcasted_iota(I32, (N_EXPERTS, LANES), 0)
    k = 1
    while k < N_EXPERTS:
        pe = pe + jnp.where(erow >= k, pltpu.roll(pe, k, 0), 0)
        k *= 2
    pends = pe[:, 0:1]
    pstarts = pends - padded

    eidx_t = lax.broadcasted_iota(I32, (N_EXPERTS, t), 0)
    slot = cum_ref[...] + pstarts.astype(F32)
    dest_ref[0:1, :] = jnp.sum(jnp.where(eidx_t == ex0, slot, 0.0), axis=0, keepdims=True).astype(I32)
    dest_ref[1:2, :] = jnp.sum(jnp.where(eidx_t == ex1, slot, 0.0), axis=0, keepdims=True).astype(I32)

    nb = meta_ref.shape[1]
    own = lax.broadcasted_iota(I32, (N_EXPERTS, nb), 0) == lax.broadcasted_iota(I32, (N_EXPERTS, nb), 1)
    first_blk = (pstarts >> MOE_SHIFT).astype(F32)
    n_blk = (padded >> MOE_SHIFT).astype(F32)
    meta_ref[0:1, :] = jnp.sum(jnp.where(own, first_blk, 0.0), axis=0, keepdims=True).astype(I32)
    meta_ref[1:2, :] = jnp.sum(jnp.where(own, n_blk, 0.0), axis=0, keepdims=True).astype(I32)
    meta_ref[2:3, :] = jnp.broadcast_to(pends[N_EXPERTS - 1:N_EXPERTS, :] >> MOE_SHIFT, (1, nb))


def _route(lgt):
    t = lgt.shape[1]
    return pl.pallas_call(
        _route_kernel,
        out_shape=[
            jax.ShapeDtypeStruct((2, t), I32),
            jax.ShapeDtypeStruct((2, t), F32),
            jax.ShapeDtypeStruct((3, LANES), I32),
        ],
        scratch_shapes=[pltpu.VMEM((N_EXPERTS, t), F32)],
        name="route",
    )(lgt)


W_CHUNKS = 8
GATHER_AHEAD = 3
X_SLOTS = GATHER_AHEAD + 1
W_AHEAD = 2
W_SLOTS = W_AHEAD + 1


def _expert_kernel(first_ref, nblk_ref, nact_ref, dest_ref, u_ref, wg_ref, wu_ref, wd_ref, ys_ref,
                   tok_ref, xbuf_ref, ybuf_ref, wgf_ref, wuf_ref, wdf_ref, wgb_ref, wub_ref, wdb_ref,
                   gsem, osem, wsem):
    e = pl.program_id(0)
    first = first_ref[e]
    nb = nblk_ref[e]
    nact = nact_ref[0]
    ntok = u_ref.shape[0] // ROW_CHUNKS
    nslot = tok_ref.shape[0]
    blk_rows = MOE_BLOCK * ROW_CHUNKS

    def row_copy(blk, slot, r):
        tok = tok_ref[blk * MOE_BLOCK + r]
        src = u_ref.at[pl.ds(pl.multiple_of(tok * ROW_CHUNKS, ROW_CHUNKS), ROW_CHUNKS)]
        dst = xbuf_ref.at[pl.ds(pl.multiple_of(slot * blk_rows + r * ROW_CHUNKS, ROW_CHUNKS), ROW_CHUNKS)]
        return pltpu.make_async_copy(src, dst, gsem.at[slot])

    def start_rows(blk, slot):
        def body(r, carry):
            row_copy(blk, slot, r).start()
            return carry
        lax.fori_loop(0, MOE_BLOCK, body, 0, unroll=8)

    def wait_rows(blk, slot):
        def body(r, carry):
            row_copy(blk, slot, r).wait()
            return carry
        lax.fori_loop(0, MOE_BLOCK, body, 0, unroll=8)

    def out_copy(blk, slot):
        src = ybuf_ref.at[pl.ds(pl.multiple_of(slot * blk_rows, blk_rows), blk_rows)]
        dst = ys_ref.at[pl.ds(pl.multiple_of(blk * blk_rows, blk_rows), blk_rows)]
        return pltpu.make_async_copy(src, dst, osem.at[slot])

    def build_slot_table():
        def clear(p, carry):
            tok_ref[p] = 0
            return carry
        lax.fori_loop(0, nslot, clear, 0, unroll=8)

        def put(t, carry):
            for kk in range(TOP_K_IN_GROUP):
                tok_ref[dest_ref[kk * ntok + t]] = t
            return carry
        lax.fori_loop(0, ntok, put, 0, unroll=8)

    def weight_copies(ex, wslot):
        copies = []
        for src, dst in ((wg_ref, wgf_ref), (wu_ref, wuf_ref), (wd_ref, wdf_ref)):
            rows = src.shape[1] // W_CHUNKS
            for c in range(W_CHUNKS):
                rs = pl.ds(c * rows, rows)
                copies.append(pltpu.make_async_copy(src.at[ex, rs], dst.at[wslot, rs], wsem.at[wslot]))
        return copies

    @pl.when(e == 0)
    def _():
        for a in range(W_AHEAD):
            for cp in weight_copies(a, a):
                cp.start(priority=1)

    @pl.when(e + W_AHEAD < pl.num_programs(0))
    def _():
        for cp in weight_copies(e + W_AHEAD, (e + W_AHEAD) % W_SLOTS):
            cp.start(priority=1)

    @pl.when(e == 0)
    def _():
        build_slot_table()
        for a in range(GATHER_AHEAD):
            @pl.when(a < nact)
            def _():
                start_rows(a, a)

    wslot = e % W_SLOTS
    for cp in weight_copies(e, wslot):
        cp.wait()
    wgb_ref[...] = wgf_ref[wslot].astype(BF16)
    wub_ref[...] = wuf_ref[wslot].astype(BF16)
    wdb_ref[...] = wdf_ref[wslot].astype(BF16)

    def block(blk, carry):
        xslot = blk % X_SLOTS
        yslot = blk % 2

        @pl.when(blk + GATHER_AHEAD < nact)
        def _():
            start_rows(blk + GATHER_AHEAD, (blk + GATHER_AHEAD) % X_SLOTS)

        wait_rows(blk, xslot)
        xb = _load_token_major(xbuf_ref, xslot * blk_rows, MOE_BLOCK).astype(BF16)
        hg = jnp.dot(xb, wgb_ref[...], preferred_element_type=F32)
        hu = jnp.dot(xb, wub_ref[...], preferred_element_type=F32)
        h = (hg * jax.nn.sigmoid(hg)) * hu
        y = jnp.dot(h.astype(BF16), wdb_ref[...], preferred_element_type=F32)

        @pl.when(blk >= 2)
        def _():
            out_copy(blk - 2, yslot).wait()

        _store_token_major(ybuf_ref, yslot * blk_rows, y)
        out_copy(blk, yslot).start()
        return carry

    lax.fori_loop(first, first + nb, block, 0)

    @pl.when(e == pl.num_programs(0) - 1)
    def _():
        @pl.when(nact >= 2)
        def _():
            out_copy(nact - 2, nact % 2).wait()

        @pl.when(nact >= 1)
        def _():
            out_copy(nact - 1, (nact - 1) % 2).wait()

        ntotal = ys_ref.shape[0] // blk_rows
        ybuf_ref[0:blk_rows, :] = jnp.zeros((blk_rows, LANES), ybuf_ref.dtype)

        def fill_start(blk, carry):
            out_copy(blk, 0).start()
            return carry

        def fill_wait(blk, carry):
            out_copy(blk, 0).wait()
            return carry

        lax.fori_loop(nact_ref[0], ntotal, fill_start, 0)
        lax.fori_loop(nact_ref[0], ntotal, fill_wait, 0)


def _experts(first_blk, n_blk, nact, dest_flat, u2, w_gate, w_up, w_down, cap):
    d = D_MODEL
    grid_spec = pltpu.PrefetchScalarGridSpec(
        num_scalar_prefetch=4,
        grid=(N_EXPERTS,),
        in_specs=[pl.BlockSpec(memory_space=pl.ANY)] * 4,
        out_specs=pl.BlockSpec(memory_space=pl.ANY),
        scratch_shapes=[
            pltpu.SMEM((cap,), I32),
            pltpu.VMEM((X_SLOTS * MOE_BLOCK * ROW_CHUNKS, LANES), F32),
            pltpu.VMEM((2 * MOE_BLOCK * ROW_CHUNKS, LANES), F32),
            pltpu.VMEM((W_SLOTS, d, EXPERT_FF), F32),
            pltpu.VMEM((W_SLOTS, d, EXPERT_FF), F32),
            pltpu.VMEM((W_SLOTS, EXPERT_FF, d), F32),
            pltpu.VMEM((d, EXPERT_FF), BF16),
            pltpu.VMEM((d, EXPERT_FF), BF16),
            pltpu.VMEM((EXPERT_FF, d), BF16),
            pltpu.SemaphoreType.DMA((X_SLOTS,)),
            pltpu.SemaphoreType.DMA((2,)),
            pltpu.SemaphoreType.DMA((W_SLOTS,)),
        ],
    )
    return pl.pallas_call(
        _expert_kernel,
        grid_spec=grid_spec,
        out_shape=jax.ShapeDtypeStruct((cap * ROW_CHUNKS, LANES), F32),
        compiler_params=pltpu.CompilerParams(dimension_semantics=("arbitrary",), has_side_effects=True),
        name="experts",
    )(first_blk, n_blk, nact, dest_flat, u2, w_gate, w_up, w_down)


COMB_TOK = 64
COMB_AHEAD = 3
COMB_SLOTS = COMB_AHEAD + 1


def _combine_kernel(dest_ref, ys_ref, x1_ref, gw_ref, gf_ref, o_ref, buf_ref, sem):
    i = pl.program_id(0)
    n = pl.num_programs(0)
    ntok = n * COMB_TOK

    def buf_base(slot, kk):
        return (slot * TOP_K_IN_GROUP + kk) * (COMB_TOK * ROW_CHUNKS)

    def copy(step, slot, tl, kk):
        d = dest_ref[kk * ntok + step * COMB_TOK + tl]
        src = ys_ref.at[pl.ds(pl.multiple_of(d * ROW_CHUNKS, ROW_CHUNKS), ROW_CHUNKS)]
        dst = buf_ref.at[pl.ds(pl.multiple_of(buf_base(slot, kk) + tl * ROW_CHUNKS, ROW_CHUNKS), ROW_CHUNKS)]
        return pltpu.make_async_copy(src, dst, sem.at[slot])

    def start_all(step, slot):
        def body(tl, carry):
            for kk in range(TOP_K_IN_GROUP):
                copy(step, slot, tl, kk).start(priority=kk)
            return carry
        lax.fori_loop(0, COMB_TOK, body, 0, unroll=8)

    def wait_all(step, slot):
        def body(tl, carry):
            for kk in range(TOP_K_IN_GROUP):
                copy(step, slot, tl, kk).wait()
            return carry
        lax.fori_loop(0, COMB_TOK, body, 0, unroll=8)

    @pl.when(i == 0)
    def _():
        for a in range(COMB_AHEAD):
            start_all(a, a)

    @pl.when(i + COMB_AHEAD < n)
    def _():
        start_all(i + COMB_AHEAD, (i + COMB_AHEAD) % COMB_SLOTS)

    slot = i % COMB_SLOTS
    wait_all(i, slot)
    y0 = _load_token_major(buf_ref, buf_base(slot, 0), COMB_TOK)
    y1 = _load_token_major(buf_ref, buf_base(slot, 1), COMB_TOK)
    y = gw_ref[:, 0:1] * y0 + gw_ref[:, 1:2] * y1
    x = x1_ref[...] + y
    ms = jnp.mean(x * x, axis=-1, keepdims=True)
    o_ref[...] = x * lax.rsqrt(ms + EPS) * gf_ref[...]


def _combine(dest, ys, x1, gw_t, gf):
    t, d = x1.shape
    return pl.pallas_call(
        _combine_kernel,
        grid=(t // COMB_TOK,),
        in_specs=[
            pl.BlockSpec(memory_space=pltpu.SMEM),
            pl.BlockSpec(memory_space=pl.ANY),
            pl.BlockSpec((COMB_TOK, d), lambda i: (i, 0)),
            pl.BlockSpec((COMB_TOK, TOP_K_IN_GROUP), lambda i: (i, 0)),
            pl.BlockSpec((1, d), lambda i: (0, 0)),
        ],
        out_specs=pl.BlockSpec((COMB_TOK, d), lambda i: (i, 0)),
        out_shape=jax.ShapeDtypeStruct((t, d), F32),
        scratch_shapes=[
            pltpu.VMEM((COMB_SLOTS * TOP_K_IN_GROUP * COMB_TOK * ROW_CHUNKS, LANES), F32),
            pltpu.SemaphoreType.DMA((COMB_SLOTS,)),
        ],
        compiler_params=pltpu.CompilerParams(dimension_semantics=("arbitrary",)),
        name="combine",
    )(dest, ys, x1, gw_t, gf)


def kernel(x, norm1_g, w_in, conv_w, conv_b, lru_wa, lru_ba, lru_wx, lru_bx, lru_lambda, w_proj_attn,
           w_proj_rnn, w_out, rel_bias, norm2_g, w_group, b_group, w_expert_router, b_expert_router,
           w_gate, w_up, w_down, norm_f_g):
    batch, seq, d = x.shape
    tokens = batch * seq
    nblk = seq // QB
    rep = ATTN_HEADS // KV_HEADS
    x2 = x.reshape(tokens, d)

    w = w_in[0]
    w_a = w[:, :PA_COLS].astype(BF16)
    w_c = jnp.concatenate(
        [w[:, OFF_XR:], w[:, OFF_KI:OFF_XR], jnp.zeros((d, PC_COLS - PC_MAIN - (OFF_XR - OFF_KI)), w.dtype)],
        axis=1).astype(BF16)
    g1 = norm1_g[0].reshape(1, d)
    pa = _norm_proj(x2, g1, w_a, BF16, 1024, 512)
    pc = _norm_proj(x2, g1, w_c, F32, 1024, 1280)

    q = pa[:, OFF_Q:OFF_K].reshape(batch, nblk, QB, KV_HEADS, rep, HEAD_DIM)
    qt = q.transpose(0, 1, 3, 5, 4, 2).reshape(batch, nblk, KV_HEADS, HEAD_DIM, rep * QB)
    k = pa[:, OFF_K:OFF_V].reshape(batch, seq, KV_WIDTH)
    v = pa[:, OFF_V:OFF_QI].reshape(batch, seq // KT, KT, KV_HEADS, HEAD_DIM)
    vt = v.transpose(0, 3, 1, 4, 2)
    qi = pa[:, OFF_QI:OFF_KI].reshape(batch, nblk, QB, IDX_HEADS, IDX_DIM)
    qit = qi.transpose(0, 1, 4, 3, 2).reshape(batch, nblk, IDX_DIM, IDX_HEADS * QB)
    ki = pc[:, PC_MAIN:PC_MAIN + IDX_DIM].astype(BF16).reshape(batch, seq, IDX_DIM)
    wi = pc[:, PC_MAIN + IDX_DIM:PC_MAIN + IDX_DIM + IDX_HEADS].reshape(batch, nblk, QB, IDX_HEADS)
    wit = wi.transpose(0, 1, 3, 2).reshape(batch, nblk, 1, IDX_HEADS * QB)
    ks = np.arange(QB)[:, None]
    qs = np.arange(QB)[None, :]
    bkt = jnp.asarray(np.stack([_rel_bucket_np(qs - ks + QB * dd) for dd in range(2)]))
    attn = _dsa_attention(rel_bias, qit, wit, ki, k, qt, vt, bkt, batch, seq).reshape(tokens, ATTN_WIDTH)

    rnn = _rglru(pc, conv_w[0], conv_b[0].reshape(1, RNN_WIDTH), lru_wa[0].astype(BF16), lru_ba[0],
                 lru_wx[0].astype(BF16), lru_bx[0], lru_lambda[0].reshape(1, RNN_WIDTH), batch, seq)

    w_r = jnp.concatenate([w_group[0], w_expert_router[0],
                           jnp.zeros((d, LANES - N_GROUPS - N_EXPERTS), F32)], axis=1)
    w_rh = w_r.astype(BF16)
    w_rl = (w_r - w_rh.astype(F32)).astype(BF16)
    b_r = jnp.concatenate([b_group[0], b_expert_router[0],
                           jnp.zeros((LANES - N_GROUPS - N_EXPERTS,), F32)]).reshape(1, LANES)
    x1, u2, lgt = _merge(attn, rnn, pc, x2, w_proj_attn[0].astype(BF16), w_proj_rnn[0].astype(BF16),
                         w_out[0].astype(BF16), norm2_g[0].reshape(1, d), w_rh, w_rl, b_r)

    n_slots = tokens * TOP_K_IN_GROUP
    cap = -(-(n_slots + N_EXPERTS * (MOE_BLOCK - 1)) // MOE_BLOCK) * MOE_BLOCK
    dest, gw, meta = _route(lgt)

    dest_flat = dest.reshape(-1)
    ys = _experts(meta[0, :N_EXPERTS], meta[1, :N_EXPERTS], meta[2, :1], dest_flat, u2,
                  w_gate[0], w_up[0], w_down[0], cap)
    out = _combine(dest_flat, ys, x1, gw.T, norm_f_g.reshape(1, d))
    return out.reshape(batch, seq, d)
```

```python
import functools

import numpy as np
import jax
import jax.numpy as jnp
from jax import lax
from jax.experimental import pallas as pl
from jax.experimental.pallas import tpu as pltpu

D_MODEL = 2048
ATTN_HEADS = 8
KV_HEADS = 2
HEAD_DIM = 128
ATTN_WIDTH = ATTN_HEADS * HEAD_DIM
KV_WIDTH = KV_HEADS * HEAD_DIM
IDX_HEADS = 16
IDX_DIM = 64
TOPK_MAX = 256
RNN_WIDTH = 1024
RNN_BLOCKS = 8
RNN_BLOCK_W = RNN_WIDTH // RNN_BLOCKS
CONV_WIDTH = 4
LRU_C = 8.0
REL_BUCKETS = 32
REL_MAX_DIST = 128
N_GROUPS = 8
EXPERTS_PER_GROUP = 8
N_EXPERTS = N_GROUPS * EXPERTS_PER_GROUP
TOP_K_IN_GROUP = 2
EXPERT_FF = 512
MOE_BLOCK = 128
MOE_SHIFT = 7
EPS = 1e-6

LANES = 128
QB = 128
KT = 256
NEG_BIG = -1e30
INT_MIN = -(2 ** 31)
LOG2E = np.float32(np.log2(np.e))
F32 = jnp.float32
BF16 = jnp.bfloat16
I32 = jnp.int32

OFF_Q = 0
OFF_K = OFF_Q + ATTN_WIDTH
OFF_V = OFF_K + KV_WIDTH
OFF_QI = OFF_V + KV_WIDTH
OFF_KI = OFF_QI + IDX_HEADS * IDX_DIM
OFF_WI = OFF_KI + IDX_DIM
OFF_XR = OFF_WI + IDX_HEADS
OFF_RG = OFF_XR + RNN_WIDTH
OFF_GA = OFF_RG + RNN_WIDTH
OFF_GR = OFF_GA + D_MODEL
IN_COLS = OFF_GR + D_MODEL
PA_COLS = OFF_KI
PC_MAIN = IN_COLS - OFF_XR
PC_COLS = 6400


def _rel_bucket_np(n):
    n = np.maximum(n, 0)
    max_exact = REL_BUCKETS // 2
    nf = np.maximum(n, 1).astype(np.float32)
    large = max_exact + (np.log(nf / np.float32(max_exact)) / np.float32(np.log(REL_MAX_DIST / max_exact))
                         * np.float32(REL_BUCKETS - max_exact)).astype(np.int32)
    large = np.minimum(large, REL_BUCKETS - 1)
    return np.where(n < max_exact, n, large).astype(np.int32)


def _norm_proj_kernel(x_ref, g_ref, w_ref, o_ref, u_ref):
    @pl.when(pl.program_id(1) == 0)
    def _():
        x = x_ref[...]
        ms = jnp.mean(x * x, axis=-1, keepdims=True)
        u_ref[...] = (x * lax.rsqrt(ms + EPS) * g_ref[...]).astype(BF16)

    o_ref[...] = jnp.dot(u_ref[...], w_ref[...], preferred_element_type=F32).astype(o_ref.dtype)


def _norm_proj(x2, g, w, out_dtype, tm, tn):
    m, d = x2.shape
    n = w.shape[1]
    return pl.pallas_call(
        _norm_proj_kernel,
        grid=(m // tm, n // tn),
        in_specs=[
            pl.BlockSpec((tm, d), lambda i, j: (i, 0)),
            pl.BlockSpec((1, d), lambda i, j: (0, 0)),
            pl.BlockSpec((d, tn), lambda i, j: (0, j)),
        ],
        out_specs=pl.BlockSpec((tm, tn), lambda i, j: (i, j)),
        out_shape=jax.ShapeDtypeStruct((m, n), out_dtype),
        scratch_shapes=[pltpu.VMEM((tm, d), BF16)],
        compiler_params=pltpu.CompilerParams(dimension_semantics=("arbitrary", "arbitrary")),
        name="norm_proj",
    )(x2, g, w)


def _for_tiles_by_two(lo, hi, tile_fn):
    n = hi - lo

    def pair(i, carry):
        tile_fn(lo + 2 * i)
        tile_fn(lo + 2 * i + 1)
        return carry

    lax.fori_loop(0, n // 2, pair, 0)

    @pl.when(n % 2 == 1)
    def _():
        tile_fn(hi - 1)


def _dsa_kernel(relb_ref, qit_ref, wit_ref, ki_ref, k_ref, qt_ref, vt_ref, bkt_ref, o_ref,
                keys_ref, bias_ref, m_ref, l_ref, acc_ref, s_ref):
    b = pl.program_id(0)
    j = pl.program_id(1)

    @pl.when((b == 0) & (j == 0))
    def _():
        for d in range(2):
            bk = bkt_ref[d]
            for h in range(ATTN_HEADS):
                tile = jnp.zeros((QB, QB), F32)
                for bb in range(REL_BUCKETS):
                    tile = jnp.where(bk == bb, relb_ref[bb, h] * LOG2E, tile)
                bias_ref[d, h] = tile

    t0 = j * QB
    ntile = (j + 2) // 2
    q_pos = t0 + lax.broadcasted_iota(I32, (KT, QB), 1)
    row_iota = lax.broadcasted_iota(I32, (KT, QB), 0)

    w_scale = np.float32(IDX_HEADS ** -0.5) * np.float32(IDX_DIM ** -0.5)
    wrow = wit_ref[...] * w_scale

    def score_tile(kt):
        r0 = pl.multiple_of(kt * KT, KT)
        ki_t = ki_ref[pl.ds(r0, KT), :]
        acc = jnp.zeros((KT, QB), F32)
        for c in range(4):
            dots = jnp.dot(ki_t, qit_ref[:, c * 512:(c + 1) * 512], preferred_element_type=F32)
            for hh in range(4):
                h = c * 4 + hh
                acc = acc + jnp.maximum(dots[:, hh * QB:(hh + 1) * QB], 0.0) * wrow[:, h * QB:(h + 1) * QB]
        bits = pltpu.bitcast(acc, I32)
        key = bits ^ ((bits >> 31) & jnp.int32(0x7FFFFFFF))
        key = jnp.where(r0 + row_iota <= q_pos, key, jnp.int32(INT_MIN))
        keys_ref[pl.ds(r0, KT), :] = key

    _for_tiles_by_two(0, ntile, score_tile)

    n_acc = 4

    def count_ge(cand):
        cand8 = jnp.broadcast_to(cand, (8, QB))

        def body(kt, accs):
            r0 = pl.multiple_of(kt * KT, KT)
            accs = list(accs)
            for v in range(KT // 8):
                blk = keys_ref[pl.ds(r0 + v * 8, 8), :]
                accs[v % n_acc] = accs[v % n_acc] + jnp.where(blk >= cand8, jnp.int32(1), jnp.int32(0))
            return tuple(accs)

        accs = lax.fori_loop(0, ntile, body, (jnp.zeros((8, QB), I32),) * n_acc)
        acc8 = (accs[0] + accs[1]) + (accs[2] + accs[3])
        return jnp.sum(acc8.astype(F32), axis=0, keepdims=True).astype(I32)

    c0 = count_ge(jnp.zeros((1, QB), I32))
    ok0 = c0 >= TOPK_MAX
    thr0 = jnp.where(ok0, jnp.int32(0), jnp.int32(INT_MIN))
    cnt0 = jnp.where(ok0, c0, jnp.int32(0))

    def bit_step(i, carry):
        thr, cnt = carry
        cand = thr | (jnp.int32(1) << (30 - i))
        c = count_ge(cand)
        ok = c >= TOPK_MAX
        return jnp.where(ok, cand, thr), jnp.where(ok, c, cnt)

    thr, cnt = lax.fori_loop(0, 31, bit_step, (thr0, cnt0))
    thr = jnp.maximum(thr, jnp.int32(INT_MIN + 1))

    pos_bits = (keys_ref.shape[0] - 1).bit_length()

    @pl.when(jnp.max(cnt.astype(F32)) > TOPK_MAX)
    def _():
        tied = cnt > TOPK_MAX
        need = TOPK_MAX - count_ge(thr + 1)
        thr8 = jnp.broadcast_to(thr, (8, QB))
        sub8 = lax.broadcasted_iota(I32, (8, QB), 0)

        def count_eq_before(limit):
            lim8 = jnp.broadcast_to(limit, (8, QB))

            def body(v, acc):
                r = pl.multiple_of(v * 8, 8)
                hit = jnp.logical_and(keys_ref[pl.ds(r, 8), :] == thr8, r + sub8 < lim8)
                return acc + jnp.where(hit, jnp.int32(1), jnp.int32(0))

            acc = lax.fori_loop(0, ntile * (KT // 8), body, jnp.zeros((8, QB), I32))
            return jnp.sum(acc.astype(F32), axis=0, keepdims=True).astype(I32)

        def pos_step(i, last):
            cand = last | (jnp.int32(1) << (pos_bits - 1 - i))
            return jnp.where(count_eq_before(cand) < need, cand, last)

        last = lax.fori_loop(0, pos_bits, pos_step, jnp.zeros((1, QB), I32))
        last8 = jnp.broadcast_to(jnp.where(tied, last, jnp.int32(2 ** pos_bits)), (8, QB))

        def demote(v, carry):
            r = pl.multiple_of(v * 8, 8)
            blk = keys_ref[pl.ds(r, 8), :]
            drop = jnp.logical_and(blk == thr8, r + sub8 > last8)
            keys_ref[pl.ds(r, 8), :] = jnp.where(drop, thr8 - 1, blk)
            return carry

        lax.fori_loop(0, ntile * (KT // 8), demote, 0)

    qk_scale = np.float32(HEAD_DIM ** -0.5) * LOG2E
    rep = ATTN_HEADS // KV_HEADS
    nfar = jnp.maximum((j - 1) // 2, 0)

    gw = rep * QB
    m_ref[...] = jnp.full(m_ref.shape, NEG_BIG, F32)
    l_ref[...] = jnp.zeros(l_ref.shape, F32)
    acc_ref[...] = jnp.zeros(acc_ref.shape, F32)

    def logits(kt, g, near):
        r0 = pl.multiple_of(kt * KT, KT)
        k_t = k_ref[pl.ds(r0, KT), g * HEAD_DIM:(g + 1) * HEAD_DIM]
        lg = jnp.dot(k_t, qt_ref[g], preferred_element_type=F32) * qk_scale
        sel = keys_ref[pl.ds(r0, KT), :] >= thr
        cols = []
        for r in range(rep):
            h = g * rep + r
            far_bias = relb_ref[REL_BUCKETS - 1, h] * LOG2E
            blk = lg[:, r * QB:(r + 1) * QB]
            if near:
                subs = []
                for sub in range(KT // QB):
                    dblk = j - (kt * (KT // QB) + sub)
                    bias = jnp.where(dblk == 0, bias_ref[0, h],
                                     jnp.where(dblk == 1, bias_ref[1, h], far_bias))
                    subs.append(blk[sub * QB:(sub + 1) * QB, :] + bias)
                blk = jnp.concatenate(subs, axis=0)
            else:
                blk = blk + far_bias
            cols.append(jnp.where(sel, blk, NEG_BIG))
        s = jnp.concatenate(cols, axis=1)
        s_ref[pl.ds(r0, KT), g * gw:(g + 1) * gw] = s
        m_ref[g] = jnp.maximum(m_ref[g], jnp.max(s.reshape(KT // 8, 8, gw), axis=0))

    def far_tile(kt):
        for g in range(KV_HEADS):
            logits(kt, g, False)

    def near_body(kt, carry):
        for g in range(KV_HEADS):
            logits(kt, g, True)
        return carry

    _for_tiles_by_two(0, nfar, far_tile)
    lax.fori_loop(nfar, ntile, near_body, 0)

    m_fin = [jnp.max(m_ref[g], axis=0, keepdims=True) for g in range(KV_HEADS)]

    def weigh(kt):
        r0 = pl.multiple_of(kt * KT, KT)
        for g in range(KV_HEADS):
            p = jnp.exp2(s_ref[pl.ds(r0, KT), g * gw:(g + 1) * gw] - m_fin[g])
            l_ref[g] = l_ref[g] + jnp.sum(p.reshape(KT // 8, 8, gw), axis=0)
            acc_ref[g] = acc_ref[g] + jnp.dot(vt_ref[g, kt], p.astype(BF16), preferred_element_type=F32)

    _for_tiles_by_two(0, ntile, weigh)

    for g in range(KV_HEADS):
        out_t = acc_ref[g] / jnp.sum(l_ref[g], axis=0, keepdims=True)
        for r in range(rep):
            h = g * rep + r
            o_ref[:, h * HEAD_DIM:(h + 1) * HEAD_DIM] = out_t[:, r * QB:(r + 1) * QB].T.astype(o_ref.dtype)


def _dsa_attention(rel_bias, qit, wit, ki, k, qt, vt, bkt, batch, seq):
    nblk = seq // QB
    rep = ATTN_HEADS // KV_HEADS
    return pl.pallas_call(
        _dsa_kernel,
        grid=(batch, nblk),
        in_specs=[
            pl.BlockSpec(memory_space=pltpu.SMEM),
            pl.BlockSpec((None, None, IDX_DIM, IDX_HEADS * QB), lambda b, j: (b, j, 0, 0)),
            pl.BlockSpec((None, None, 1, IDX_HEADS * QB), lambda b, j: (b, j, 0, 0)),
            pl.BlockSpec((None, seq, IDX_DIM), lambda b, j: (b, 0, 0)),
            pl.BlockSpec((None, seq, KV_WIDTH), lambda b, j: (b, 0, 0)),
            pl.BlockSpec((None, None, KV_HEADS, HEAD_DIM, rep * QB), lambda b, j: (b, j, 0, 0, 0)),
            pl.BlockSpec((None, KV_HEADS, seq // KT, HEAD_DIM, KT), lambda b, j: (b, 0, 0, 0, 0)),
            pl.BlockSpec((2, QB, QB), lambda b, j: (0, 0, 0)),
        ],
        out_specs=pl.BlockSpec((None, QB, ATTN_WIDTH), lambda b, j: (b, j, 0)),
        out_shape=jax.ShapeDtypeStruct((batch, seq, ATTN_WIDTH), BF16),
        scratch_shapes=[
            pltpu.VMEM((seq, QB), I32),
            pltpu.VMEM((2, ATTN_HEADS, QB, QB), F32),
            pltpu.VMEM((KV_HEADS, 8, rep * QB), F32),
            pltpu.VMEM((KV_HEADS, 8, rep * QB), F32),
            pltpu.VMEM((KV_HEADS, HEAD_DIM, rep * QB), F32),
            pltpu.VMEM((seq, ATTN_HEADS * QB), F32),
        ],
        compiler_params=pltpu.CompilerParams(dimension_semantics=("arbitrary", "arbitrary")),
        name="dsa_attention",
    )(rel_bias, qit, wit, ki, k, qt, vt, bkt)


RNN_TS = 256


def _gelu_tanh(x):
    c = np.float32(np.sqrt(2.0 / np.pi))
    return x * (0.5 * (1.0 + jnp.tanh(c * (x + np.float32(0.044715) * (x * x * x)))))


def _softplus(z):
    return jnp.maximum(z, 0.0) + jnp.log1p(jnp.exp(-jnp.abs(z)))


def _rglru_kernel(xr_ref, gate_ref, cw_ref, cb_ref, wa_ref, ba_ref, wx_ref, bx_ref, lam_ref, o_ref,
                  xext_ref, h_ref):
    i = pl.program_id(1)
    ts = RNN_TS

    @pl.when(i == 0)
    def _():
        xext_ref[0:8, :] = jnp.zeros((8, RNN_WIDTH), F32)
        h_ref[...] = jnp.zeros((1, RNN_WIDTH), F32)

    @pl.when(i > 0)
    def _():
        xext_ref[0:8, :] = xext_ref[ts:ts + 8, :]

    xext_ref[8:8 + ts, :] = xr_ref[...]

    row = lax.broadcasted_iota(I32, (ts, RNN_BLOCK_W), 0)
    for n in range(RNN_BLOCKS):
        cs = slice(n * RNN_BLOCK_W, (n + 1) * RNN_BLOCK_W)
        xc = cb_ref[:, cs]
        for jj in range(CONV_WIDTH):
            off = 8 - (CONV_WIDTH - 1) + jj
            xc = xc + xext_ref[off:off + ts, cs] * cw_ref[jj:jj + 1, cs]
        xcb = xc.astype(BF16)
        r = jax.nn.sigmoid(jnp.dot(xcb, wa_ref[n], preferred_element_type=F32) + ba_ref[n:n + 1, :])
        gi = jax.nn.sigmoid(jnp.dot(xcb, wx_ref[n], preferred_element_type=F32) + bx_ref[n:n + 1, :])
        log_a = (-LRU_C * r) * _softplus(-lam_ref[:, cs])
        a = jnp.exp(log_a)
        mult = jnp.sqrt(1.0 - jnp.exp(2.0 * log_a))
        bt = mult * (gi * xc)
        k = 1
        while k < ts:
            a_s = pltpu.roll(a, k, 0)
            b_s = pltpu.roll(bt, k, 0)
            keep = row >= k
            bt = jnp.where(keep, a * b_s + bt, bt)
            a = jnp.where(keep, a * a_s, a)
            k *= 2
        h = bt + a * h_ref[:, cs]
        h_ref[:, cs] = h[ts - 1:ts, :]
        o_ref[:, cs] = (h * _gelu_tanh(gate_ref[:, cs])).astype(o_ref.dtype)


def _rglru(pc, conv_w, conv_b, wa, ba, wx, bx, lam, batch, seq):
    nts = seq // RNN_TS
    full = lambda shape: pl.BlockSpec(shape, lambda b, i: (0,) * len(shape))
    return pl.pallas_call(
        _rglru_kernel,
        grid=(batch, nts),
        in_specs=[
            pl.BlockSpec((RNN_TS, RNN_WIDTH), lambda b, i: (b * nts + i, 0)),
            pl.BlockSpec((RNN_TS, RNN_WIDTH), lambda b, i: (b * nts + i, 1)),
            full((CONV_WIDTH, RNN_WIDTH)),
            full((1, RNN_WIDTH)),
            full((RNN_BLOCKS, RNN_BLOCK_W, RNN_BLOCK_W)),
            full((RNN_BLOCKS, RNN_BLOCK_W)),
            full((RNN_BLOCKS, RNN_BLOCK_W, RNN_BLOCK_W)),
            full((RNN_BLOCKS, RNN_BLOCK_W)),
            full((1, RNN_WIDTH)),
        ],
        out_specs=pl.BlockSpec((RNN_TS, RNN_WIDTH), lambda b, i: (b * nts + i, 0)),
        out_shape=jax.ShapeDtypeStruct((batch * seq, RNN_WIDTH), BF16),
        scratch_shapes=[pltpu.VMEM((RNN_TS + 8, RNN_WIDTH), F32), pltpu.VMEM((1, RNN_WIDTH), F32)],
        compiler_params=pltpu.CompilerParams(dimension_semantics=("arbitrary", "arbitrary")),
        name="rglru",
    )(pc, pc, conv_w, conv_b, wa, ba, wx, bx, lam)


MERGE_TM = 256
ROW_CHUNKS = D_MODEL // LANES


def _store_token_major(ref, base, val):
    n = val.shape[0]
    for c in range(ROW_CHUNKS):
        ref[pl.ds(base + c, n, stride=ROW_CHUNKS), :] = val[:, c * LANES:(c + 1) * LANES]


def _load_token_major(ref, base, n):
    return jnp.concatenate([ref[pl.ds(base + c, n, stride=ROW_CHUNKS), :] for c in range(ROW_CHUNKS)], axis=1)


def _merge_kernel(attn_ref, rnn_ref, ga_ref, gr_ref, x_ref, wpa_ref, wpr_ref, wo_ref, g2_ref,
                  wrh_ref, wrl_ref, br_ref, x1_ref, u2_ref, lgt_ref):
    pa = jnp.dot(attn_ref[...], wpa_ref[...], preferred_element_type=F32)
    pr = jnp.dot(rnn_ref[...], wpr_ref[...], preferred_element_type=F32)
    merged = jax.nn.sigmoid(ga_ref[...]) * pa + jax.nn.sigmoid(gr_ref[...]) * pr
    x1 = x_ref[...] + jnp.dot(merged.astype(BF16), wo_ref[...], preferred_element_type=F32)
    x1_ref[...] = x1
    ms = jnp.mean(x1 * x1, axis=-1, keepdims=True)
    u2 = x1 * lax.rsqrt(ms + EPS) * g2_ref[...]
    _store_token_major(u2_ref, 0, u2)
    hi = u2.astype(BF16)
    lo = (u2 - hi.astype(F32)).astype(BF16)
    lg = (jnp.dot(hi, wrh_ref[...], preferred_element_type=F32)
          + jnp.dot(lo, wrh_ref[...], preferred_element_type=F32)
          + jnp.dot(hi, wrl_ref[...], preferred_element_type=F32)) + br_ref[...]
    lgt_ref[...] = lg.T


def _merge(attn, rnn, pc, x2, wpa, wpr, wo, g2, wrh, wrl, br):
    m = x2.shape[0]
    tm = MERGE_TM
    const = lambda shape: pl.BlockSpec(shape, lambda i: (0,) * len(shape), pipeline_mode=pl.Buffered(1))
    return pl.pallas_call(
        _merge_kernel,
        grid=(m // tm,),
        in_specs=[
            pl.BlockSpec((tm, ATTN_WIDTH), lambda i: (i, 0)),
            pl.BlockSpec((tm, RNN_WIDTH), lambda i: (i, 0)),
            pl.BlockSpec((tm, D_MODEL), lambda i: (i, 1)),
            pl.BlockSpec((tm, D_MODEL), lambda i: (i, 2)),
            pl.BlockSpec((tm, D_MODEL), lambda i: (i, 0)),
            const((ATTN_WIDTH, D_MODEL)),
            const((RNN_WIDTH, D_MODEL)),
            const((D_MODEL, D_MODEL)),
            const((1, D_MODEL)),
            const((D_MODEL, LANES)),
            const((D_MODEL, LANES)),
            const((1, LANES)),
        ],
        out_specs=[
            pl.BlockSpec((tm, D_MODEL), lambda i: (i, 0)),
            pl.BlockSpec((tm * ROW_CHUNKS, LANES), lambda i: (i, 0)),
            pl.BlockSpec((LANES, tm), lambda i: (0, i)),
        ],
        out_shape=[
            jax.ShapeDtypeStruct((m, D_MODEL), F32),
            jax.ShapeDtypeStruct((m * ROW_CHUNKS, LANES), F32),
            jax.ShapeDtypeStruct((LANES, m), F32),
        ],
        compiler_params=pltpu.CompilerParams(dimension_semantics=("arbitrary",)),
        name="merge_outproj",
    )(attn, rnn, pc, pc, x2, wpa, wpr, wo, g2, wrh, wrl, br)


ROUTE_CHUNK = 256


def _first_index_of_max(v, ridx, n):
    vmax = jnp.max(v, axis=0, keepdims=True)
    idx = jnp.min(jnp.where(v == vmax, ridx, jnp.int32(n)).astype(F32), axis=0, keepdims=True)
    return vmax, idx.astype(I32)


def _route_kernel(lgt_ref, dest_ref, gw_ref, meta_ref, cum_ref):
    t = lgt_ref.shape[1]
    eg = EXPERTS_PER_GROUP
    ridx8 = lax.broadcasted_iota(I32, (eg, t), 0)
    gl = lgt_ref[0:N_GROUPS, :]
    gmax, g_sel = _first_index_of_max(gl, ridx8, N_GROUPS)
    p_sel = 1.0 / jnp.sum(jnp.exp(gl - gmax), axis=0, keepdims=True)
    el = lgt_ref[N_GROUPS:N_GROUPS + eg, :]
    for g in range(1, N_GROUPS):
        el = jnp.where(g_sel == g, lgt_ref[N_GROUPS + g * eg:N_GROUPS + (g + 1) * eg, :], el)
    v0, i0 = _first_index_of_max(el, ridx8, eg)
    el1 = jnp.where(ridx8 == i0, -jnp.inf, el)
    v1, i1 = _first_index_of_max(el1, ridx8, eg)
    e1 = jnp.exp(v1 - v0)
    den = 1.0 + e1
    gw_ref[0:1, :] = p_sel * (1.0 / den)
    gw_ref[1:2, :] = p_sel * (e1 / den)
    ex0 = g_sel * eg + i0
    ex1 = g_sel * eg + i1

    eidx = lax.broadcasted_iota(I32, (N_EXPERTS, ROUTE_CHUNK), 0)
    ui = lax.broadcasted_iota(I32, (ROUTE_CHUNK, ROUTE_CHUNK), 0)
    uj = lax.broadcasted_iota(I32, (ROUTE_CHUNK, ROUTE_CHUNK), 1)
    upper = jnp.where(ui < uj, 1.0, 0.0).astype(BF16)
    run = jnp.zeros((N_EXPERTS, 1), F32)
    for c in range(t // ROUTE_CHUNK):
        cs = slice(c * ROUTE_CHUNK, (c + 1) * ROUTE_CHUNK)
        hit = jnp.where(eidx == ex0[:, cs], 1.0, jnp.where(eidx == ex1[:, cs], 1.0, 0.0))
        cum_ref[:, cs] = jnp.dot(hit.astype(BF16), upper, preferred_element_type=F32) + run
        run = run + jnp.sum(hit, axis=1, keepdims=True)

    counts = run.astype(I32)
    padded = ((counts + (MOE_BLOCK - 1)) >> MOE_SHIFT) << MOE_SHIFT
    pe = jnp.broadcast_to(padded, (N_EXPERTS, LANES))
    erow = lax.broadcasted_iota(I32, (N_EXPERTS, LANES), 0)
    k = 1
    while k < N_EXPERTS:
        pe = pe + jnp.where(erow >= k, pltpu.roll(pe, k, 0), 0)
        k *= 2
    pends = pe[:, 0:1]
    pstarts = pends - padded

    eidx_t = lax.broadcasted_iota(I32, (N_EXPERTS, t), 0)
    slot = cum_ref[...] + pstarts.astype(F32)
    dest_ref[0:1, :] = jnp.sum(jnp.where(eidx_t == ex0, slot, 0.0), axis=0, keepdims=True).astype(I32)
    dest_ref[1:2, :] = jnp.sum(jnp.where(eidx_t == ex1, slot, 0.0), axis=0, keepdims=True).astype(I32)

    nb = meta_ref.shape[1]
    own = lax.broadcasted_iota(I32, (N_EXPERTS, nb), 0) == lax.broadcasted_iota(I32, (N_EXPERTS, nb), 1)
    first_blk = (pstarts >> MOE_SHIFT).astype(F32)
    n_blk = (padded >> MOE_SHIFT).astype(F32)
    meta_ref[0:1, :] = jnp.sum(jnp.where(own, first_blk, 0.0), axis=0, keepdims=True).astype(I32)
    meta_ref[1:2, :] = jnp.sum(jnp.where(own, n_blk, 0.0), axis=0, keepdims=True).astype(I32)
    meta_ref[2:3, :] = jnp.broadcast_to(pends[N_EXPERTS - 1:N_EXPERTS, :] >> MOE_SHIFT, (1, nb))


def _route(lgt):
    t = lgt.shape[1]
    return pl.pallas_call(
        _route_kernel,
        out_shape=[
            jax.ShapeDtypeStruct((2, t), I32),
            jax.ShapeDtypeStruct((2, t), F32),
            jax.ShapeDtypeStruct((3, LANES), I32),
        ],
        scratch_shapes=[pltpu.VMEM((N_EXPERTS, t), F32)],
        name="route",
    )(lgt)


W_CHUNKS = 8
GATHER_AHEAD = 3
X_SLOTS = GATHER_AHEAD + 1
W_AHEAD = 2
W_SLOTS = W_AHEAD + 1


def _expert_kernel(first_ref, nblk_ref, nact_ref, dest_ref, u_ref, wg_ref, wu_ref, wd_ref, ys_ref,
                   tok_ref, xbuf_ref, ybuf_ref, wgf_ref, wuf_ref, wdf_ref, wgb_ref, wub_ref, wdb_ref,
                   gsem, osem, wsem):
    e = pl.program_id(0)
    first = first_ref[e]
    nb = nblk_ref[e]
    nact = nact_ref[0]
    ntok = u_ref.shape[0] // ROW_CHUNKS
    nslot = tok_ref.shape[0]
    blk_rows = MOE_BLOCK * ROW_CHUNKS

    def row_copy(blk, slot, r):
        tok = tok_ref[blk * MOE_BLOCK + r]
        src = u_ref.at[pl.ds(pl.multiple_of(tok * ROW_CHUNKS, ROW_CHUNKS), ROW_CHUNKS)]
        dst = xbuf_ref.at[pl.ds(pl.multiple_of(slot * blk_rows + r * ROW_CHUNKS, ROW_CHUNKS), ROW_CHUNKS)]
        return pltpu.make_async_copy(src, dst, gsem.at[slot])

    def start_rows(blk, slot):
        def body(r, carry):
            row_copy(blk, slot, r).start()
            return carry
        lax.fori_loop(0, MOE_BLOCK, body, 0, unroll=8)

    def wait_rows(blk, slot):
        def body(r, carry):
            row_copy(blk, slot, r).wait()
            return carry
        lax.fori_loop(0, MOE_BLOCK, body, 0, unroll=8)

    def out_copy(blk, slot):
        src = ybuf_ref.at[pl.ds(pl.multiple_of(slot * blk_rows, blk_rows), blk_rows)]
        dst = ys_ref.at[pl.ds(pl.multiple_of(blk * blk_rows, blk_rows), blk_rows)]
        return pltpu.make_async_copy(src, dst, osem.at[slot])

    def build_slot_table():
        def clear(p, carry):
            tok_ref[p] = 0
            return carry
        lax.fori_loop(0, nslot, clear, 0, unroll=8)

        def put(t, carry):
            for kk in range(TOP_K_IN_GROUP):
                tok_ref[dest_ref[kk * ntok + t]] = t
            return carry
        lax.fori_loop(0, ntok, put, 0, unroll=8)

    def weight_copies(ex, wslot):
        copies = []
        for src, dst in ((wg_ref, wgf_ref), (wu_ref, wuf_ref), (wd_ref, wdf_ref)):
            rows = src.shape[1] // W_CHUNKS
            for c in range(W_CHUNKS):
                rs = pl.ds(c * rows, rows)
                copies.append(pltpu.make_async_copy(src.at[ex, rs], dst.at[wslot, rs], wsem.at[wslot]))
        return copies

    @pl.when(e == 0)
    def _():
        for a in range(W_AHEAD):
            for cp in weight_copies(a, a):
                cp.start(priority=1)

    @pl.when(e + W_AHEAD < pl.num_programs(0))
    def _():
        for cp in weight_copies(e + W_AHEAD, (e + W_AHEAD) % W_SLOTS):
            cp.start(priority=1)

    @pl.when(e == 0)
    def _():
        build_slot_table()
        for a in range(GATHER_AHEAD):
            @pl.when(a < nact)
            def _():
                start_rows(a, a)

    wslot = e % W_SLOTS
    for cp in weight_copies(e, wslot):
        cp.wait()
    wgb_ref[...] = wgf_ref[wslot].astype(BF16)
    wub_ref[...] = wuf_ref[wslot].astype(BF16)
    wdb_ref[...] = wdf_ref[wslot].astype(BF16)

    def block(blk, carry):
        xslot = blk % X_SLOTS
        yslot = blk % 2

        @pl.when(blk + GATHER_AHEAD < nact)
        def _():
            start_rows(blk + GATHER_AHEAD, (blk + GATHER_AHEAD) % X_SLOTS)

        wait_rows(blk, xslot)
        xb = _load_token_major(xbuf_ref, xslot * blk_rows, MOE_BLOCK).astype(BF16)
        hg = jnp.dot(xb, wgb_ref[...], preferred_element_type=F32)
        hu = jnp.dot(xb, wub_ref[...], preferred_element_type=F32)
        h = (hg * jax.nn.sigmoid(hg)) * hu
        y = jnp.dot(h.astype(BF16), wdb_ref[...], preferred_element_type=F32)

        @pl.when(blk >= 2)
        def _():
            out_copy(blk - 2, yslot).wait()

        _store_token_major(ybuf_ref, yslot * blk_rows, y)
        out_copy(blk, yslot).start()
        return carry

    lax.fori_loop(first, first + nb, block, 0)

    @pl.when(e == pl.num_programs(0) - 1)
    def _():
        @pl.when(nact >= 2)
        def _():
            out_copy(nact - 2, nact % 2).wait()

        @pl.when(nact >= 1)
        def _():
            out_copy(nact - 1, (nact - 1) % 2).wait()

        ntotal = ys_ref.shape[0] // blk_rows
        ybuf_ref[0:blk_rows, :] = jnp.zeros((blk_rows, LANES), ybuf_ref.dtype)

        def fill_start(blk, carry):
            out_copy(blk, 0).start()
            return carry

        def fill_wait(blk, carry):
            out_copy(blk, 0).wait()
            return carry

        lax.fori_loop(nact_ref[0], ntotal, fill_start, 0)
        lax.fori_loop(nact_ref[0], ntotal, fill_wait, 0)


def _experts(first_blk, n_blk, nact, dest_flat, u2, w_gate, w_up, w_down, cap):
    d = D_MODEL
    grid_spec = pltpu.PrefetchScalarGridSpec(
        num_scalar_prefetch=4,
        grid=(N_EXPERTS,),
        in_specs=[pl.BlockSpec(memory_space=pl.ANY)] * 4,
        out_specs=pl.BlockSpec(memory_space=pl.ANY),
        scratch_shapes=[
            pltpu.SMEM((cap,), I32),
            pltpu.VMEM((X_SLOTS * MOE_BLOCK * ROW_CHUNKS, LANES), F32),
            pltpu.VMEM((2 * MOE_BLOCK * ROW_CHUNKS, LANES), F32),
            pltpu.VMEM((W_SLOTS, d, EXPERT_FF), F32),
            pltpu.VMEM((W_SLOTS, d, EXPERT_FF), F32),
            pltpu.VMEM((W_SLOTS, EXPERT_FF, d), F32),
            pltpu.VMEM((d, EXPERT_FF), BF16),
            pltpu.VMEM((d, EXPERT_FF), BF16),
            pltpu.VMEM((EXPERT_FF, d), BF16),
            pltpu.SemaphoreType.DMA((X_SLOTS,)),
            pltpu.SemaphoreType.DMA((2,)),
            pltpu.SemaphoreType.DMA((W_SLOTS,)),
        ],
    )
    return pl.pallas_call(
        _expert_kernel,
        grid_spec=grid_spec,
        out_shape=jax.ShapeDtypeStruct((cap * ROW_CHUNKS, LANES), F32),
        compiler_params=pltpu.CompilerParams(dimension_semantics=("arbitrary",), has_side_effects=True),
        name="experts",
    )(first_blk, n_blk, nact, dest_flat, u2, w_gate, w_up, w_down)


COMB_TOK = 64
COMB_AHEAD = 3
COMB_SLOTS = COMB_AHEAD + 1


def _combine_kernel(dest_ref, ys_ref, x1_ref, gw_ref, gf_ref, o_ref, buf_ref, sem):
    i = pl.program_id(0)
    n = pl.num_programs(0)
    ntok = n * COMB_TOK

    def buf_base(slot, kk):
        return (slot * TOP_K_IN_GROUP + kk) * (COMB_TOK * ROW_CHUNKS)

    def copy(step, slot, tl, kk):
        d = dest_ref[kk * ntok + step * COMB_TOK + tl]
        src = ys_ref.at[pl.ds(pl.multiple_of(d * ROW_CHUNKS, ROW_CHUNKS), ROW_CHUNKS)]
        dst = buf_ref.at[pl.ds(pl.multiple_of(buf_base(slot, kk) + tl * ROW_CHUNKS, ROW_CHUNKS), ROW_CHUNKS)]
        return pltpu.make_async_copy(src, dst, sem.at[slot])

    def start_all(step, slot):
        def body(tl, carry):
            for kk in range(TOP_K_IN_GROUP):
                copy(step, slot, tl, kk).start(priority=kk)
            return carry
        lax.fori_loop(0, COMB_TOK, body, 0, unroll=8)

    def wait_all(step, slot):
        def body(tl, carry):
            for kk in range(TOP_K_IN_GROUP):
                copy(step, slot, tl, kk).wait()
            return carry
        lax.fori_loop(0, COMB_TOK, body, 0, unroll=8)

    @pl.when(i == 0)
    def _():
        for a in range(COMB_AHEAD):
            start_all(a, a)

    @pl.when(i + COMB_AHEAD < n)
    def _():
        start_all(i + COMB_AHEAD, (i + COMB_AHEAD) % COMB_SLOTS)

    slot = i % COMB_SLOTS
    wait_all(i, slot)
    y0 = _load_token_major(buf_ref, buf_base(slot, 0), COMB_TOK)
    y1 = _load_token_major(buf_ref, buf_base(slot, 1), COMB_TOK)
    y = gw_ref[:, 0:1] * y0 + gw_ref[:, 1:2] * y1
    x = x1_ref[...] + y
    ms = jnp.mean(x * x, axis=-1, keepdims=True)
    o_ref[...] = x * lax.rsqrt(ms + EPS) * gf_ref[...]


def _combine(dest, ys, x1, gw_t, gf):
    t, d = x1.shape
    return pl.pallas_call(
        _combine_kernel,
        grid=(t // COMB_TOK,),
        in_specs=[
            pl.BlockSpec(memory_space=pltpu.SMEM),
            pl.BlockSpec(memory_space=pl.ANY),
            pl.BlockSpec((COMB_TOK, d), lambda i: (i, 0)),
            pl.BlockSpec((COMB_TOK, TOP_K_IN_GROUP), lambda i: (i, 0)),
            pl.BlockSpec((1, d), lambda i: (0, 0)),
        ],
        out_specs=pl.BlockSpec((COMB_TOK, d), lambda i: (i, 0)),
        out_shape=jax.ShapeDtypeStruct((t, d), F32),
        scratch_shapes=[
            pltpu.VMEM((COMB_SLOTS * TOP_K_IN_GROUP * COMB_TOK * ROW_CHUNKS, LANES), F32),
            pltpu.SemaphoreType.DMA((COMB_SLOTS,)),
        ],
        compiler_params=pltpu.CompilerParams(dimension_semantics=("arbitrary",)),
        name="combine",
    )(dest, ys, x1, gw_t, gf)


def kernel(x, norm1_g, w_in, conv_w, conv_b, lru_wa, lru_ba, lru_wx, lru_bx, lru_lambda, w_proj_attn,
           w_proj_rnn, w_out, rel_bias, norm2_g, w_group, b_group, w_expert_router, b_expert_router,
           w_gate, w_up, w_down, norm_f_g):
    batch, seq, d = x.shape
    tokens = batch * seq
    nblk = seq // QB
    rep = ATTN_HEADS // KV_HEADS
    x2 = x.reshape(tokens, d)

    w = w_in[0]
    w_a = w[:, :PA_COLS].astype(BF16)
    w_c = jnp.concatenate(
        [w[:, OFF_XR:], w[:, OFF_KI:OFF_XR], jnp.zeros((d, PC_COLS - PC_MAIN - (OFF_XR - OFF_KI)), w.dtype)],
        axis=1).astype(BF16)
    g1 = norm1_g[0].reshape(1, d)
    pa = _norm_proj(x2, g1, w_a, BF16, 1024, 512)
    pc = _norm_proj(x2, g1, w_c, F32, 1024, 1280)

    q = pa[:, OFF_Q:OFF_K].reshape(batch, nblk, QB, KV_HEADS, rep, HEAD_DIM)
    qt = q.transpose(0, 1, 3, 5, 4, 2).reshape(batch, nblk, KV_HEADS, HEAD_DIM, rep * QB)
    k = pa[:, OFF_K:OFF_V].reshape(batch, seq, KV_WIDTH)
    v = pa[:, OFF_V:OFF_QI].reshape(batch, seq // KT, KT, KV_HEADS, HEAD_DIM)
    vt = v.transpose(0, 3, 1, 4, 2)
    qi = pa[:, OFF_QI:OFF_KI].reshape(batch, nblk, QB, IDX_HEADS, IDX_DIM)
    qit = qi.transpose(0, 1, 4, 3, 2).reshape(batch, nblk, IDX_DIM, IDX_HEADS * QB)
    ki = pc[:, PC_MAIN:PC_MAIN + IDX_DIM].astype(BF16).reshape(batch, seq, IDX_DIM)
    wi = pc[:, PC_MAIN + IDX_DIM:PC_MAIN + IDX_DIM + IDX_HEADS].reshape(batch, nblk, QB, IDX_HEADS)
    wit = wi.transpose(0, 1, 3, 2).reshape(batch, nblk, 1, IDX_HEADS * QB)
    ks = np.arange(QB)[:, None]
    qs = np.arange(QB)[None, :]
    bkt = jnp.asarray(np.stack([_rel_bucket_np(qs - ks + QB * dd) for dd in range(2)]))
    attn = _dsa_attention(rel_bias, qit, wit, ki, k, qt, vt, bkt, batch, seq).reshape(tokens, ATTN_WIDTH)

    rnn = _rglru(pc, conv_w[0], conv_b[0].reshape(1, RNN_WIDTH), lru_wa[0].astype(BF16), lru_ba[0],
                 lru_wx[0].astype(BF16), lru_bx[0], lru_lambda[0].reshape(1, RNN_WIDTH), batch, seq)

    w_r = jnp.concatenate([w_group[0], w_expert_router[0],
                           jnp.zeros((d, LANES - N_GROUPS - N_EXPERTS), F32)], axis=1)
    w_rh = w_r.astype(BF16)
    w_rl = (w_r - w_rh.astype(F32)).astype(BF16)
    b_r = jnp.concatenate([b_group[0], b_expert_router[0],
                           jnp.zeros((LANES - N_GROUPS - N_EXPERTS,), F32)]).reshape(1, LANES)
    x1, u2, lgt = _merge(attn, rnn, pc, x2, w_proj_attn[0].astype(BF16), w_proj_rnn[0].astype(BF16),
                         w_out[0].astype(BF16), norm2_g[0].reshape(1, d), w_rh, w_rl, b_r)

    n_slots = tokens * TOP_K_IN_GROUP
    cap = -(-(n_slots + N_EXPERTS * (MOE_BLOCK - 1)) // MOE_BLOCK) * MOE_BLOCK
    dest, gw, meta = _route(lgt)

    dest_flat = dest.reshape(-1)
    ys = _experts(meta[0, :N_EXPERTS], meta[1, :N_EXPERTS], meta[2, :1], dest_flat, u2,
                  w_gate[0], w_up[0], w_down[0], cap)
    out = _combine(dest_flat, ys, x1, gw.T, norm_f_g.reshape(1, d))
    return out.reshape(batch, seq, d)
```

```python
import functools

import numpy as np
import jax
import jax.numpy as jnp
from jax import lax
from jax.experimental import pallas as pl
from jax.experimental.pallas import tpu as pltpu

D_MODEL = 2048
ATTN_HEADS = 8
KV_HEADS = 2
HEAD_DIM = 128
ATTN_WIDTH = ATTN_HEADS * HEAD_DIM
KV_WIDTH = KV_HEADS * HEAD_DIM
IDX_HEADS = 16
IDX_DIM = 64
TOPK_MAX = 256
RNN_WIDTH = 1024
RNN_BLOCKS = 8
RNN_BLOCK_W = RNN_WIDTH // RNN_BLOCKS
CONV_WIDTH = 4
LRU_C = 8.0
REL_BUCKETS = 32
REL_MAX_DIST = 128
N_GROUPS = 8
EXPERTS_PER_GROUP = 8
N_EXPERTS = N_GROUPS * EXPERTS_PER_GROUP
TOP_K_IN_GROUP = 2
EXPERT_FF = 512
MOE_BLOCK = 128
MOE_SHIFT = 7
EPS = 1e-6

LANES = 128
QB = 128
KT = 256
NEG_BIG = -1e30
INT_MIN = -(2 ** 31)
LOG2E = np.float32(np.log2(np.e))
F32 = jnp.float32
BF16 = jnp.bfloat16
I32 = jnp.int32

OFF_Q = 0
OFF_K = OFF_Q + ATTN_WIDTH
OFF_V = OFF_K + KV_WIDTH
OFF_QI = OFF_V + KV_WIDTH
OFF_KI = OFF_QI + IDX_HEADS * IDX_DIM
OFF_WI = OFF_KI + IDX_DIM
OFF_XR = OFF_WI + IDX_HEADS
OFF_RG = OFF_XR + RNN_WIDTH
OFF_GA = OFF_RG + RNN_WIDTH
OFF_GR = OFF_GA + D_MODEL
IN_COLS = OFF_GR + D_MODEL
PA_COLS = OFF_KI
PC_MAIN = IN_COLS - OFF_XR
PC_COLS = 6400


def _rel_bucket_np(n):
    n = np.maximum(n, 0)
    max_exact = REL_BUCKETS // 2
    nf = np.maximum(n, 1).astype(np.float32)
    large = max_exact + (np.log(nf / np.float32(max_exact)) / np.float32(np.log(REL_MAX_DIST / max_exact))
                         * np.float32(REL_BUCKETS - max_exact)).astype(np.int32)
    large = np.minimum(large, REL_BUCKETS - 1)
    return np.where(n < max_exact, n, large).astype(np.int32)


def _norm_proj_kernel(x_ref, g_ref, w_ref, o_ref, u_ref):
    @pl.when(pl.program_id(1) == 0)
    def _():
        x = x_ref[...]
        ms = jnp.mean(x * x, axis=-1, keepdims=True)
        u_ref[...] = (x * lax.rsqrt(ms + EPS) * g_ref[...]).astype(BF16)

    o_ref[...] = jnp.dot(u_ref[...], w_ref[...], preferred_element_type=F32).astype(o_ref.dtype)


def _norm_proj(x2, g, w, out_dtype, tm, tn):
    m, d = x2.shape
    n = w.shape[1]
    return pl.pallas_call(
        _norm_proj_kernel,
        grid=(m // tm, n // tn),
        in_specs=[
            pl.BlockSpec((tm, d), lambda i, j: (i, 0)),
            pl.BlockSpec((1, d), lambda i, j: (0, 0)),
            pl.BlockSpec((d, tn), lambda i, j: (0, j)),
        ],
        out_specs=pl.BlockSpec((tm, tn), lambda i, j: (i, j)),
        out_shape=jax.ShapeDtypeStruct((m, n), out_dtype),
        scratch_shapes=[pltpu.VMEM((tm, d), BF16)],
        compiler_params=pltpu.CompilerParams(dimension_semantics=("arbitrary", "arbitrary")),
        name="norm_proj",
    )(x2, g, w)


WT_ROWS = 2560
WT_Q, WT_QI, WT_V = 0, ATTN_WIDTH, ATTN_WIDTH + IDX_HEADS * IDX_DIM
KK_COLS = 384


def _attn_proj_kernel(x_ref, g_ref, wt_ref, wtwi_ref, wkk_ref, ot_ref, wit_ref, okk_ref, u_ref):
    nt = (((1,), (1,)), ((), ()))

    @pl.when(pl.program_id(1) == 0)
    def _():
        x = x_ref[...]
        ms = jnp.mean(x * x, axis=-1, keepdims=True)
        u = (x * lax.rsqrt(ms + EPS) * g_ref[...]).astype(BF16)
        u_ref[...] = u
        okk_ref[...] = jnp.dot(u, wkk_ref[...], preferred_element_type=F32).astype(okk_ref.dtype)
        wit_ref[...] = lax.dot_general(wtwi_ref[...], u, nt, preferred_element_type=F32)

    ot_ref[...] = lax.dot_general(wt_ref[...], u_ref[...], nt, preferred_element_type=F32).astype(ot_ref.dtype)


def _attn_proj(x2, g, wt_all, wt_wi, w_kk, tm, tr):
    m, d = x2.shape
    return pl.pallas_call(
        _attn_proj_kernel,
        grid=(m // tm, WT_ROWS // tr),
        in_specs=[
            pl.BlockSpec((tm, d), lambda i, j: (i, 0)),
            pl.BlockSpec((1, d), lambda i, j: (0, 0)),
            pl.BlockSpec((tr, d), lambda i, j: (j, 0)),
            pl.BlockSpec((LANES, d), lambda i, j: (0, 0)),
            pl.BlockSpec((d, KK_COLS), lambda i, j: (0, 0)),
        ],
        out_specs=[
            pl.BlockSpec((tr, tm), lambda i, j: (j, i)),
            pl.BlockSpec((LANES, tm), lambda i, j: (0, i)),
            pl.BlockSpec((tm, KK_COLS), lambda i, j: (i, 0)),
        ],
        out_shape=[
            jax.ShapeDtypeStruct((WT_ROWS, m), BF16),
            jax.ShapeDtypeStruct((LANES, m), F32),
            jax.ShapeDtypeStruct((m, KK_COLS), BF16),
        ],
        scratch_shapes=[pltpu.VMEM((tm, d), BF16)],
        compiler_params=pltpu.CompilerParams(dimension_semantics=("arbitrary", "arbitrary")),
        name="attn_proj",
    )(x2, g, wt_all, wt_wi, w_kk)


def _for_tiles_by_two(lo, hi, tile_fn):
    n = hi - lo

    def pair(i, carry):
        tile_fn(lo + 2 * i)
        tile_fn(lo + 2 * i + 1)
        return carry

    lax.fori_loop(0, n // 2, pair, 0)

    @pl.when(n % 2 == 1)
    def _():
        tile_fn(hi - 1)


def _dsa_kernel(relb_ref, q_ref, qi_ref, wi_ref, kk_ref, vt_ref, bkt_ref, o_ref,
                keys_ref, bias_ref, m_ref, l_ref, acc_ref, s_ref):
    b = pl.program_id(0)
    j = pl.program_id(1)

    @pl.when((b == 0) & (j == 0))
    def _():
        for d in range(2):
            bk = bkt_ref[d]
            for h in range(ATTN_HEADS):
                tile = jnp.zeros((QB, QB), F32)
                for bb in range(REL_BUCKETS):
                    tile = jnp.where(bk == bb, relb_ref[bb, h] * LOG2E, tile)
                bias_ref[d, h] = tile

    t0 = j * QB
    ntile = (j + 2) // 2
    q_pos = t0 + lax.broadcasted_iota(I32, (KT, QB), 1)
    row_iota = lax.broadcasted_iota(I32, (KT, QB), 0)

    w_scale = np.float32(IDX_HEADS ** -0.5) * np.float32(IDX_DIM ** -0.5)
    wrows = wi_ref[0:IDX_HEADS, :] * w_scale

    def score_tile(kt):
        r0 = pl.multiple_of(kt * KT, KT)
        ki_t = kk_ref[pl.ds(r0, KT), KV_WIDTH:KV_WIDTH + IDX_DIM]
        acc = jnp.zeros((KT, QB), F32)
        for c in range(4):
            qi_c = jnp.concatenate([qi_ref[(c * 4 + hh) * IDX_DIM:(c * 4 + hh + 1) * IDX_DIM, :]
                                    for hh in range(4)], axis=1)
            dots = jnp.dot(ki_t, qi_c, preferred_element_type=F32)
            for hh in range(4):
                h = c * 4 + hh
                acc = acc + jnp.maximum(dots[:, hh * QB:(hh + 1) * QB], 0.0) * wrows[h:h + 1, :]
        bits = pltpu.bitcast(acc, I32)
        key = bits ^ ((bits >> 31) & jnp.int32(0x7FFFFFFF))
        key = jnp.where(r0 + row_iota <= q_pos, key, jnp.int32(INT_MIN))
        keys_ref[pl.ds(r0, KT), :] = key

    _for_tiles_by_two(0, ntile, score_tile)

    n_acc = 4

    def count_ge(cand):
        cand8 = jnp.broadcast_to(cand, (8, QB))

        def body(kt, accs):
            r0 = pl.multiple_of(kt * KT, KT)
            accs = list(accs)
            for v in range(KT // 8):
                blk = keys_ref[pl.ds(r0 + v * 8, 8), :]
                accs[v % n_acc] = accs[v % n_acc] + jnp.where(blk >= cand8, jnp.int32(1), jnp.int32(0))
            return tuple(accs)

        accs = lax.fori_loop(0, ntile, body, (jnp.zeros((8, QB), I32),) * n_acc)
        acc8 = (accs[0] + accs[1]) + (accs[2] + accs[3])
        return jnp.sum(acc8.astype(F32), axis=0, keepdims=True).astype(I32)

    c0 = count_ge(jnp.zeros((1, QB), I32))
    ok0 = c0 >= TOPK_MAX
    thr0 = jnp.where(ok0, jnp.int32(0), jnp.int32(INT_MIN))
    cnt0 = jnp.where(ok0, c0, jnp.int32(0))

    def bit_step(i, carry):
        thr, cnt = carry
        cand = thr | (jnp.int32(1) << (30 - i))
        c = count_ge(cand)
        ok = c >= TOPK_MAX
        return jnp.where(ok, cand, thr), jnp.where(ok, c, cnt)

    thr, cnt = lax.fori_loop(0, 31, bit_step, (thr0, cnt0))
    thr = jnp.maximum(thr, jnp.int32(INT_MIN + 1))

    pos_bits = (keys_ref.shape[0] - 1).bit_length()

    @pl.when(jnp.max(cnt.astype(F32)) > TOPK_MAX)
    def _():
        tied = cnt > TOPK_MAX
        need = TOPK_MAX - count_ge(thr + 1)
        thr8 = jnp.broadcast_to(thr, (8, QB))
        sub8 = lax.broadcasted_iota(I32, (8, QB), 0)

        def count_eq_before(limit):
            lim8 = jnp.broadcast_to(limit, (8, QB))

            def body(v, acc):
                r = pl.multiple_of(v * 8, 8)
                hit = jnp.logical_and(keys_ref[pl.ds(r, 8), :] == thr8, r + sub8 < lim8)
                return acc + jnp.where(hit, jnp.int32(1), jnp.int32(0))

            acc = lax.fori_loop(0, ntile * (KT // 8), body, jnp.zeros((8, QB), I32))
            return jnp.sum(acc.astype(F32), axis=0, keepdims=True).astype(I32)

        def pos_step(i, last):
            cand = last | (jnp.int32(1) << (pos_bits - 1 - i))
            return jnp.where(count_eq_before(cand) < need, cand, last)

        last = lax.fori_loop(0, pos_bits, pos_step, jnp.zeros((1, QB), I32))
        last8 = jnp.broadcast_to(jnp.where(tied, last, jnp.int32(2 ** pos_bits)), (8, QB))

        def demote(v, carry):
            r = pl.multiple_of(v * 8, 8)
            blk = keys_ref[pl.ds(r, 8), :]
            drop = jnp.logical_and(blk == thr8, r + sub8 > last8)
            keys_ref[pl.ds(r, 8), :] = jnp.where(drop, thr8 - 1, blk)
            return carry

        lax.fori_loop(0, ntile * (KT // 8), demote, 0)

    qk_scale = np.float32(HEAD_DIM ** -0.5) * LOG2E
    rep = ATTN_HEADS // KV_HEADS
    nfar = jnp.maximum((j - 1) // 2, 0)

    gw = rep * QB
    m_ref[...] = jnp.full(m_ref.shape, NEG_BIG, F32)
    l_ref[...] = jnp.zeros(l_ref.shape, F32)
    acc_ref[...] = jnp.zeros(acc_ref.shape, F32)

    def logits(kt, g, near):
        r0 = pl.multiple_of(kt * KT, KT)
        k_t = kk_ref[pl.ds(r0, KT), g * HEAD_DIM:(g + 1) * HEAD_DIM]
        q_g = jnp.concatenate([q_ref[(g * rep + r) * HEAD_DIM:(g * rep + r + 1) * HEAD_DIM, :]
                               for r in range(rep)], axis=1)
        lg = jnp.dot(k_t, q_g, preferred_element_type=F32) * qk_scale
        sel = keys_ref[pl.ds(r0, KT), :] >= thr
        cols = []
        for r in range(rep):
            h = g * rep + r
            far_bias = relb_ref[REL_BUCKETS - 1, h] * LOG2E
            blk = lg[:, r * QB:(r + 1) * QB]
            if near:
                subs = []
                for sub in range(KT // QB):
                    dblk = j - (kt * (KT // QB) + sub)
                    bias = jnp.where(dblk == 0, bias_ref[0, h],
                                     jnp.where(dblk == 1, bias_ref[1, h], far_bias))
                    subs.append(blk[sub * QB:(sub + 1) * QB, :] + bias)
                blk = jnp.concatenate(subs, axis=0)
            else:
                blk = blk + far_bias
            cols.append(jnp.where(sel, blk, NEG_BIG))
        s = jnp.concatenate(cols, axis=1)
        s_ref[pl.ds(r0, KT), g * gw:(g + 1) * gw] = s
        m_ref[g] = jnp.maximum(m_ref[g], jnp.max(s.reshape(KT // 8, 8, gw), axis=0))

    def far_tile(kt):
        for g in range(KV_HEADS):
            logits(kt, g, False)

    def near_body(kt, carry):
        for g in range(KV_HEADS):
            logits(kt, g, True)
        return carry

    _for_tiles_by_two(0, nfar, far_tile)
    lax.fori_loop(nfar, ntile, near_body, 0)

    m_fin = [jnp.max(m_ref[g], axis=0, keepdims=True) for g in range(KV_HEADS)]

    def weigh(kt):
        r0 = pl.multiple_of(kt * KT, KT)
        for g in range(KV_HEADS):
            p = jnp.exp2(s_ref[pl.ds(r0, KT), g * gw:(g + 1) * gw] - m_fin[g])
            l_ref[g] = l_ref[g] + jnp.sum(p.reshape(KT // 8, 8, gw), axis=0)
            v_t = vt_ref[g * HEAD_DIM:(g + 1) * HEAD_DIM, pl.ds(r0, KT)]
            acc_ref[g] = acc_ref[g] + jnp.dot(v_t, p.astype(BF16), preferred_element_type=F32)

    _for_tiles_by_two(0, ntile, weigh)

    for g in range(KV_HEADS):
        out_t = acc_ref[g] / jnp.sum(l_ref[g], axis=0, keepdims=True)
        for r in range(rep):
            h = g * rep + r
            o_ref[:, h * HEAD_DIM:(h + 1) * HEAD_DIM] = out_t[:, r * QB:(r + 1) * QB].T.astype(o_ref.dtype)


def _dsa_attention(rel_bias, ot, wit, okk, bkt, batch, seq):
    nblk = seq // QB
    rep = ATTN_HEADS // KV_HEADS
    return pl.pallas_call(
        _dsa_kernel,
        grid=(batch, nblk),
        in_specs=[
            pl.BlockSpec(memory_space=pltpu.SMEM),
            pl.BlockSpec((ATTN_WIDTH, QB), lambda b, j: (WT_Q // ATTN_WIDTH, b * nblk + j)),
            pl.BlockSpec((IDX_HEADS * IDX_DIM, QB), lambda b, j: (WT_QI // (IDX_HEADS * IDX_DIM), b * nblk + j)),
            pl.BlockSpec((LANES, QB), lambda b, j: (0, b * nblk + j)),
            pl.BlockSpec((seq, KK_COLS), lambda b, j: (b, 0)),
            pl.BlockSpec((KV_WIDTH, seq), lambda b, j: (WT_V // KV_WIDTH, b)),
            pl.BlockSpec((2, QB, QB), lambda b, j: (0, 0, 0)),
        ],
        out_specs=pl.BlockSpec((None, QB, ATTN_WIDTH), lambda b, j: (b, j, 0)),
        out_shape=jax.ShapeDtypeStruct((batch, seq, ATTN_WIDTH), BF16),
        scratch_shapes=[
            pltpu.VMEM((seq, QB), I32),
            pltpu.VMEM((2, ATTN_HEADS, QB, QB), F32),
            pltpu.VMEM((KV_HEADS, 8, rep * QB), F32),
            pltpu.VMEM((KV_HEADS, 8, rep * QB), F32),
            pltpu.VMEM((KV_HEADS, HEAD_DIM, rep * QB), F32),
            pltpu.VMEM((seq, ATTN_HEADS * QB), F32),
        ],
        compiler_params=pltpu.CompilerParams(dimension_semantics=("arbitrary", "arbitrary")),
        name="dsa_attention",
    )(rel_bias, ot, ot, wit, okk, ot, bkt)


RNN_TS = 256


def _gelu_tanh(x):
    c = np.float32(np.sqrt(2.0 / np.pi))
    return x * (0.5 * (1.0 + jnp.tanh(c * (x + np.float32(0.044715) * (x * x * x)))))


def _softplus(z):
    return jnp.maximum(z, 0.0) + jnp.log1p(jnp.exp(-jnp.abs(z)))


def _rglru_kernel(xr_ref, gate_ref, cw_ref, cb_ref, wa_ref, ba_ref, wx_ref, bx_ref, lam_ref, o_ref,
                  xext_ref, h_ref):
    i = pl.program_id(1)
    ts = RNN_TS

    @pl.when(i == 0)
    def _():
        xext_ref[0:8, :] = jnp.zeros((8, RNN_WIDTH), F32)
        h_ref[...] = jnp.zeros((1, RNN_WIDTH), F32)

    @pl.when(i > 0)
    def _():
        xext_ref[0:8, :] = xext_ref[ts:ts + 8, :]

    xext_ref[8:8 + ts, :] = xr_ref[...]

    row = lax.broadcasted_iota(I32, (ts, RNN_BLOCK_W), 0)
    for n in range(RNN_BLOCKS):
        cs = slice(n * RNN_BLOCK_W, (n + 1) * RNN_BLOCK_W)
        xc = cb_ref[:, cs]
        for jj in range(CONV_WIDTH):
            off = 8 - (CONV_WIDTH - 1) + jj
            xc = xc + xext_ref[off:off + ts, cs] * cw_ref[jj:jj + 1, cs]
        xcb = xc.astype(BF16)
        r = jax.nn.sigmoid(jnp.dot(xcb, wa_ref[n], preferred_element_type=F32) + ba_ref[n:n + 1, :])
        gi = jax.nn.sigmoid(jnp.dot(xcb, wx_ref[n], preferred_element_type=F32) + bx_ref[n:n + 1, :])
        log_a = (-LRU_C * r) * _softplus(-lam_ref[:, cs])
        a = jnp.exp(log_a)
        mult = jnp.sqrt(1.0 - jnp.exp(2.0 * log_a))
        bt = mult * (gi * xc)
        k = 1
        while k < ts:
            a_s = pltpu.roll(a, k, 0)
            b_s = pltpu.roll(bt, k, 0)
            keep = row >= k
            bt = jnp.where(keep, a * b_s + bt, bt)
            a = jnp.where(keep, a * a_s, a)
            k *= 2
        h = bt + a * h_ref[:, cs]
        h_ref[:, cs] = h[ts - 1:ts, :]
        o_ref[:, cs] = (h * _gelu_tanh(gate_ref[:, cs])).astype(o_ref.dtype)


def _rglru(pc, conv_w, conv_b, wa, ba, wx, bx, lam, batch, seq):
    nts = seq // RNN_TS
    full = lambda shape: pl.BlockSpec(shape, lambda b, i: (0,) * len(shape))
    return pl.pallas_call(
        _rglru_kernel,
        grid=(batch, nts),
        in_specs=[
            pl.BlockSpec((RNN_TS, RNN_WIDTH), lambda b, i: (b * nts + i, 0)),
            pl.BlockSpec((RNN_TS, RNN_WIDTH), lambda b, i: (b * nts + i, 1)),
            full((CONV_WIDTH, RNN_WIDTH)),
            full((1, RNN_WIDTH)),
            full((RNN_BLOCKS, RNN_BLOCK_W, RNN_BLOCK_W)),
            full((RNN_BLOCKS, RNN_BLOCK_W)),
            full((RNN_BLOCKS, RNN_BLOCK_W, RNN_BLOCK_W)),
            full((RNN_BLOCKS, RNN_BLOCK_W)),
            full((1, RNN_WIDTH)),
        ],
        out_specs=pl.BlockSpec((RNN_TS, RNN_WIDTH), lambda b, i: (b * nts + i, 0)),
        out_shape=jax.ShapeDtypeStruct((batch * seq, RNN_WIDTH), BF16),
        scratch_shapes=[pltpu.VMEM((RNN_TS + 8, RNN_WIDTH), F32), pltpu.VMEM((1, RNN_WIDTH), F32)],
        compiler_params=pltpu.CompilerParams(dimension_semantics=("arbitrary", "arbitrary")),
        name="rglru",
    )(pc, pc, conv_w, conv_b, wa, ba, wx, bx, lam)


MERGE_TM = 256
ROW_CHUNKS = D_MODEL // LANES


def _store_token_major(ref, base, val):
    n = val.shape[0]
    for c in range(ROW_CHUNKS):
        ref[pl.ds(base + c, n, stride=ROW_CHUNKS), :] = val[:, c * LANES:(c + 1) * LANES]


def _load_token_major(ref, base, n):
    return jnp.concatenate([ref[pl.ds(base + c, n, stride=ROW_CHUNKS), :] for c in range(ROW_CHUNKS)], axis=1)


def _merge_kernel(attn_ref, rnn_ref, ga_ref, gr_ref, x_ref, wpa_ref, wpr_ref, wo_ref, g2_ref,
                  wrh_ref, wrl_ref, br_ref, x1_ref, u2_ref, lgt_ref):
    pa = jnp.dot(attn_ref[...], wpa_ref[...], preferred_element_type=F32)
    pr = jnp.dot(rnn_ref[...], wpr_ref[...], preferred_element_type=F32)
    merged = jax.nn.sigmoid(ga_ref[...]) * pa + jax.nn.sigmoid(gr_ref[...]) * pr
    x1 = x_ref[...] + jnp.dot(merged.astype(BF16), wo_ref[...], preferred_element_type=F32)
    x1_ref[...] = x1
    ms = jnp.mean(x1 * x1, axis=-1, keepdims=True)
    u2 = x1 * lax.rsqrt(ms + EPS) * g2_ref[...]
    _store_token_major(u2_ref, 0, u2)
    hi = u2.astype(BF16)
    lo = (u2 - hi.astype(F32)).astype(BF16)
    lg = (jnp.dot(hi, wrh_ref[...], preferred_element_type=F32)
          + jnp.dot(lo, wrh_ref[...], preferred_element_type=F32)
          + jnp.dot(hi, wrl_ref[...], preferred_element_type=F32)) + br_ref[...]
    lgt_ref[...] = lg.T


def _merge(attn, rnn, pc, x2, wpa, wpr, wo, g2, wrh, wrl, br):
    m = x2.shape[0]
    tm = MERGE_TM
    const = lambda shape: pl.BlockSpec(shape, lambda i: (0,) * len(shape), pipeline_mode=pl.Buffered(1))
    return pl.pallas_call(
        _merge_kernel,
        grid=(m // tm,),
        in_specs=[
            pl.BlockSpec((tm, ATTN_WIDTH), lambda i: (i, 0)),
            pl.BlockSpec((tm, RNN_WIDTH), lambda i: (i, 0)),
            pl.BlockSpec((tm, D_MODEL), lambda i: (i, 1)),
            pl.BlockSpec((tm, D_MODEL), lambda i: (i, 2)),
            pl.BlockSpec((tm, D_MODEL), lambda i: (i, 0)),
            const((ATTN_WIDTH, D_MODEL)),
            const((RNN_WIDTH, D_MODEL)),
            const((D_MODEL, D_MODEL)),
            const((1, D_MODEL)),
            const((D_MODEL, LANES)),
            const((D_MODEL, LANES)),
            const((1, LANES)),
        ],
        out_specs=[
            pl.BlockSpec((tm, D_MODEL), lambda i: (i, 0)),
            pl.BlockSpec((tm * ROW_CHUNKS, LANES), lambda i: (i, 0)),
            pl.BlockSpec((LANES, tm), lambda i: (0, i)),
        ],
        out_shape=[
            jax.ShapeDtypeStruct((m, D_MODEL), F32),
            jax.ShapeDtypeStruct((m * ROW_CHUNKS, LANES), F32),
            jax.ShapeDtypeStruct((LANES, m), F32),
        ],
        compiler_params=pltpu.CompilerParams(dimension_semantics=("arbitrary",)),
        name="merge_outproj",
    )(attn, rnn, pc, pc, x2, wpa, wpr, wo, g2, wrh, wrl, br)


ROUTE_CHUNK = 256


def _first_index_of_max(v, ridx, n):
    vmax = jnp.max(v, axis=0, keepdims=True)
    idx = jnp.min(jnp.where(v == vmax, ridx, jnp.int32(n)).astype(F32), axis=0, keepdims=True)
    return vmax, idx.astype(I32)


def _route_kernel(lgt_ref, dest_ref, gw_ref, meta_ref, cum_ref):
    t = lgt_ref.shape[1]
    eg = EXPERTS_PER_GROUP
    ridx8 = lax.broadcasted_iota(I32, (eg, t), 0)
    gl = lgt_ref[0:N_GROUPS, :]
    gmax, g_sel = _first_index_of_max(gl, ridx8, N_GROUPS)
    p_sel = 1.0 / jnp.sum(jnp.exp(gl - gmax), axis=0, keepdims=True)
    el = lgt_ref[N_GROUPS:N_GROUPS + eg, :]
    for g in range(1, N_GROUPS):
        el = jnp.where(g_sel == g, lgt_ref[N_GROUPS + g * eg:N_GROUPS + (g + 1) * eg, :], el)
    v0, i0 = _first_index_of_max(el, ridx8, eg)
    el1 = jnp.where(ridx8 == i0, -jnp.inf, el)
    v1, i1 = _first_index_of_max(el1, ridx8, eg)
    e1 = jnp.exp(v1 - v0)
    den = 1.0 + e1
    gw_ref[0:1, :] = p_sel * (1.0 / den)
    gw_ref[1:2, :] = p_sel * (e1 / den)
    ex0 = g_sel * eg + i0
    ex1 = g_sel * eg + i1

    eidx = lax.broadcasted_iota(I32, (N_EXPERTS, ROUTE_CHUNK), 0)
    ui = lax.broadcasted_iota(I32, (ROUTE_CHUNK, ROUTE_CHUNK), 0)
    uj = lax.broadcasted_iota(I32, (ROUTE_CHUNK, ROUTE_CHUNK), 1)
    upper = jnp.where(ui < uj, 1.0, 0.0).astype(BF16)
    run = jnp.zeros((N_EXPERTS, 1), F32)
    for c in range(t // ROUTE_CHUNK):
        cs = slice(c * ROUTE_CHUNK, (c + 1) * ROUTE_CHUNK)
        hit = jnp.where(eidx == ex0[:, cs], 1.0, jnp.where(eidx == ex1[:, cs], 1.0, 0.0))
        cum_ref[:, cs] = jnp.dot(hit.astype(BF16), upper, preferred_element_type=F32) + run
        run = run + jnp.sum(hit, axis=1, keepdims=True)

    counts = run.astype(I32)
    padded = ((counts + (MOE_BLOCK - 1)) >> MOE_SHIFT) << MOE_SHIFT
    pe = jnp.broadcast_to(padded, (N_EXPERTS, LANES))
    erow = lax.broadcasted_iota(I32, (N_EXPERTS, LANES), 0)
    k = 1
    while k < N_EXPERTS:
        pe = pe + jnp.where(erow >= k, pltpu.roll(pe, k, 0), 0)
        k *= 2
    pends = pe[:, 0:1]
    pstarts = pends - padded

    eidx_t = lax.broadcasted_iota(I32, (N_EXPERTS, t), 0)
    slot = cum_ref[...] + pstarts.astype(F32)
    dest_ref[0:1, :] = jnp.sum(jnp.where(eidx_t == ex0, slot, 0.0), axis=0, keepdims=True).astype(I32)
    dest_ref[1:2, :] = jnp.sum(jnp.where(eidx_t == ex1, slot, 0.0), axis=0, keepdims=True).astype(I32)

    nb = meta_ref.shape[1]
    own = lax.broadcasted_iota(I32, (N_EXPERTS, nb), 0) == lax.broadcasted_iota(I32, (N_EXPERTS, nb), 1)
    first_blk = (pstarts >> MOE_SHIFT).astype(F32)
    n_blk = (padded >> MOE_SHIFT).astype(F32)
    meta_ref[0:1, :] = jnp.sum(jnp.where(own, first_blk, 0.0), axis=0, keepdims=True).astype(I32)
    meta_ref[1:2, :] = jnp.sum(jnp.where(own, n_blk, 0.0), axis=0, keepdims=True).astype(I32)
    meta_ref[2:3, :] = jnp.broadcast_to(pends[N_EXPERTS - 1:N_EXPERTS, :] >> MOE_SHIFT, (1, nb))


def _route(lgt):
    t = lgt.shape[1]
    return pl.pallas_call(
        _route_kernel,
        out_shape=[
            jax.ShapeDtypeStruct((2, t), I32),
            jax.ShapeDtypeStruct((2, t), F32),
            jax.ShapeDtypeStruct((3, LANES), I32),
        ],
        scratch_shapes=[pltpu.VMEM((N_EXPERTS, t), F32)],
        name="route",
    )(lgt)


W_CHUNKS = 8
GATHER_AHEAD = 3
X_SLOTS = GATHER_AHEAD + 1
W_AHEAD = 2
W_SLOTS = W_AHEAD + 1


def _expert_kernel(first_ref, nblk_ref, nact_ref, dest_ref, u_ref, wg_ref, wu_ref, wd_ref, ys_ref,
                   tok_ref, xbuf_ref, ybuf_ref, wgf_ref, wuf_ref, wdf_ref, wgb_ref, wub_ref, wdb_ref,
                   gsem, osem, wsem):
    e = pl.program_id(0)
    first = first_ref[e]
    nb = nblk_ref[e]
    nact = nact_ref[0]
    ntok = u_ref.shape[0] // ROW_CHUNKS
    nslot = tok_ref.shape[0]
    blk_rows = MOE_BLOCK * ROW_CHUNKS

    def row_copy(blk, slot, r):
        tok = tok_ref[blk * MOE_BLOCK + r]
        src = u_ref.at[pl.ds(pl.multiple_of(tok * ROW_CHUNKS, ROW_CHUNKS), ROW_CHUNKS)]
        dst = xbuf_ref.at[pl.ds(pl.multiple_of(slot * blk_rows + r * ROW_CHUNKS, ROW_CHUNKS), ROW_CHUNKS)]
        return pltpu.make_async_copy(src, dst, gsem.at[slot])

    def start_rows(blk, slot):
        def body(r, carry):
            row_copy(blk, slot, r).start()
            return carry
        lax.fori_loop(0, MOE_BLOCK, body, 0, unroll=8)

    def wait_rows(blk, slot):
        def body(r, carry):
            row_copy(blk, slot, r).wait()
            return carry
        lax.fori_loop(0, MOE_BLOCK, body, 0, unroll=8)

    def out_copy(blk, slot):
        src = ybuf_ref.at[pl.ds(pl.multiple_of(slot * blk_rows, blk_rows), blk_rows)]
        dst = ys_ref.at[pl.ds(pl.multiple_of(blk * blk_rows, blk_rows), blk_rows)]
        return pltpu.make_async_copy(src, dst, osem.at[slot])

    def build_slot_table():
        def clear(p, carry):
            tok_ref[p] = 0
            return carry
        lax.fori_loop(0, nslot, clear, 0, unroll=8)

        def put(t, carry):
            for kk in range(TOP_K_IN_GROUP):
                tok_ref[dest_ref[kk * ntok + t]] = t
            return carry
        lax.fori_loop(0, ntok, put, 0, unroll=8)

    def weight_copies(ex, wslot):
        copies = []
        for src, dst in ((wg_ref, wgf_ref), (wu_ref, wuf_ref), (wd_ref, wdf_ref)):
            rows = src.shape[1] // W_CHUNKS
            for c in range(W_CHUNKS):
                rs = pl.ds(c * rows, rows)
                copies.append(pltpu.make_async_copy(src.at[ex, rs], dst.at[wslot, rs], wsem.at[wslot]))
        return copies

    @pl.when(e == 0)
    def _():
        for a in range(W_AHEAD):
            for cp in weight_copies(a, a):
                cp.start(priority=1)

    @pl.when(e + W_AHEAD < pl.num_programs(0))
    def _():
        for cp in weight_copies(e + W_AHEAD, (e + W_AHEAD) % W_SLOTS):
            cp.start(priority=1)

    @pl.when(e == 0)
    def _():
        build_slot_table()
        for a in range(GATHER_AHEAD):
            @pl.when(a < nact)
            def _():
                start_rows(a, a)

    wslot = e % W_SLOTS
    for cp in weight_copies(e, wslot):
        cp.wait()
    wgb_ref[...] = wgf_ref[wslot].astype(BF16)
    wub_ref[...] = wuf_ref[wslot].astype(BF16)
    wdb_ref[...] = wdf_ref[wslot].astype(BF16)

    def block(blk, carry):
        xslot = blk % X_SLOTS
        yslot = blk % 2

        @pl.when(blk + GATHER_AHEAD < nact)
        def _():
            start_rows(blk + GATHER_AHEAD, (blk + GATHER_AHEAD) % X_SLOTS)

        wait_rows(blk, xslot)
        xb = _load_token_major(xbuf_ref, xslot * blk_rows, MOE_BLOCK).astype(BF16)
        hg = jnp.dot(xb, wgb_ref[...], preferred_element_type=F32)
        hu = jnp.dot(xb, wub_ref[...], preferred_element_type=F32)
        h = (hg * jax.nn.sigmoid(hg)) * hu
        y = jnp.dot(h.astype(BF16), wdb_ref[...], preferred_element_type=F32)

        @pl.when(blk >= 2)
        def _():
            out_copy(blk - 2, yslot).wait()

        _store_token_major(ybuf_ref, yslot * blk_rows, y)
        out_copy(blk, yslot).start()
        return carry

    lax.fori_loop(first, first + nb, block, 0)

    @pl.when(e == pl.num_programs(0) - 1)
    def _():
        @pl.when(nact >= 2)
        def _():
            out_copy(nact - 2, nact % 2).wait()

        @pl.when(nact >= 1)
        def _():
            out_copy(nact - 1, (nact - 1) % 2).wait()

        ntotal = ys_ref.shape[0] // blk_rows
        ybuf_ref[0:blk_rows, :] = jnp.zeros((blk_rows, LANES), ybuf_ref.dtype)

        def fill_start(blk, carry):
            out_copy(blk, 0).start()
            return carry

        def fill_wait(blk, carry):
            out_copy(blk, 0).wait()
            return carry

        lax.fori_loop(nact_ref[0], ntotal, fill_start, 0)
        lax.fori_loop(nact_ref[0], ntotal, fill_wait, 0)


def _experts(first_blk, n_blk, nact, dest_flat, u2, w_gate, w_up, w_down, cap):
    d = D_MODEL
    grid_spec = pltpu.PrefetchScalarGridSpec(
        num_scalar_prefetch=4,
        grid=(N_EXPERTS,),
        in_specs=[pl.BlockSpec(memory_space=pl.ANY)] * 4,
        out_specs=pl.BlockSpec(memory_space=pl.ANY),
        scratch_shapes=[
            pltpu.SMEM((cap,), I32),
            pltpu.VMEM((X_SLOTS * MOE_BLOCK * ROW_CHUNKS, LANES), F32),
            pltpu.VMEM((2 * MOE_BLOCK * ROW_CHUNKS, LANES), F32),
            pltpu.VMEM((W_SLOTS, d, EXPERT_FF), F32),
            pltpu.VMEM((W_SLOTS, d, EXPERT_FF), F32),
            pltpu.VMEM((W_SLOTS, EXPERT_FF, d), F32),
            pltpu.VMEM((d, EXPERT_FF), BF16),
            pltpu.VMEM((d, EXPERT_FF), BF16),
            pltpu.VMEM((EXPERT_FF, d), BF16),
            pltpu.SemaphoreType.DMA((X_SLOTS,)),
            pltpu.SemaphoreType.DMA((2,)),
            pltpu.SemaphoreType.DMA((W_SLOTS,)),
        ],
    )
    return pl.pallas_call(
        _expert_kernel,
        grid_spec=grid_spec,
        out_shape=jax.ShapeDtypeStruct((cap * ROW_CHUNKS, LANES), F32),
        compiler_params=pltpu.CompilerParams(dimension_semantics=("arbitrary",), has_side_effects=True),
        name="experts",
    )(first_blk, n_blk, nact, dest_flat, u2, w_gate, w_up, w_down)


COMB_TOK = 64
COMB_AHEAD = 3
COMB_SLOTS = COMB_AHEAD + 1


def _combine_kernel(dest_ref, ys_ref, x1_ref, gw_ref, gf_ref, o_ref, buf_ref, sem):
    i = pl.program_id(0)
    n = pl.num_programs(0)
    ntok = n * COMB_TOK

    def buf_base(slot, kk):
        return (slot * TOP_K_IN_GROUP + kk) * (COMB_TOK * ROW_CHUNKS)

    def copy(step, slot, tl, kk):
        d = dest_ref[kk * ntok + step * COMB_TOK + tl]
        src = ys_ref.at[pl.ds(pl.multiple_of(d * ROW_CHUNKS, ROW_CHUNKS), ROW_CHUNKS)]
        dst = buf_ref.at[pl.ds(pl.multiple_of(buf_base(slot, kk) + tl * ROW_CHUNKS, ROW_CHUNKS), ROW_CHUNKS)]
        return pltpu.make_async_copy(src, dst, sem.at[slot])

    def start_all(step, slot):
        def body(tl, carry):
            for kk in range(TOP_K_IN_GROUP):
                copy(step, slot, tl, kk).start(priority=kk)
            return carry
        lax.fori_loop(0, COMB_TOK, body, 0, unroll=8)

    def wait_all(step, slot):
        def body(tl, carry):
            for kk in range(TOP_K_IN_GROUP):
                copy(step, slot, tl, kk).wait()
            return carry
        lax.fori_loop(0, COMB_TOK, body, 0, unroll=8)

    @pl.when(i == 0)
    def _():
        for a in range(COMB_AHEAD):
            start_all(a, a)

    @pl.when(i + COMB_AHEAD < n)
    def _():
        start_all(i + COMB_AHEAD, (i + COMB_AHEAD) % COMB_SLOTS)

    slot = i % COMB_SLOTS
    wait_all(i, slot)
    y0 = _load_token_major(buf_ref, buf_base(slot, 0), COMB_TOK)
    y1 = _load_token_major(buf_ref, buf_base(slot, 1), COMB_TOK)
    y = gw_ref[:, 0:1] * y0 + gw_ref[:, 1:2] * y1
    x = x1_ref[...] + y
    ms = jnp.mean(x * x, axis=-1, keepdims=True)
    o_ref[...] = x * lax.rsqrt(ms + EPS) * gf_ref[...]


def _combine(dest, ys, x1, gw_t, gf):
    t, d = x1.shape
    return pl.pallas_call(
        _combine_kernel,
        grid=(t // COMB_TOK,),
        in_specs=[
            pl.BlockSpec(memory_space=pltpu.SMEM),
            pl.BlockSpec(memory_space=pl.ANY),
            pl.BlockSpec((COMB_TOK, d), lambda i: (i, 0)),
            pl.BlockSpec((COMB_TOK, TOP_K_IN_GROUP), lambda i: (i, 0)),
            pl.BlockSpec((1, d), lambda i: (0, 0)),
        ],
        out_specs=pl.BlockSpec((COMB_TOK, d), lambda i: (i, 0)),
        out_shape=jax.ShapeDtypeStruct((t, d), F32),
        scratch_shapes=[
            pltpu.VMEM((COMB_SLOTS * TOP_K_IN_GROUP * COMB_TOK * ROW_CHUNKS, LANES), F32),
            pltpu.SemaphoreType.DMA((COMB_SLOTS,)),
        ],
        compiler_params=pltpu.CompilerParams(dimension_semantics=("arbitrary",)),
        name="combine",
    )(dest, ys, x1, gw_t, gf)


def kernel(x, norm1_g, w_in, conv_w, conv_b, lru_wa, lru_ba, lru_wx, lru_bx, lru_lambda, w_proj_attn,
           w_proj_rnn, w_out, rel_bias, norm2_g, w_group, b_group, w_expert_router, b_expert_router,
           w_gate, w_up, w_down, norm_f_g):
    batch, seq, d = x.shape
    tokens = batch * seq
    nblk = seq // QB
    rep = ATTN_HEADS // KV_HEADS
    x2 = x.reshape(tokens, d)

    w = w_in[0]
    wt_all = jnp.concatenate(
        [w[:, OFF_Q:OFF_K], w[:, OFF_QI:OFF_KI], w[:, OFF_V:OFF_QI],
         jnp.zeros((d, WT_ROWS - WT_V - KV_WIDTH), w.dtype)], axis=1).T.astype(BF16)
    wt_wi = jnp.pad(w[:, OFF_WI:OFF_XR].T, ((0, LANES - IDX_HEADS), (0, 0))).astype(BF16)
    w_kk = jnp.concatenate(
        [w[:, OFF_K:OFF_V], w[:, OFF_KI:OFF_WI],
         jnp.zeros((d, KK_COLS - KV_WIDTH - IDX_DIM), w.dtype)], axis=1).astype(BF16)
    w_c = w[:, OFF_XR:].astype(BF16)
    g1 = norm1_g[0].reshape(1, d)
    ot, wit, okk = _attn_proj(x2, g1, wt_all, wt_wi, w_kk, 1024, 512)
    pc = _norm_proj(x2, g1, w_c, F32, 1024, 1024)

    ks = np.arange(QB)[:, None]
    qs = np.arange(QB)[None, :]
    bkt = jnp.asarray(np.stack([_rel_bucket_np(qs - ks + QB * dd) for dd in range(2)]))
    attn = _dsa_attention(rel_bias, ot, wit, okk, bkt, batch, seq).reshape(tokens, ATTN_WIDTH)

    rnn = _rglru(pc, conv_w[0], conv_b[0].reshape(1, RNN_WIDTH), lru_wa[0].astype(BF16), lru_ba[0],
                 lru_wx[0].astype(BF16), lru_bx[0], lru_lambda[0].reshape(1, RNN_WIDTH), batch, seq)

    w_r = jnp.concatenate([w_group[0], w_expert_router[0],
                           jnp.zeros((d, LANES - N_GROUPS - N_EXPERTS), F32)], axis=1)
    w_rh = w_r.astype(BF16)
    w_rl = (w_r - w_rh.astype(F32)).astype(BF16)
    b_r = jnp.concatenate([b_group[0], b_expert_router[0],
                           jnp.zeros((LANES - N_GROUPS - N_EXPERTS,), F32)]).reshape(1, LANES)
    x1, u2, lgt = _merge(attn, rnn, pc, x2, w_proj_attn[0].astype(BF16), w_proj_rnn[0].astype(BF16),
                         w_out[0].astype(BF16), norm2_g[0].reshape(1, d), w_rh, w_rl, b_r)

    n_slots = tokens * TOP_K_IN_GROUP
    cap = -(-(n_slots + N_EXPERTS * (MOE_BLOCK - 1)) // MOE_BLOCK) * MOE_BLOCK
    dest, gw, meta = _route(lgt)

    dest_flat = dest.reshape(-1)
    ys = _experts(meta[0, :N_EXPERTS], meta[1, :N_EXPERTS], meta[2, :1], dest_flat, u2,
                  w_gate[0], w_up[0], w_down[0], cap)
    out = _combine(dest_flat, ys, x1, gw.T, norm_f_g.reshape(1, d))
    return out.reshape(batch, seq, d)
```

```python
import functools

import numpy as np
import jax
import jax.numpy as jnp
from jax import lax
from jax.experimental import pallas as pl
from jax.experimental.pallas import tpu as pltpu

D_MODEL = 2048
ATTN_HEADS = 8
KV_HEADS = 2
HEAD_DIM = 128
ATTN_WIDTH = ATTN_HEADS * HEAD_DIM
KV_WIDTH = KV_HEADS * HEAD_DIM
IDX_HEADS = 16
IDX_DIM = 64
TOPK_MAX = 256
RNN_WIDTH = 1024
RNN_BLOCKS = 8
RNN_BLOCK_W = RNN_WIDTH // RNN_BLOCKS
CONV_WIDTH = 4
LRU_C = 8.0
REL_BUCKETS = 32
REL_MAX_DIST = 128
N_GROUPS = 8
EXPERTS_PER_GROUP = 8
N_EXPERTS = N_GROUPS * EXPERTS_PER_GROUP
TOP_K_IN_GROUP = 2
EXPERT_FF = 512
MOE_BLOCK = 128
MOE_SHIFT = 7
EPS = 1e-6

LANES = 128
QB = 128
KT = 256
NEG_BIG = -1e30
INT_MIN = -(2 ** 31)
LOG2E = np.float32(np.log2(np.e))
F32 = jnp.float32
BF16 = jnp.bfloat16
I32 = jnp.int32

OFF_Q = 0
OFF_K = OFF_Q + ATTN_WIDTH
OFF_V = OFF_K + KV_WIDTH
OFF_QI = OFF_V + KV_WIDTH
OFF_KI = OFF_QI + IDX_HEADS * IDX_DIM
OFF_WI = OFF_KI + IDX_DIM
OFF_XR = OFF_WI + IDX_HEADS
OFF_RG = OFF_XR + RNN_WIDTH
OFF_GA = OFF_RG + RNN_WIDTH
OFF_GR = OFF_GA + D_MODEL
IN_COLS = OFF_GR + D_MODEL
PA_COLS = OFF_KI
PC_MAIN = IN_COLS - OFF_XR
PC_COLS = 6400


def _rel_bucket_np(n):
    n = np.maximum(n, 0)
    max_exact = REL_BUCKETS // 2
    nf = np.maximum(n, 1).astype(np.float32)
    large = max_exact + (np.log(nf / np.float32(max_exact)) / np.float32(np.log(REL_MAX_DIST / max_exact))
                         * np.float32(REL_BUCKETS - max_exact)).astype(np.int32)
    large = np.minimum(large, REL_BUCKETS - 1)
    return np.where(n < max_exact, n, large).astype(np.int32)


def _norm_proj_kernel(x_ref, g_ref, w_ref, o_ref, u_ref):
    @pl.when(pl.program_id(1) == 0)
    def _():
        x = x_ref[...]
        ms = jnp.mean(x * x, axis=-1, keepdims=True)
        u_ref[...] = (x * lax.rsqrt(ms + EPS) * g_ref[...]).astype(BF16)

    o_ref[...] = jnp.dot(u_ref[...], w_ref[...], preferred_element_type=F32).astype(o_ref.dtype)


def _norm_proj(x2, g, w, out_dtype, tm, tn):
    m, d = x2.shape
    n = w.shape[1]
    return pl.pallas_call(
        _norm_proj_kernel,
        grid=(m // tm, n // tn),
        in_specs=[
            pl.BlockSpec((tm, d), lambda i, j: (i, 0)),
            pl.BlockSpec((1, d), lambda i, j: (0, 0)),
            pl.BlockSpec((d, tn), lambda i, j: (0, j)),
        ],
        out_specs=pl.BlockSpec((tm, tn), lambda i, j: (i, j)),
        out_shape=jax.ShapeDtypeStruct((m, n), out_dtype),
        scratch_shapes=[pltpu.VMEM((tm, d), BF16)],
        compiler_params=pltpu.CompilerParams(dimension_semantics=("arbitrary", "arbitrary")),
        name="norm_proj",
    )(x2, g, w)


WT_ROWS = 2560
WT_Q, WT_QI, WT_V = 0, ATTN_WIDTH, ATTN_WIDTH + IDX_HEADS * IDX_DIM
KK_COLS = 384


def _attn_proj_kernel(x_ref, g_ref, wt_ref, wtwi_ref, wkk_ref, ot_ref, wit_ref, okk_ref, u_ref):
    nt = (((1,), (1,)), ((), ()))

    @pl.when(pl.program_id(1) == 0)
    def _():
        x = x_ref[...]
        ms = jnp.mean(x * x, axis=-1, keepdims=True)
        u = (x * lax.rsqrt(ms + EPS) * g_ref[...]).astype(BF16)
        u_ref[...] = u
        okk_ref[...] = jnp.dot(u, wkk_ref[...], preferred_element_type=F32).astype(okk_ref.dtype)
        wit_ref[...] = lax.dot_general(wtwi_ref[...], u, nt, preferred_element_type=F32)

    ot_ref[...] = lax.dot_general(wt_ref[...], u_ref[...], nt, preferred_element_type=F32).astype(ot_ref.dtype)


def _attn_proj(x2, g, wt_all, wt_wi, w_kk, tm, tr):
    m, d = x2.shape
    return pl.pallas_call(
        _attn_proj_kernel,
        grid=(m // tm, WT_ROWS // tr),
        in_specs=[
            pl.BlockSpec((tm, d), lambda i, j: (i, 0)),
            pl.BlockSpec((1, d), lambda i, j: (0, 0)),
            pl.BlockSpec((tr, d), lambda i, j: (j, 0)),
            pl.BlockSpec((LANES, d), lambda i, j: (0, 0)),
            pl.BlockSpec((d, KK_COLS), lambda i, j: (0, 0)),
        ],
        out_specs=[
            pl.BlockSpec((tr, tm), lambda i, j: (j, i)),
            pl.BlockSpec((LANES, tm), lambda i, j: (0, i)),
            pl.BlockSpec((tm, KK_COLS), lambda i, j: (i, 0)),
        ],
        out_shape=[
            jax.ShapeDtypeStruct((WT_ROWS, m), BF16),
            jax.ShapeDtypeStruct((LANES, m), F32),
            jax.ShapeDtypeStruct((m, KK_COLS), BF16),
        ],
        scratch_shapes=[pltpu.VMEM((tm, d), BF16)],
        compiler_params=pltpu.CompilerParams(dimension_semantics=("arbitrary", "arbitrary")),
        name="attn_proj",
    )(x2, g, wt_all, wt_wi, w_kk)


def _for_tiles_by_two(lo, hi, tile_fn):
    n = hi - lo

    def pair(i, carry):
        tile_fn(lo + 2 * i)
        tile_fn(lo + 2 * i + 1)
        return carry

    lax.fori_loop(0, n // 2, pair, 0)

    @pl.when(n % 2 == 1)
    def _():
        tile_fn(hi - 1)


def _dsa_kernel(relb_ref, q_ref, qi_ref, wi_ref, kk_ref, vt_ref, bkt_ref, o_ref,
                keys_ref, bias_ref, m_ref, l_ref, acc_ref, s_ref):
    b = pl.program_id(0)
    j = pl.program_id(1)

    @pl.when((b == 0) & (j == 0))
    def _():
        for d in range(2):
            bk = bkt_ref[d]
            for h in range(ATTN_HEADS):
                tile = jnp.zeros((QB, QB), F32)
                for bb in range(REL_BUCKETS):
                    tile = jnp.where(bk == bb, relb_ref[bb, h] * LOG2E, tile)
                bias_ref[d, h] = tile

    t0 = j * QB
    ntile = (j + 2) // 2
    q_pos = t0 + lax.broadcasted_iota(I32, (KT, QB), 1)
    row_iota = lax.broadcasted_iota(I32, (KT, QB), 0)

    w_scale = np.float32(IDX_HEADS ** -0.5) * np.float32(IDX_DIM ** -0.5)
    wrows = wi_ref[0:IDX_HEADS, :] * w_scale

    def score_tile(kt):
        r0 = pl.multiple_of(kt * KT, KT)
        ki_t = kk_ref[pl.ds(r0, KT), KV_WIDTH:KV_WIDTH + IDX_DIM]
        acc = jnp.zeros((KT, QB), F32)
        for c in range(4):
            qi_c = jnp.concatenate([qi_ref[(c * 4 + hh) * IDX_DIM:(c * 4 + hh + 1) * IDX_DIM, :]
                                    for hh in range(4)], axis=1)
            dots = jnp.dot(ki_t, qi_c, preferred_element_type=F32)
            for hh in range(4):
                h = c * 4 + hh
                acc = acc + jnp.maximum(dots[:, hh * QB:(hh + 1) * QB], 0.0) * wrows[h:h + 1, :]
        bits = pltpu.bitcast(acc, I32)
        key = bits ^ ((bits >> 31) & jnp.int32(0x7FFFFFFF))
        key = jnp.where(r0 + row_iota <= q_pos, key, jnp.int32(INT_MIN))
        keys_ref[pl.ds(r0, KT), :] = key

    _for_tiles_by_two(0, ntile, score_tile)

    n_acc = 4

    def count_ge(cand):
        cand8 = jnp.broadcast_to(cand, (8, QB))

        def body(kt, accs):
            r0 = pl.multiple_of(kt * KT, KT)
            accs = list(accs)
            for v in range(KT // 8):
                blk = keys_ref[pl.ds(r0 + v * 8, 8), :]
                accs[v % n_acc] = accs[v % n_acc] + jnp.where(blk >= cand8, jnp.int32(1), jnp.int32(0))
            return tuple(accs)

        accs = lax.fori_loop(0, ntile, body, (jnp.zeros((8, QB), I32),) * n_acc)
        acc8 = (accs[0] + accs[1]) + (accs[2] + accs[3])
        return jnp.sum(acc8.astype(F32), axis=0, keepdims=True).astype(I32)

    c0 = count_ge(jnp.zeros((1, QB), I32))
    ok0 = c0 >= TOPK_MAX
    thr0 = jnp.where(ok0, jnp.int32(0), jnp.int32(INT_MIN))
    cnt0 = jnp.where(ok0, c0, jnp.int32(0))

    def bit_step(i, carry):
        thr, cnt = carry
        cand = thr | (jnp.int32(1) << (30 - i))
        c = count_ge(cand)
        ok = c >= TOPK_MAX
        return jnp.where(ok, cand, thr), jnp.where(ok, c, cnt)

    thr, cnt = lax.fori_loop(0, 31, bit_step, (thr0, cnt0))
    thr = jnp.maximum(thr, jnp.int32(INT_MIN + 1))

    pos_bits = (keys_ref.shape[0] - 1).bit_length()

    @pl.when(jnp.max(cnt.astype(F32)) > TOPK_MAX)
    def _():
        tied = cnt > TOPK_MAX
        need = TOPK_MAX - count_ge(thr + 1)
        thr8 = jnp.broadcast_to(thr, (8, QB))
        sub8 = lax.broadcasted_iota(I32, (8, QB), 0)

        def count_eq_before(limit):
            lim8 = jnp.broadcast_to(limit, (8, QB))

            def body(v, acc):
                r = pl.multiple_of(v * 8, 8)
                hit = jnp.logical_and(keys_ref[pl.ds(r, 8), :] == thr8, r + sub8 < lim8)
                return acc + jnp.where(hit, jnp.int32(1), jnp.int32(0))

            acc = lax.fori_loop(0, ntile * (KT // 8), body, jnp.zeros((8, QB), I32))
            return jnp.sum(acc.astype(F32), axis=0, keepdims=True).astype(I32)

        def pos_step(i, last):
            cand = last | (jnp.int32(1) << (pos_bits - 1 - i))
            return jnp.where(count_eq_before(cand) < need, cand, last)

        last = lax.fori_loop(0, pos_bits, pos_step, jnp.zeros((1, QB), I32))
        last8 = jnp.broadcast_to(jnp.where(tied, last, jnp.int32(2 ** pos_bits)), (8, QB))

        def demote(v, carry):
            r = pl.multiple_of(v * 8, 8)
            blk = keys_ref[pl.ds(r, 8), :]
            drop = jnp.logical_and(blk == thr8, r + sub8 > last8)
            keys_ref[pl.ds(r, 8), :] = jnp.where(drop, thr8 - 1, blk)
            return carry

        lax.fori_loop(0, ntile * (KT // 8), demote, 0)

    qk_scale = np.float32(HEAD_DIM ** -0.5) * LOG2E
    rep = ATTN_HEADS // KV_HEADS
    nfar = jnp.maximum((j - 1) // 2, 0)

    gw = rep * QB
    m_ref[...] = jnp.full(m_ref.shape, NEG_BIG, F32)
    l_ref[...] = jnp.zeros(l_ref.shape, F32)
    acc_ref[...] = jnp.zeros(acc_ref.shape, F32)

    def logits(kt, g, near):
        r0 = pl.multiple_of(kt * KT, KT)
        k_t = kk_ref[pl.ds(r0, KT), g * HEAD_DIM:(g + 1) * HEAD_DIM]
        q_g = jnp.concatenate([q_ref[(g * rep + r) * HEAD_DIM:(g * rep + r + 1) * HEAD_DIM, :]
                               for r in range(rep)], axis=1)
        lg = jnp.dot(k_t, q_g, preferred_element_type=F32) * qk_scale
        sel = keys_ref[pl.ds(r0, KT), :] >= thr
        cols = []
        for r in range(rep):
            h = g * rep + r
            far_bias = relb_ref[REL_BUCKETS - 1, h] * LOG2E
            blk = lg[:, r * QB:(r + 1) * QB]
            if near:
                subs = []
                for sub in range(KT // QB):
                    dblk = j - (kt * (KT // QB) + sub)
                    bias = jnp.where(dblk == 0, bias_ref[0, h],
                                     jnp.where(dblk == 1, bias_ref[1, h], far_bias))
                    subs.append(blk[sub * QB:(sub + 1) * QB, :] + bias)
                blk = jnp.concatenate(subs, axis=0)
            else:
                blk = blk + far_bias
            cols.append(jnp.where(sel, blk, NEG_BIG))
        s = jnp.concatenate(cols, axis=1)
        s_ref[pl.ds(r0, KT), g * gw:(g + 1) * gw] = s
        m_ref[g] = jnp.maximum(m_ref[g], jnp.max(s.reshape(KT // 8, 8, gw), axis=0))

    def far_tile(kt):
        for g in range(KV_HEADS):
            logits(kt, g, False)

    def near_body(kt, carry):
        for g in range(KV_HEADS):
            logits(kt, g, True)
        return carry

    _for_tiles_by_two(0, nfar, far_tile)
    lax.fori_loop(nfar, ntile, near_body, 0)

    m_fin = [jnp.max(m_ref[g], axis=0, keepdims=True) for g in range(KV_HEADS)]

    def weigh(kt):
        r0 = pl.multiple_of(kt * KT, KT)
        for g in range(KV_HEADS):
            p = jnp.exp2(s_ref[pl.ds(r0, KT), g * gw:(g + 1) * gw] - m_fin[g])
            l_ref[g] = l_ref[g] + jnp.sum(p.reshape(KT // 8, 8, gw), axis=0)
            v_t = vt_ref[g * HEAD_DIM:(g + 1) * HEAD_DIM, pl.ds(r0, KT)]
            acc_ref[g] = acc_ref[g] + jnp.dot(v_t, p.astype(BF16), preferred_element_type=F32)

    _for_tiles_by_two(0, ntile, weigh)

    for g in range(KV_HEADS):
        out_t = acc_ref[g] / jnp.sum(l_ref[g], axis=0, keepdims=True)
        for r in range(rep):
            h = g * rep + r
            o_ref[:, h * HEAD_DIM:(h + 1) * HEAD_DIM] = out_t[:, r * QB:(r + 1) * QB].T.astype(o_ref.dtype)


def _dsa_attention(rel_bias, ot, wit, okk, bkt, batch, seq):
    nblk = seq // QB
    rep = ATTN_HEADS // KV_HEADS
    return pl.pallas_call(
        _dsa_kernel,
        grid=(batch, nblk),
        in_specs=[
            pl.BlockSpec(memory_space=pltpu.SMEM),
            pl.BlockSpec((ATTN_WIDTH, QB), lambda b, j: (WT_Q // ATTN_WIDTH, b * nblk + j)),
            pl.BlockSpec((IDX_HEADS * IDX_DIM, QB), lambda b, j: (WT_QI // (IDX_HEADS * IDX_DIM), b * nblk + j)),
            pl.BlockSpec((LANES, QB), lambda b, j: (0, b * nblk + j)),
            pl.BlockSpec((seq, KK_COLS), lambda b, j: (b, 0)),
            pl.BlockSpec((KV_WIDTH, seq), lambda b, j: (WT_V // KV_WIDTH, b)),
            pl.BlockSpec((2, QB, QB), lambda b, j: (0, 0, 0)),
        ],
        out_specs=pl.BlockSpec((None, QB, ATTN_WIDTH), lambda b, j: (b, j, 0)),
        out_shape=jax.ShapeDtypeStruct((batch, seq, ATTN_WIDTH), BF16),
        scratch_shapes=[
            pltpu.VMEM((seq, QB), I32),
            pltpu.VMEM((2, ATTN_HEADS, QB, QB), F32),
            pltpu.VMEM((KV_HEADS, 8, rep * QB), F32),
            pltpu.VMEM((KV_HEADS, 8, rep * QB), F32),
            pltpu.VMEM((KV_HEADS, HEAD_DIM, rep * QB), F32),
            pltpu.VMEM((seq, ATTN_HEADS * QB), F32),
        ],
        compiler_params=pltpu.CompilerParams(dimension_semantics=("arbitrary", "arbitrary")),
        name="dsa_attention",
    )(rel_bias, ot, ot, wit, okk, ot, bkt)


RNN_TS = 256


def _gelu_tanh(x):
    c = np.float32(np.sqrt(2.0 / np.pi))
    return x * (0.5 * (1.0 + jnp.tanh(c * (x + np.float32(0.044715) * (x * x * x)))))


def _softplus(z):
    return jnp.maximum(z, 0.0) + jnp.log1p(jnp.exp(-jnp.abs(z)))


def _rglru_kernel(xr_ref, gate_ref, cw_ref, cb_ref, wa_ref, ba_ref, wx_ref, bx_ref, lam_ref, o_ref,
                  xext_ref, h_ref):
    i = pl.program_id(1)
    ts = RNN_TS

    @pl.when(i == 0)
    def _():
        xext_ref[0:8, :] = jnp.zeros((8, RNN_WIDTH), F32)
        h_ref[...] = jnp.zeros((1, RNN_WIDTH), F32)

    @pl.when(i > 0)
    def _():
        xext_ref[0:8, :] = xext_ref[ts:ts + 8, :]

    xext_ref[8:8 + ts, :] = xr_ref[...]

    row = lax.broadcasted_iota(I32, (ts, RNN_BLOCK_W), 0)
    for n in range(RNN_BLOCKS):
        cs = slice(n * RNN_BLOCK_W, (n + 1) * RNN_BLOCK_W)
        xc = cb_ref[:, cs]
        for jj in range(CONV_WIDTH):
            off = 8 - (CONV_WIDTH - 1) + jj
            xc = xc + xext_ref[off:off + ts, cs] * cw_ref[jj:jj + 1, cs]
        xcb = xc.astype(BF16)
        r = jax.nn.sigmoid(jnp.dot(xcb, wa_ref[n], preferred_element_type=F32) + ba_ref[n:n + 1, :])
        gi = jax.nn.sigmoid(jnp.dot(xcb, wx_ref[n], preferred_element_type=F32) + bx_ref[n:n + 1, :])
        log_a = (-LRU_C * r) * _softplus(-lam_ref[:, cs])
        a = jnp.exp(log_a)
        mult = jnp.sqrt(1.0 - jnp.exp(2.0 * log_a))
        bt = mult * (gi * xc)
        k = 1
        while k < ts:
            a_s = pltpu.roll(a, k, 0)
            b_s = pltpu.roll(bt, k, 0)
            keep = row >= k
            bt = jnp.where(keep, a * b_s + bt, bt)
            a = jnp.where(keep, a * a_s, a)
            k *= 2
        h = bt + a * h_ref[:, cs]
        h_ref[:, cs] = h[ts - 1:ts, :]
        o_ref[:, cs] = (h * _gelu_tanh(gate_ref[:, cs])).astype(o_ref.dtype)


def _rglru(pc, conv_w, conv_b, wa, ba, wx, bx, lam, batch, seq):
    nts = seq // RNN_TS
    full = lambda shape: pl.BlockSpec(shape, lambda b, i: (0,) * len(shape))
    return pl.pallas_call(
        _rglru_kernel,
        grid=(batch, nts),
        in_specs=[
            pl.BlockSpec((RNN_TS, RNN_WIDTH), lambda b, i: (b * nts + i, 0)),
            pl.BlockSpec((RNN_TS, RNN_WIDTH), lambda b, i: (b * nts + i, 1)),
            full((CONV_WIDTH, RNN_WIDTH)),
            full((1, RNN_WIDTH)),
            full((RNN_BLOCKS, RNN_BLOCK_W, RNN_BLOCK_W)),
            full((RNN_BLOCKS, RNN_BLOCK_W)),
            full((RNN_BLOCKS, RNN_BLOCK_W, RNN_BLOCK_W)),
            full((RNN_BLOCKS, RNN_BLOCK_W)),
            full((1, RNN_WIDTH)),
        ],
        out_specs=pl.BlockSpec((RNN_TS, RNN_WIDTH), lambda b, i: (b * nts + i, 0)),
        out_shape=jax.ShapeDtypeStruct((batch * seq, RNN_WIDTH), BF16),
        scratch_shapes=[pltpu.VMEM((RNN_TS + 8, RNN_WIDTH), F32), pltpu.VMEM((1, RNN_WIDTH), F32)],
        compiler_params=pltpu.CompilerParams(dimension_semantics=("arbitrary", "arbitrary")),
        name="rglru",
    )(pc, pc, conv_w, conv_b, wa, ba, wx, bx, lam)


MERGE_TM = 256
ROW_CHUNKS = D_MODEL // LANES


def _store_token_major(ref, base, val):
    n = val.shape[0]
    for c in range(ROW_CHUNKS):
        ref[pl.ds(base + c, n, stride=ROW_CHUNKS), :] = val[:, c * LANES:(c + 1) * LANES]


def _load_token_major(ref, base, n):
    return jnp.concatenate([ref[pl.ds(base + c, n, stride=ROW_CHUNKS), :] for c in range(ROW_CHUNKS)], axis=1)


def _merge_kernel(attn_ref, rnn_ref, ga_ref, gr_ref, x_ref, wpa_ref, wpr_ref, wo_ref, g2_ref,
                  wrh_ref, wrl_ref, br_ref, x1_ref, u2_ref, lgt_ref):
    pa = jnp.dot(attn_ref[...], wpa_ref[...], preferred_element_type=F32)
    pr = jnp.dot(rnn_ref[...], wpr_ref[...], preferred_element_type=F32)
    merged = jax.nn.sigmoid(ga_ref[...]) * pa + jax.nn.sigmoid(gr_ref[...]) * pr
    x1 = x_ref[...] + jnp.dot(merged.astype(BF16), wo_ref[...], preferred_element_type=F32)
    x1_ref[...] = x1
    ms = jnp.mean(x1 * x1, axis=-1, keepdims=True)
    u2 = x1 * lax.rsqrt(ms + EPS) * g2_ref[...]
    _store_token_major(u2_ref, 0, u2)
    hi = u2.astype(BF16)
    lo = (u2 - hi.astype(F32)).astype(BF16)
    lg = (jnp.dot(hi, wrh_ref[...], preferred_element_type=F32)
          + jnp.dot(lo, wrh_ref[...], preferred_element_type=F32)
          + jnp.dot(hi, wrl_ref[...], preferred_element_type=F32)) + br_ref[...]
    lgt_ref[...] = lg.T


def _merge(attn, rnn, pc, x2, wpa, wpr, wo, g2, wrh, wrl, br):
    m = x2.shape[0]
    tm = MERGE_TM
    const = lambda shape: pl.BlockSpec(shape, lambda i: (0,) * len(shape), pipeline_mode=pl.Buffered(1))
    return pl.pallas_call(
        _merge_kernel,
        grid=(m // tm,),
        in_specs=[
            pl.BlockSpec((tm, ATTN_WIDTH), lambda i: (i, 0)),
            pl.BlockSpec((tm, RNN_WIDTH), lambda i: (i, 0)),
            pl.BlockSpec((tm, D_MODEL), lambda i: (i, 1)),
            pl.BlockSpec((tm, D_MODEL), lambda i: (i, 2)),
            pl.BlockSpec((tm, D_MODEL), lambda i: (i, 0)),
            const((ATTN_WIDTH, D_MODEL)),
            const((RNN_WIDTH, D_MODEL)),
            const((D_MODEL, D_MODEL)),
            const((1, D_MODEL)),
            const((D_MODEL, LANES)),
            const((D_MODEL, LANES)),
            const((1, LANES)),
        ],
        out_specs=[
            pl.BlockSpec((tm, D_MODEL), lambda i: (i, 0)),
            pl.BlockSpec((tm * ROW_CHUNKS, LANES), lambda i: (i, 0)),
            pl.BlockSpec((LANES, tm), lambda i: (0, i)),
        ],
        out_shape=[
            jax.ShapeDtypeStruct((m, D_MODEL), F32),
            jax.ShapeDtypeStruct((m * ROW_CHUNKS, LANES), F32),
            jax.ShapeDtypeStruct((LANES, m), F32),
        ],
        compiler_params=pltpu.CompilerParams(dimension_semantics=("arbitrary",)),
        name="merge_outproj",
    )(attn, rnn, pc, pc, x2, wpa, wpr, wo, g2, wrh, wrl, br)


ROUTE_CHUNK = 256


def _first_index_of_max(v, ridx, n):
    vmax = jnp.max(v, axis=0, keepdims=True)
    idx = jnp.min(jnp.where(v == vmax, ridx, jnp.int32(n)).astype(F32), axis=0, keepdims=True)
    return vmax, idx.astype(I32)


def _route_kernel(lgt_ref, dest_ref, gw_ref, meta_ref, cum_ref):
    t = lgt_ref.shape[1]
    eg = EXPERTS_PER_GROUP
    ridx8 = lax.broadcasted_iota(I32, (eg, t), 0)
    gl = lgt_ref[0:N_GROUPS, :]
    gmax, g_sel = _first_index_of_max(gl, ridx8, N_GROUPS)
    p_sel = 1.0 / jnp.sum(jnp.exp(gl - gmax), axis=0, keepdims=True)
    el = lgt_ref[N_GROUPS:N_GROUPS + eg, :]
    for g in range(1, N_GROUPS):
        el = jnp.where(g_sel == g, lgt_ref[N_GROUPS + g * eg:N_GROUPS + (g + 1) * eg, :], el)
    v0, i0 = _first_index_of_max(el, ridx8, eg)
    el1 = jnp.where(ridx8 == i0, -jnp.inf, el)
    v1, i1 = _first_index_of_max(el1, ridx8, eg)
    e1 = jnp.exp(v1 - v0)
    den = 1.0 + e1
    gw_ref[0:1, :] = p_sel * (1.0 / den)
    gw_ref[1:2, :] = p_sel * (e1 / den)
    ex0 = g_sel * eg + i0
    ex1 = g_sel * eg + i1

    eidx = lax.broadcasted_iota(I32, (N_EXPERTS, ROUTE_CHUNK), 0)
    ui = lax.broadcasted_iota(I32, (ROUTE_CHUNK, ROUTE_CHUNK), 0)
    uj = lax.broadcasted_iota(I32, (ROUTE_CHUNK, ROUTE_CHUNK), 1)
    upper = jnp.where(ui < uj, 1.0, 0.0).astype(BF16)
    run = jnp.zeros((N_EXPERTS, 1), F32)
    for c in range(t // ROUTE_CHUNK):
        cs = slice(c * ROUTE_CHUNK, (c + 1) * ROUTE_CHUNK)
        hit = jnp.where(eidx == ex0[:, cs], 1.0, jnp.where(eidx == ex1[:, cs], 1.0, 0.0))
        cum_ref[:, cs] = jnp.dot(hit.astype(BF16), upper, preferred_element_type=F32) + run
        run = run + jnp.sum(hit, axis=1, keepdims=True)

    counts = run.astype(I32)
    padded = ((counts + (MOE_BLOCK - 1)) >> MOE_SHIFT) << MOE_SHIFT
    pe = jnp.broadcast_to(padded, (N_EXPERTS, LANES))
    erow = lax.broadcasted_iota(I32, (N_EXPERTS, LANES), 0)
    k = 1
    while k < N_EXPERTS:
        pe = pe + jnp.where(erow >= k, pltpu.roll(pe, k, 0), 0)
        k *= 2
    pends = pe[:, 0:1]
    pstarts = pends - padded

    eidx_t = lax.broadcasted_iota(I32, (N_EXPERTS, t), 0)
    slot = cum_ref[...] + pstarts.astype(F32)
    dest_ref[0:1, :] = jnp.sum(jnp.where(eidx_t == ex0, slot, 0.0), axis=0, keepdims=True).astype(I32)
    dest_ref[1:2, :] = jnp.sum(jnp.where(eidx_t == ex1, slot, 0.0), axis=0, keepdims=True).astype(I32)

    nb = meta_ref.shape[1]
    own = lax.broadcasted_iota(I32, (N_EXPERTS, nb), 0) == lax.broadcasted_iota(I32, (N_EXPERTS, nb), 1)
    first_blk = (pstarts >> MOE_SHIFT).astype(F32)
    n_blk = (padded >> MOE_SHIFT).astype(F32)
    meta_ref[0:1, :] = jnp.sum(jnp.where(own, first_blk, 0.0), axis=0, keepdims=True).astype(I32)
    meta_ref[1:2, :] = jnp.sum(jnp.where(own, n_blk, 0.0), axis=0, keepdims=True).astype(I32)
    meta_ref[2:3, :] = jnp.broadcast_to(pends[N_EXPERTS - 1:N_EXPERTS, :] >> MOE_SHIFT, (1, nb))


def _route(lgt):
    t = lgt.shape[1]
    return pl.pallas_call(
        _route_kernel,
        out_shape=[
            jax.ShapeDtypeStruct((2, t), I32),
            jax.ShapeDtypeStruct((2, t), F32),
            jax.ShapeDtypeStruct((3, LANES), I32),
        ],
        scratch_shapes=[pltpu.VMEM((N_EXPERTS, t), F32)],
        name="route",
    )(lgt)


W_CHUNKS = 8
GATHER_AHEAD = 3
X_SLOTS = GATHER_AHEAD + 1
W_AHEAD = 2
W_SLOTS = W_AHEAD + 1


def _expert_kernel(first_ref, nblk_ref, nact_ref, dest_ref, u_ref, wg_ref, wu_ref, wd_ref, ys_ref,
                   tok_ref, xbuf_ref, ybuf_ref, wgf_ref, wuf_ref, wdf_ref, wgb_ref, wub_ref, wdb_ref,
                   gsem, osem, wsem):
    e = pl.program_id(0)
    first = first_ref[e]
    nb = nblk_ref[e]
    nact = nact_ref[0]
    ntok = u_ref.shape[0] // ROW_CHUNKS
    nslot = tok_ref.shape[0]
    blk_rows = MOE_BLOCK * ROW_CHUNKS

    def row_copy(blk, slot, r):
        tok = tok_ref[blk * MOE_BLOCK + r]
        src = u_ref.at[pl.ds(pl.multiple_of(tok * ROW_CHUNKS, ROW_CHUNKS), ROW_CHUNKS)]
        dst = xbuf_ref.at[pl.ds(pl.multiple_of(slot * blk_rows + r * ROW_CHUNKS, ROW_CHUNKS), ROW_CHUNKS)]
        return pltpu.make_async_copy(src, dst, gsem.at[slot])

    def start_rows(blk, slot):
        def body(r, carry):
            row_copy(blk, slot, r).start()
            return carry
        lax.fori_loop(0, MOE_BLOCK, body, 0, unroll=8)

    def wait_rows(blk, slot):
        def body(r, carry):
            row_copy(blk, slot, r).wait()
            return carry
        lax.fori_loop(0, MOE_BLOCK, body, 0, unroll=8)

    def out_copy(blk, slot):
        src = ybuf_ref.at[pl.ds(pl.multiple_of(slot * blk_rows, blk_rows), blk_rows)]
        dst = ys_ref.at[pl.ds(pl.multiple_of(blk * blk_rows, blk_rows), blk_rows)]
        return pltpu.make_async_copy(src, dst, osem.at[slot])

    def build_slot_table():
        def clear(p, carry):
            tok_ref[p] = 0
            return carry
        lax.fori_loop(0, nslot, clear, 0, unroll=8)

        def put(t, carry):
            for kk in range(TOP_K_IN_GROUP):
                tok_ref[dest_ref[kk * ntok + t]] = t
            return carry
        lax.fori_loop(0, ntok, put, 0, unroll=8)

    def weight_copies(ex, wslot):
        copies = []
        for src, dst in ((wg_ref, wgf_ref), (wu_ref, wuf_ref), (wd_ref, wdf_ref)):
            rows = src.shape[1] // W_CHUNKS
            for c in range(W_CHUNKS):
                rs = pl.ds(c * rows, rows)
                copies.append(pltpu.make_async_copy(src.at[ex, rs], dst.at[wslot, rs], wsem.at[wslot]))
        return copies

    def start_weights(ex, wslot):
        for cp in weight_copies(ex, wslot):
            cp.start(priority=1)

    @pl.when(e == 0)
    def _():
        for a in range(W_AHEAD):
            start_weights(a, a)

    @pl.when(e + W_AHEAD < pl.num_programs(0))
    def _():
        start_weights(e + W_AHEAD, (e + W_AHEAD) % W_SLOTS)

    @pl.when(e == 0)
    def _():
        build_slot_table()
        for a in range(GATHER_AHEAD):
            @pl.when(a < nact)
            def _():
                start_rows(a, a)

    wslot = e % W_SLOTS
    for cp in weight_copies(e, wslot):
        cp.wait()
    wgb_ref[...] = wgf_ref[wslot].astype(BF16)
    wub_ref[...] = wuf_ref[wslot].astype(BF16)
    wdb_ref[...] = wdf_ref[wslot].astype(BF16)

    def block(blk, carry):
        xslot = blk % X_SLOTS
        yslot = blk % 2

        @pl.when(blk + GATHER_AHEAD < nact)
        def _():
            start_rows(blk + GATHER_AHEAD, (blk + GATHER_AHEAD) % X_SLOTS)

        wait_rows(blk, xslot)
        xb = _load_token_major(xbuf_ref, xslot * blk_rows, MOE_BLOCK).astype(BF16)
        hg = jnp.dot(xb, wgb_ref[...], preferred_element_type=F32)
        hu = jnp.dot(xb, wub_ref[...], preferred_element_type=F32)
        h = (hg * jax.nn.sigmoid(hg)) * hu
        y = jnp.dot(h.astype(BF16), wdb_ref[...], preferred_element_type=F32)

        @pl.when(blk >= 2)
        def _():
            out_copy(blk - 2, yslot).wait()

        _store_token_major(ybuf_ref, yslot * blk_rows, y)
        out_copy(blk, yslot).start()
        return carry

    lax.fori_loop(first, first + nb, block, 0)

    @pl.when(e == pl.num_programs(0) - 1)
    def _():
        @pl.when(nact >= 2)
        def _():
            out_copy(nact - 2, nact % 2).wait()

        @pl.when(nact >= 1)
        def _():
            out_copy(nact - 1, (nact - 1) % 2).wait()

        ntotal = ys_ref.shape[0] // blk_rows
        ybuf_ref[0:blk_rows, :] = jnp.zeros((blk_rows, LANES), ybuf_ref.dtype)

        def fill_start(blk, carry):
            out_copy(blk, 0).start()
            return carry

        def fill_wait(blk, carry):
            out_copy(blk, 0).wait()
            return carry

        lax.fori_loop(nact_ref[0], ntotal, fill_start, 0)
        lax.fori_loop(nact_ref[0], ntotal, fill_wait, 0)


def _experts(first_blk, n_blk, nact, dest_flat, u2, w_gate, w_up, w_down, cap):
    d = D_MODEL
    grid_spec = pltpu.PrefetchScalarGridSpec(
        num_scalar_prefetch=4,
        grid=(N_EXPERTS,),
        in_specs=[pl.BlockSpec(memory_space=pl.ANY)] * 4,
        out_specs=pl.BlockSpec(memory_space=pl.ANY),
        scratch_shapes=[
            pltpu.SMEM((cap,), I32),
            pltpu.VMEM((X_SLOTS * MOE_BLOCK * ROW_CHUNKS, LANES), F32),
            pltpu.VMEM((2 * MOE_BLOCK * ROW_CHUNKS, LANES), F32),
            pltpu.VMEM((W_SLOTS, d, EXPERT_FF), F32),
            pltpu.VMEM((W_SLOTS, d, EXPERT_FF), F32),
            pltpu.VMEM((W_SLOTS, EXPERT_FF, d), F32),
            pltpu.VMEM((d, EXPERT_FF), BF16),
            pltpu.VMEM((d, EXPERT_FF), BF16),
            pltpu.VMEM((EXPERT_FF, d), BF16),
            pltpu.SemaphoreType.DMA((X_SLOTS,)),
            pltpu.SemaphoreType.DMA((2,)),
            pltpu.SemaphoreType.DMA((W_SLOTS,)),
        ],
    )
    return pl.pallas_call(
        _expert_kernel,
        grid_spec=grid_spec,
        out_shape=jax.ShapeDtypeStruct((cap * ROW_CHUNKS, LANES), F32),
        compiler_params=pltpu.CompilerParams(dimension_semantics=("arbitrary",), has_side_effects=True),
        name="experts",
    )(first_blk, n_blk, nact, dest_flat, u2, w_gate, w_up, w_down)


COMB_TOK = 128
COMB_AHEAD = 3
COMB_SLOTS = COMB_AHEAD + 1


def _combine_kernel(dest_ref, ys_ref, x1_ref, gw_ref, gf_ref, o_ref, buf_ref, sem):
    i = pl.program_id(0)
    n = pl.num_programs(0)
    ntok = n * COMB_TOK

    def buf_base(slot, kk):
        return (slot * TOP_K_IN_GROUP + kk) * (COMB_TOK * ROW_CHUNKS)

    def copy(step, slot, tl, kk):
        d = dest_ref[kk * ntok + step * COMB_TOK + tl]
        src = ys_ref.at[pl.ds(pl.multiple_of(d * ROW_CHUNKS, ROW_CHUNKS), ROW_CHUNKS)]
        dst = buf_ref.at[pl.ds(pl.multiple_of(buf_base(slot, kk) + tl * ROW_CHUNKS, ROW_CHUNKS), ROW_CHUNKS)]
        return pltpu.make_async_copy(src, dst, sem.at[slot])

    def start_all(step, slot):
        def body(tl, carry):
            for kk in range(TOP_K_IN_GROUP):
                copy(step, slot, tl, kk).start(priority=kk)
            return carry
        lax.fori_loop(0, COMB_TOK, body, 0, unroll=8)

    def wait_all(step, slot):
        def body(tl, carry):
            for kk in range(TOP_K_IN_GROUP):
                copy(step, slot, tl, kk).wait()
            return carry
        lax.fori_loop(0, COMB_TOK, body, 0, unroll=8)

    @pl.when(i == 0)
    def _():
        for a in range(COMB_AHEAD):
            start_all(a, a)

    @pl.when(i + COMB_AHEAD < n)
    def _():
        start_all(i + COMB_AHEAD, (i + COMB_AHEAD) % COMB_SLOTS)

    slot = i % COMB_SLOTS
    wait_all(i, slot)
    y0 = _load_token_major(buf_ref, buf_base(slot, 0), COMB_TOK)
    y1 = _load_token_major(buf_ref, buf_base(slot, 1), COMB_TOK)
    y = gw_ref[:, 0:1] * y0 + gw_ref[:, 1:2] * y1
    x = x1_ref[...] + y
    ms = jnp.mean(x * x, axis=-1, keepdims=True)
    o_ref[...] = x * lax.rsqrt(ms + EPS) * gf_ref[...]


def _combine(dest, ys, x1, gw_t, gf):
    t, d = x1.shape
    return pl.pallas_call(
        _combine_kernel,
        grid=(t // COMB_TOK,),
        in_specs=[
            pl.BlockSpec(memory_space=pltpu.SMEM),
            pl.BlockSpec(memory_space=pl.ANY),
            pl.BlockSpec((COMB_TOK, d), lambda i: (i, 0)),
            pl.BlockSpec((COMB_TOK, TOP_K_IN_GROUP), lambda i: (i, 0)),
            pl.BlockSpec((1, d), lambda i: (0, 0)),
        ],
        out_specs=pl.BlockSpec((COMB_TOK, d), lambda i: (i, 0)),
        out_shape=jax.ShapeDtypeStruct((t, d), F32),
        scratch_shapes=[
            pltpu.VMEM((COMB_SLOTS * TOP_K_IN_GROUP * COMB_TOK * ROW_CHUNKS, LANES), F32),
            pltpu.SemaphoreType.DMA((COMB_SLOTS,)),
        ],
        compiler_params=pltpu.CompilerParams(dimension_semantics=("arbitrary",)),
        name="combine",
    )(dest, ys, x1, gw_t, gf)


def kernel(x, norm1_g, w_in, conv_w, conv_b, lru_wa, lru_ba, lru_wx, lru_bx, lru_lambda, w_proj_attn,
           w_proj_rnn, w_out, rel_bias, norm2_g, w_group, b_group, w_expert_router, b_expert_router,
           w_gate, w_up, w_down, norm_f_g):
    batch, seq, d = x.shape
    tokens = batch * seq
    nblk = seq // QB
    rep = ATTN_HEADS // KV_HEADS
    x2 = x.reshape(tokens, d)

    w = w_in[0]
    wt_all = jnp.concatenate(
        [w[:, OFF_Q:OFF_K], w[:, OFF_QI:OFF_KI], w[:, OFF_V:OFF_QI],
         jnp.zeros((d, WT_ROWS - WT_V - KV_WIDTH), w.dtype)], axis=1).T.astype(BF16)
    wt_wi = jnp.pad(w[:, OFF_WI:OFF_XR].T, ((0, LANES - IDX_HEADS), (0, 0))).astype(BF16)
    w_kk = jnp.concatenate(
        [w[:, OFF_K:OFF_V], w[:, OFF_KI:OFF_WI],
         jnp.zeros((d, KK_COLS - KV_WIDTH - IDX_DIM), w.dtype)], axis=1).astype(BF16)
    w_c = w[:, OFF_XR:].astype(BF16)
    g1 = norm1_g[0].reshape(1, d)
    ot, wit, okk = _attn_proj(x2, g1, wt_all, wt_wi, w_kk, 1024, 1280)
    pc = _norm_proj(x2, g1, w_c, F32, 1024, 1536)

    ks = np.arange(QB)[:, None]
    qs = np.arange(QB)[None, :]
    bkt = jnp.asarray(np.stack([_rel_bucket_np(qs - ks + QB * dd) for dd in range(2)]))
    attn = _dsa_attention(rel_bias, ot, wit, okk, bkt, batch, seq).reshape(tokens, ATTN_WIDTH)

    rnn = _rglru(pc, conv_w[0], conv_b[0].reshape(1, RNN_WIDTH), lru_wa[0].astype(BF16), lru_ba[0],
                 lru_wx[0].astype(BF16), lru_bx[0], lru_lambda[0].reshape(1, RNN_WIDTH), batch, seq)

    w_r = jnp.concatenate([w_group[0], w_expert_router[0],
                           jnp.zeros((d, LANES - N_GROUPS - N_EXPERTS), F32)], axis=1)
    w_rh = w_r.astype(BF16)
    w_rl = (w_r - w_rh.astype(F32)).astype(BF16)
    b_r = jnp.concatenate([b_group[0], b_expert_router[0],
                           jnp.zeros((LANES - N_GROUPS - N_EXPERTS,), F32)]).reshape(1, LANES)
    x1, u2, lgt = _merge(attn, rnn, pc, x2, w_proj_attn[0].astype(BF16), w_proj_rnn[0].astype(BF16),
                         w_out[0].astype(BF16), norm2_g[0].reshape(1, d), w_rh, w_rl, b_r)

    n_slots = tokens * TOP_K_IN_GROUP
    cap = -(-(n_slots + N_EXPERTS * (MOE_BLOCK - 1)) // MOE_BLOCK) * MOE_BLOCK
    dest, gw, meta = _route(lgt)

    dest_flat = dest.reshape(-1)
    ys = _experts(meta[0, :N_EXPERTS], meta[1, :N_EXPERTS], meta[2, :1], dest_flat, u2,
                  w_gate[0], w_up[0], w_down[0], cap)
    out = _combine(dest_flat, ys, x1, gw.T, norm_f_g.reshape(1, d))
    return out.reshape(batch, seq, d)
```

```python
import numpy as np
import jax
import jax.numpy as jnp
from jax import lax
from jax.experimental import pallas as pl
from jax.experimental.pallas import tpu as pltpu

D_MODEL = 2048
ATTN_HEADS = 8
KV_HEADS = 2
HEAD_DIM = 128
ATTN_WIDTH = ATTN_HEADS * HEAD_DIM
KV_WIDTH = KV_HEADS * HEAD_DIM
IDX_HEADS = 16
IDX_DIM = 64
TOPK_MAX = 256
RNN_WIDTH = 1024
RNN_BLOCKS = 8
RNN_BLOCK_W = RNN_WIDTH // RNN_BLOCKS
CONV_WIDTH = 4
LRU_C = 8.0
REL_BUCKETS = 32
REL_MAX_DIST = 128
N_GROUPS = 8
EXPERTS_PER_GROUP = 8
N_EXPERTS = N_GROUPS * EXPERTS_PER_GROUP
TOP_K_IN_GROUP = 2
EXPERT_FF = 512
MOE_BLOCK = 128
MOE_SHIFT = 7
EPS = 1e-6

LANES = 128
QB = 128
KT = 256
NEG_BIG = -1e30
INT_MIN = -(2 ** 31)
LOG2E = np.float32(np.log2(np.e))
F32 = jnp.float32
BF16 = jnp.bfloat16
I32 = jnp.int32

OFF_Q = 0
OFF_K = OFF_Q + ATTN_WIDTH
OFF_V = OFF_K + KV_WIDTH
OFF_QI = OFF_V + KV_WIDTH
OFF_KI = OFF_QI + IDX_HEADS * IDX_DIM
OFF_WI = OFF_KI + IDX_DIM
OFF_XR = OFF_WI + IDX_HEADS


def _rel_bucket_np(n):
    n = np.maximum(n, 0)
    max_exact = REL_BUCKETS // 2
    nf = np.maximum(n, 1).astype(np.float32)
    large = max_exact + (np.log(nf / np.float32(max_exact)) / np.float32(np.log(REL_MAX_DIST / max_exact))
                         * np.float32(REL_BUCKETS - max_exact)).astype(np.int32)
    large = np.minimum(large, REL_BUCKETS - 1)
    return np.where(n < max_exact, n, large).astype(np.int32)


def _norm_proj_kernel(x_ref, g_ref, w_ref, o_ref, u_ref):
    @pl.when(pl.program_id(1) == 0)
    def _():
        x = x_ref[...]
        ms = jnp.mean(x * x, axis=-1, keepdims=True)
        u_ref[...] = (x * lax.rsqrt(ms + EPS) * g_ref[...]).astype(BF16)

    o_ref[...] = jnp.dot(u_ref[...], w_ref[...], preferred_element_type=F32).astype(o_ref.dtype)


def _norm_proj(x2, g, w, out_dtype, tm, tn):
    m, d = x2.shape
    n = w.shape[1]
    return pl.pallas_call(
        _norm_proj_kernel,
        grid=(m // tm, n // tn),
        in_specs=[
            pl.BlockSpec((tm, d), lambda i, j: (i, 0)),
            pl.BlockSpec((1, d), lambda i, j: (0, 0)),
            pl.BlockSpec((d, tn), lambda i, j: (0, j)),
        ],
        out_specs=pl.BlockSpec((tm, tn), lambda i, j: (i, j)),
        out_shape=jax.ShapeDtypeStruct((m, n), out_dtype),
        scratch_shapes=[pltpu.VMEM((tm, d), BF16)],
        compiler_params=pltpu.CompilerParams(dimension_semantics=("arbitrary", "arbitrary")),
        name="norm_proj",
    )(x2, g, w)


WT_ROWS = 2560
WT_Q, WT_QI, WT_V = 0, ATTN_WIDTH, ATTN_WIDTH + IDX_HEADS * IDX_DIM
KK_COLS = 384


def _attn_proj_kernel(x_ref, g_ref, wt_ref, wtwi_ref, wkk_ref, ot_ref, wit_ref, okk_ref, u_ref):
    nt = (((1,), (1,)), ((), ()))

    @pl.when(pl.program_id(1) == 0)
    def _():
        x = x_ref[...]
        ms = jnp.mean(x * x, axis=-1, keepdims=True)
        u = (x * lax.rsqrt(ms + EPS) * g_ref[...]).astype(BF16)
        u_ref[...] = u
        okk_ref[...] = jnp.dot(u, wkk_ref[...], preferred_element_type=F32).astype(okk_ref.dtype)
        wit_ref[...] = lax.dot_general(wtwi_ref[...], u, nt, preferred_element_type=F32)

    ot_ref[...] = lax.dot_general(wt_ref[...], u_ref[...], nt, preferred_element_type=F32).astype(ot_ref.dtype)


def _attn_proj(x2, g, wt_all, wt_wi, w_kk, tm, tr):
    m, d = x2.shape
    return pl.pallas_call(
        _attn_proj_kernel,
        grid=(m // tm, WT_ROWS // tr),
        in_specs=[
            pl.BlockSpec((tm, d), lambda i, j: (i, 0)),
            pl.BlockSpec((1, d), lambda i, j: (0, 0)),
            pl.BlockSpec((tr, d), lambda i, j: (j, 0)),
            pl.BlockSpec((LANES, d), lambda i, j: (0, 0)),
            pl.BlockSpec((d, KK_COLS), lambda i, j: (0, 0)),
        ],
        out_specs=[
            pl.BlockSpec((tr, tm), lambda i, j: (j, i)),
            pl.BlockSpec((LANES, tm), lambda i, j: (0, i)),
            pl.BlockSpec((tm, KK_COLS), lambda i, j: (i, 0)),
        ],
        out_shape=[
            jax.ShapeDtypeStruct((WT_ROWS, m), BF16),
            jax.ShapeDtypeStruct((LANES, m), F32),
            jax.ShapeDtypeStruct((m, KK_COLS), BF16),
        ],
        scratch_shapes=[pltpu.VMEM((tm, d), BF16)],
        compiler_params=pltpu.CompilerParams(dimension_semantics=("arbitrary", "arbitrary")),
        name="attn_proj",
    )(x2, g, wt_all, wt_wi, w_kk)


TILES_PER_TRIP = 4


def _for_tiles_grouped(lo, hi, tile_fn):
    n = hi - lo
    full = n // TILES_PER_TRIP

    def group(i, carry):
        for u in range(TILES_PER_TRIP):
            tile_fn(lo + TILES_PER_TRIP * i + u)
        return carry

    lax.fori_loop(0, full, group, 0)

    done = lo + full * TILES_PER_TRIP
    size = TILES_PER_TRIP // 2
    while size >= 1:
        take = (n & size) != 0

        @pl.when(take)
        def _(done=done, size=size):
            for u in range(size):
                tile_fn(done + u)

        done = done + jnp.where(take, size, 0)
        size //= 2


def _dsa_kernel(relb_ref, q_ref, qi_ref, wi_ref, kk_ref, vt_ref, bkt_ref, o_ref,
                keys_ref, bias_ref, m_ref, l_ref, acc_ref, s_ref):
    b = pl.program_id(0)
    j = pl.program_id(1)

    @pl.when((b == 0) & (j == 0))
    def _():
        for d in range(2):
            bk = bkt_ref[d]
            for h in range(ATTN_HEADS):
                tile = jnp.zeros((QB, QB), F32)
                for bb in range(REL_BUCKETS):
                    tile = jnp.where(bk == bb, relb_ref[bb, h] * LOG2E, tile)
                bias_ref[d, h] = tile

    t0 = j * QB
    ntile = (j + 2) // 2
    q_pos = t0 + lax.broadcasted_iota(I32, (KT, QB), 1)
    row_iota = lax.broadcasted_iota(I32, (KT, QB), 0)

    w_scale = np.float32(IDX_HEADS ** -0.5) * np.float32(IDX_DIM ** -0.5)
    wrows = wi_ref[0:IDX_HEADS, :] * w_scale

    def score_tile(kt):
        r0 = pl.multiple_of(kt * KT, KT)
        ki_t = kk_ref[pl.ds(r0, KT), KV_WIDTH:KV_WIDTH + IDX_DIM]
        acc = jnp.zeros((KT, QB), F32)
        for c in range(4):
            qi_c = jnp.concatenate([qi_ref[(c * 4 + hh) * IDX_DIM:(c * 4 + hh + 1) * IDX_DIM, :]
                                    for hh in range(4)], axis=1)
            dots = jnp.dot(ki_t, qi_c, preferred_element_type=F32)
            for hh in range(4):
                h = c * 4 + hh
                acc = acc + jnp.maximum(dots[:, hh * QB:(hh + 1) * QB], 0.0) * wrows[h:h + 1, :]
        bits = pltpu.bitcast(acc, I32)
        key = bits ^ ((bits >> 31) & jnp.int32(0x7FFFFFFF))
        key = jnp.where(r0 + row_iota <= q_pos, key, jnp.int32(INT_MIN))
        keys_ref[pl.ds(r0, KT), :] = key

    _for_tiles_grouped(0, ntile, score_tile)

    n_acc = 4

    def count_ge(cand):
        cand8 = jnp.broadcast_to(cand, (8, QB))

        def body(kt, accs):
            r0 = pl.multiple_of(kt * KT, KT)
            accs = list(accs)
            for v in range(KT // 8):
                blk = keys_ref[pl.ds(r0 + v * 8, 8), :]
                accs[v % n_acc] = accs[v % n_acc] + jnp.where(blk >= cand8, jnp.int32(1), jnp.int32(0))
            return tuple(accs)

        accs = lax.fori_loop(0, ntile, body, (jnp.zeros((8, QB), I32),) * n_acc)
        acc8 = (accs[0] + accs[1]) + (accs[2] + accs[3])
        return jnp.sum(acc8.astype(F32), axis=0, keepdims=True).astype(I32)

    c0 = count_ge(jnp.zeros((1, QB), I32))
    ok0 = c0 >= TOPK_MAX
    thr0 = jnp.where(ok0, jnp.int32(0), jnp.int32(INT_MIN))
    cnt0 = jnp.where(ok0, c0, jnp.int32(0))

    def bit_step(i, carry):
        thr, cnt = carry
        cand = thr | (jnp.int32(1) << (30 - i))
        c = count_ge(cand)
        ok = c >= TOPK_MAX
        return jnp.where(ok, cand, thr), jnp.where(ok, c, cnt)

    thr, cnt = lax.fori_loop(0, 31, bit_step, (thr0, cnt0))
    thr = jnp.maximum(thr, jnp.int32(INT_MIN + 1))

    pos_bits = (keys_ref.shape[0] - 1).bit_length()

    @pl.when(jnp.max(cnt.astype(F32)) > TOPK_MAX)
    def _():
        tied = cnt > TOPK_MAX
        need = TOPK_MAX - count_ge(thr + 1)
        thr8 = jnp.broadcast_to(thr, (8, QB))
        sub8 = lax.broadcasted_iota(I32, (8, QB), 0)

        def count_eq_before(limit):
            lim8 = jnp.broadcast_to(limit, (8, QB))

            def body(v, acc):
                r = pl.multiple_of(v * 8, 8)
                hit = jnp.logical_and(keys_ref[pl.ds(r, 8), :] == thr8, r + sub8 < lim8)
                return acc + jnp.where(hit, jnp.int32(1), jnp.int32(0))

            acc = lax.fori_loop(0, ntile * (KT // 8), body, jnp.zeros((8, QB), I32))
            return jnp.sum(acc.astype(F32), axis=0, keepdims=True).astype(I32)

        def pos_step(i, last):
            cand = last | (jnp.int32(1) << (pos_bits - 1 - i))
            return jnp.where(count_eq_before(cand) < need, cand, last)

        last = lax.fori_loop(0, pos_bits, pos_step, jnp.zeros((1, QB), I32))
        last8 = jnp.broadcast_to(jnp.where(tied, last, jnp.int32(2 ** pos_bits)), (8, QB))

        def demote(v, carry):
            r = pl.multiple_of(v * 8, 8)
            blk = keys_ref[pl.ds(r, 8), :]
            drop = jnp.logical_and(blk == thr8, r + sub8 > last8)
            keys_ref[pl.ds(r, 8), :] = jnp.where(drop, thr8 - 1, blk)
            return carry

        lax.fori_loop(0, ntile * (KT // 8), demote, 0)

    qk_scale = np.float32(HEAD_DIM ** -0.5) * LOG2E
    rep = ATTN_HEADS // KV_HEADS
    nfar = jnp.maximum((j - 1) // 2, 0)

    gw = rep * QB
    m_ref[...] = jnp.full(m_ref.shape, NEG_BIG, F32)
    l_ref[...] = jnp.zeros(l_ref.shape, F32)
    acc_ref[...] = jnp.zeros(acc_ref.shape, F32)

    def logits(kt, g, near):
        r0 = pl.multiple_of(kt * KT, KT)
        k_t = kk_ref[pl.ds(r0, KT), g * HEAD_DIM:(g + 1) * HEAD_DIM]
        q_g = jnp.concatenate([q_ref[(g * rep + r) * HEAD_DIM:(g * rep + r + 1) * HEAD_DIM, :]
                               for r in range(rep)], axis=1)
        lg = jnp.dot(k_t, q_g, preferred_element_type=F32) * qk_scale
        sel = keys_ref[pl.ds(r0, KT), :] >= thr
        cols = []
        for r in range(rep):
            h = g * rep + r
            far_bias = relb_ref[REL_BUCKETS - 1, h] * LOG2E
            blk = lg[:, r * QB:(r + 1) * QB]
            if near:
                subs = []
                for sub in range(KT // QB):
                    dblk = j - (kt * (KT // QB) + sub)
                    bias = jnp.where(dblk == 0, bias_ref[0, h],
                                     jnp.where(dblk == 1, bias_ref[1, h], far_bias))
                    subs.append(blk[sub * QB:(sub + 1) * QB, :] + bias)
                blk = jnp.concatenate(subs, axis=0)
            else:
                blk = blk + far_bias
            cols.append(jnp.where(sel, blk, NEG_BIG))
        s = jnp.concatenate(cols, axis=1)
        s_ref[pl.ds(r0, KT), g * gw:(g + 1) * gw] = s
        m_ref[g] = jnp.maximum(m_ref[g], jnp.max(s.reshape(KT // 8, 8, gw), axis=0))

    def far_tile(kt):
        for g in range(KV_HEADS):
            logits(kt, g, False)

    def near_body(kt, carry):
        for g in range(KV_HEADS):
            logits(kt, g, True)
        return carry

    _for_tiles_grouped(0, nfar, far_tile)
    lax.fori_loop(nfar, ntile, near_body, 0)

    m_fin = [jnp.max(m_ref[g], axis=0, keepdims=True) for g in range(KV_HEADS)]

    def weigh(kt):
        r0 = pl.multiple_of(kt * KT, KT)
        for g in range(KV_HEADS):
            p = jnp.exp2(s_ref[pl.ds(r0, KT), g * gw:(g + 1) * gw] - m_fin[g])
            l_ref[g] = l_ref[g] + jnp.sum(p.reshape(KT // 8, 8, gw), axis=0)
            v_t = vt_ref[g * HEAD_DIM:(g + 1) * HEAD_DIM, pl.ds(r0, KT)]
            acc_ref[g] = acc_ref[g] + jnp.dot(v_t, p.astype(BF16), preferred_element_type=F32)

    _for_tiles_grouped(0, ntile, weigh)

    for g in range(KV_HEADS):
        out_t = acc_ref[g] / jnp.sum(l_ref[g], axis=0, keepdims=True)
        for r in range(rep):
            h = g * rep + r
            o_ref[:, h * HEAD_DIM:(h + 1) * HEAD_DIM] = out_t[:, r * QB:(r + 1) * QB].T.astype(o_ref.dtype)


def _dsa_attention(rel_bias, ot, wit, okk, bkt, batch, seq):
    nblk = seq // QB
    rep = ATTN_HEADS // KV_HEADS
    return pl.pallas_call(
        _dsa_kernel,
        grid=(batch, nblk),
        in_specs=[
            pl.BlockSpec(memory_space=pltpu.SMEM),
            pl.BlockSpec((ATTN_WIDTH, QB), lambda b, j: (WT_Q // ATTN_WIDTH, b * nblk + j)),
            pl.BlockSpec((IDX_HEADS * IDX_DIM, QB), lambda b, j: (WT_QI // (IDX_HEADS * IDX_DIM), b * nblk + j)),
            pl.BlockSpec((LANES, QB), lambda b, j: (0, b * nblk + j)),
            pl.BlockSpec((seq, KK_COLS), lambda b, j: (b, 0)),
            pl.BlockSpec((KV_WIDTH, seq), lambda b, j: (WT_V // KV_WIDTH, b)),
            pl.BlockSpec((2, QB, QB), lambda b, j: (0, 0, 0)),
        ],
        out_specs=pl.BlockSpec((None, QB, ATTN_WIDTH), lambda b, j: (b, j, 0)),
        out_shape=jax.ShapeDtypeStruct((batch, seq, ATTN_WIDTH), BF16),
        scratch_shapes=[
            pltpu.VMEM((seq, QB), I32),
            pltpu.VMEM((2, ATTN_HEADS, QB, QB), F32),
            pltpu.VMEM((KV_HEADS, 8, rep * QB), F32),
            pltpu.VMEM((KV_HEADS, 8, rep * QB), F32),
            pltpu.VMEM((KV_HEADS, HEAD_DIM, rep * QB), F32),
            pltpu.VMEM((seq, ATTN_HEADS * QB), F32),
        ],
        compiler_params=pltpu.CompilerParams(dimension_semantics=("arbitrary", "arbitrary")),
        name="dsa_attention",
    )(rel_bias, ot, ot, wit, okk, ot, bkt)


RNN_TS = 256


def _gelu_tanh(x):
    c = np.float32(np.sqrt(2.0 / np.pi))
    return x * (0.5 * (1.0 + jnp.tanh(c * (x + np.float32(0.044715) * (x * x * x)))))


def _softplus(z):
    return jnp.maximum(z, 0.0) + jnp.log1p(jnp.exp(-jnp.abs(z)))


def _rglru_kernel(xr_ref, gate_ref, cw_ref, cb_ref, wa_ref, ba_ref, wx_ref, bx_ref, lam_ref, o_ref,
                  xext_ref, h_ref):
    i = pl.program_id(1)
    ts = RNN_TS

    @pl.when(i == 0)
    def _():
        xext_ref[0:8, :] = jnp.zeros((8, RNN_WIDTH), F32)
        h_ref[...] = jnp.zeros((1, RNN_WIDTH), F32)

    @pl.when(i > 0)
    def _():
        xext_ref[0:8, :] = xext_ref[ts:ts + 8, :]

    xext_ref[8:8 + ts, :] = xr_ref[...]

    row = lax.broadcasted_iota(I32, (ts, RNN_BLOCK_W), 0)
    for n in range(RNN_BLOCKS):
        cs = slice(n * RNN_BLOCK_W, (n + 1) * RNN_BLOCK_W)
        xc = cb_ref[:, cs]
        for jj in range(CONV_WIDTH):
            off = 8 - (CONV_WIDTH - 1) + jj
            xc = xc + xext_ref[off:off + ts, cs] * cw_ref[jj:jj + 1, cs]
        xcb = xc.astype(BF16)
        r = jax.nn.sigmoid(jnp.dot(xcb, wa_ref[n], preferred_element_type=F32) + ba_ref[n:n + 1, :])
        gi = jax.nn.sigmoid(jnp.dot(xcb, wx_ref[n], preferred_element_type=F32) + bx_ref[n:n + 1, :])
        log_a = (-LRU_C * r) * _softplus(-lam_ref[:, cs])
        a = jnp.exp(log_a)
        mult = jnp.sqrt(1.0 - jnp.exp(2.0 * log_a))
        bt = mult * (gi * xc)
        k = 1
        while k < ts:
            a_s = pltpu.roll(a, k, 0)
            b_s = pltpu.roll(bt, k, 0)
            keep = row >= k
            bt = jnp.where(keep, a * b_s + bt, bt)
            a = jnp.where(keep, a * a_s, a)
            k *= 2
        h = bt + a * h_ref[:, cs]
        h_ref[:, cs] = h[ts - 1:ts, :]
        o_ref[:, cs] = (h * _gelu_tanh(gate_ref[:, cs])).astype(o_ref.dtype)


def _rglru(pc, conv_w, conv_b, wa, ba, wx, bx, lam, batch, seq):
    nts = seq // RNN_TS
    full = lambda shape: pl.BlockSpec(shape, lambda b, i: (0,) * len(shape))
    return pl.pallas_call(
        _rglru_kernel,
        grid=(batch, nts),
        in_specs=[
            pl.BlockSpec((RNN_TS, RNN_WIDTH), lambda b, i: (b * nts + i, 0)),
            pl.BlockSpec((RNN_TS, RNN_WIDTH), lambda b, i: (b * nts + i, 1)),
            full((CONV_WIDTH, RNN_WIDTH)),
            full((1, RNN_WIDTH)),
            full((RNN_BLOCKS, RNN_BLOCK_W, RNN_BLOCK_W)),
            full((RNN_BLOCKS, RNN_BLOCK_W)),
            full((RNN_BLOCKS, RNN_BLOCK_W, RNN_BLOCK_W)),
            full((RNN_BLOCKS, RNN_BLOCK_W)),
            full((1, RNN_WIDTH)),
        ],
        out_specs=pl.BlockSpec((RNN_TS, RNN_WIDTH), lambda b, i: (b * nts + i, 0)),
        out_shape=jax.ShapeDtypeStruct((batch * seq, RNN_WIDTH), BF16),
        scratch_shapes=[pltpu.VMEM((RNN_TS + 8, RNN_WIDTH), F32), pltpu.VMEM((1, RNN_WIDTH), F32)],
        compiler_params=pltpu.CompilerParams(dimension_semantics=("arbitrary", "arbitrary")),
        name="rglru",
    )(pc, pc, conv_w, conv_b, wa, ba, wx, bx, lam)


MERGE_TM = 256
ROW_CHUNKS = D_MODEL // LANES


def _store_token_major(ref, base, val):
    n = val.shape[0]
    for c in range(ROW_CHUNKS):
        ref[pl.ds(base + c, n, stride=ROW_CHUNKS), :] = val[:, c * LANES:(c + 1) * LANES]


def _load_token_major(ref, base, n):
    return jnp.concatenate([ref[pl.ds(base + c, n, stride=ROW_CHUNKS), :] for c in range(ROW_CHUNKS)], axis=1)


def _merge_kernel(attn_ref, rnn_ref, ga_ref, gr_ref, x_ref, wpa_ref, wpr_ref, wo_ref, g2_ref,
                  wrh_ref, wrl_ref, br_ref, x1_ref, u2_ref, lgt_ref):
    pa = jnp.dot(attn_ref[...], wpa_ref[...], preferred_element_type=F32)
    pr = jnp.dot(rnn_ref[...], wpr_ref[...], preferred_element_type=F32)
    merged = jax.nn.sigmoid(ga_ref[...]) * pa + jax.nn.sigmoid(gr_ref[...]) * pr
    x1 = x_ref[...] + jnp.dot(merged.astype(BF16), wo_ref[...], preferred_element_type=F32)
    x1_ref[...] = x1
    ms = jnp.mean(x1 * x1, axis=-1, keepdims=True)
    u2 = x1 * lax.rsqrt(ms + EPS) * g2_ref[...]
    _store_token_major(u2_ref, 0, u2)
    hi = u2.astype(BF16)
    lo = (u2 - hi.astype(F32)).astype(BF16)
    lg = (jnp.dot(hi, wrh_ref[...], preferred_element_type=F32)
          + jnp.dot(lo, wrh_ref[...], preferred_element_type=F32)
          + jnp.dot(hi, wrl_ref[...], preferred_element_type=F32)) + br_ref[...]
    lgt_ref[...] = lg.T


def _merge(attn, rnn, pc, x2, wpa, wpr, wo, g2, wrh, wrl, br):
    m = x2.shape[0]
    tm = MERGE_TM
    const = lambda shape: pl.BlockSpec(shape, lambda i: (0,) * len(shape), pipeline_mode=pl.Buffered(1))
    return pl.pallas_call(
        _merge_kernel,
        grid=(m // tm,),
        in_specs=[
            pl.BlockSpec((tm, ATTN_WIDTH), lambda i: (i, 0)),
            pl.BlockSpec((tm, RNN_WIDTH), lambda i: (i, 0)),
            pl.BlockSpec((tm, D_MODEL), lambda i: (i, 1)),
            pl.BlockSpec((tm, D_MODEL), lambda i: (i, 2)),
            pl.BlockSpec((tm, D_MODEL), lambda i: (i, 0)),
            const((ATTN_WIDTH, D_MODEL)),
            const((RNN_WIDTH, D_MODEL)),
            const((D_MODEL, D_MODEL)),
            const((1, D_MODEL)),
            const((D_MODEL, LANES)),
            const((D_MODEL, LANES)),
            const((1, LANES)),
        ],
        out_specs=[
            pl.BlockSpec((tm, D_MODEL), lambda i: (i, 0)),
            pl.BlockSpec((tm * ROW_CHUNKS, LANES), lambda i: (i, 0)),
            pl.BlockSpec((LANES, tm), lambda i: (0, i)),
        ],
        out_shape=[
            jax.ShapeDtypeStruct((m, D_MODEL), F32),
            jax.ShapeDtypeStruct((m * ROW_CHUNKS, LANES), F32),
            jax.ShapeDtypeStruct((LANES, m), F32),
        ],
        compiler_params=pltpu.CompilerParams(dimension_semantics=("arbitrary",)),
        name="merge_outproj",
    )(attn, rnn, pc, pc, x2, wpa, wpr, wo, g2, wrh, wrl, br)


ROUTE_CHUNK = 256


def _first_index_of_max(v, ridx, n):
    vmax = jnp.max(v, axis=0, keepdims=True)
    idx = jnp.min(jnp.where(v == vmax, ridx, jnp.int32(n)).astype(F32), axis=0, keepdims=True)
    return vmax, idx.astype(I32)


def _route_kernel(lgt_ref, dest_ref, gw_ref, meta_ref, cum_ref):
    t = lgt_ref.shape[1]
    eg = EXPERTS_PER_GROUP
    ridx8 = lax.broadcasted_iota(I32, (eg, t), 0)
    gl = lgt_ref[0:N_GROUPS, :]
    gmax, g_sel = _first_index_of_max(gl, ridx8, N_GROUPS)
    p_sel = 1.0 / jnp.sum(jnp.exp(gl - gmax), axis=0, keepdims=True)
    el = lgt_ref[N_GROUPS:N_GROUPS + eg, :]
    for g in range(1, N_GROUPS):
        el = jnp.where(g_sel == g, lgt_ref[N_GROUPS + g * eg:N_GROUPS + (g + 1) * eg, :], el)
    v0, i0 = _first_index_of_max(el, ridx8, eg)
    el1 = jnp.where(ridx8 == i0, -jnp.inf, el)
    v1, i1 = _first_index_of_max(el1, ridx8, eg)
    e1 = jnp.exp(v1 - v0)
    den = 1.0 + e1
    gw_ref[0:1, :] = p_sel * (1.0 / den)
    gw_ref[1:2, :] = p_sel * (e1 / den)
    ex0 = g_sel * eg + i0
    ex1 = g_sel * eg + i1

    eidx = lax.broadcasted_iota(I32, (N_EXPERTS, ROUTE_CHUNK), 0)
    ui = lax.broadcasted_iota(I32, (ROUTE_CHUNK, ROUTE_CHUNK), 0)
    uj = lax.broadcasted_iota(I32, (ROUTE_CHUNK, ROUTE_CHUNK), 1)
    upper = jnp.where(ui < uj, 1.0, 0.0).astype(BF16)
    run = jnp.zeros((N_EXPERTS, 1), F32)
    for c in range(t // ROUTE_CHUNK):
        cs = slice(c * ROUTE_CHUNK, (c + 1) * ROUTE_CHUNK)
        hit = jnp.where(eidx == ex0[:, cs], 1.0, jnp.where(eidx == ex1[:, cs], 1.0, 0.0))
        cum_ref[:, cs] = jnp.dot(hit.astype(BF16), upper, preferred_element_type=F32) + run
        run = run + jnp.sum(hit, axis=1, keepdims=True)

    counts = run.astype(I32)
    padded = ((counts + (MOE_BLOCK - 1)) >> MOE_SHIFT) << MOE_SHIFT
    pe = jnp.broadcast_to(padded, (N_EXPERTS, LANES))
    erow = lax.broadcasted_iota(I32, (N_EXPERTS, LANES), 0)
    k = 1
    while k < N_EXPERTS:
        pe = pe + jnp.where(erow >= k, pltpu.roll(pe, k, 0), 0)
        k *= 2
    pends = pe[:, 0:1]
    pstarts = pends - padded

    eidx_t = lax.broadcasted_iota(I32, (N_EXPERTS, t), 0)
    slot = cum_ref[...] + pstarts.astype(F32)
    dest_ref[0:1, :] = jnp.sum(jnp.where(eidx_t == ex0, slot, 0.0), axis=0, keepdims=True).astype(I32)
    dest_ref[1:2, :] = jnp.sum(jnp.where(eidx_t == ex1, slot, 0.0), axis=0, keepdims=True).astype(I32)

    nb = meta_ref.shape[1]
    own = lax.broadcasted_iota(I32, (N_EXPERTS, nb), 0) == lax.broadcasted_iota(I32, (N_EXPERTS, nb), 1)
    first_blk = (pstarts >> MOE_SHIFT).astype(F32)
    n_blk = (padded >> MOE_SHIFT).astype(F32)
    meta_ref[0:1, :] = jnp.sum(jnp.where(own, first_blk, 0.0), axis=0, keepdims=True).astype(I32)
    meta_ref[1:2, :] = jnp.sum(jnp.where(own, n_blk, 0.0), axis=0, keepdims=True).astype(I32)
    meta_ref[2:3, :] = jnp.broadcast_to(pends[N_EXPERTS - 1:N_EXPERTS, :] >> MOE_SHIFT, (1, nb))


def _route(lgt):
    t = lgt.shape[1]
    return pl.pallas_call(
        _route_kernel,
        out_shape=[
            jax.ShapeDtypeStruct((2, t), I32),
            jax.ShapeDtypeStruct((2, t), F32),
            jax.ShapeDtypeStruct((3, LANES), I32),
        ],
        scratch_shapes=[pltpu.VMEM((N_EXPERTS, t), F32)],
        name="route",
    )(lgt)


W_CHUNKS = 8
GATHER_AHEAD = 3
X_SLOTS = GATHER_AHEAD + 1
W_AHEAD = 2
W_SLOTS = W_AHEAD + 1


def _expert_kernel(first_ref, nblk_ref, nact_ref, dest_ref, u_ref, wg_ref, wu_ref, wd_ref, ys_ref,
                   tok_ref, xbuf_ref, ybuf_ref, wgf_ref, wuf_ref, wdf_ref, wgb_ref, wub_ref, wdb_ref,
                   gsem, osem, wsem):
    e = pl.program_id(0)
    first = first_ref[e]
    nb = nblk_ref[e]
    nact = nact_ref[0]
    ntok = u_ref.shape[0] // ROW_CHUNKS
    nslot = tok_ref.shape[0]
    blk_rows = MOE_BLOCK * ROW_CHUNKS

    def row_copy(blk, slot, r):
        tok = tok_ref[blk * MOE_BLOCK + r]
        src = u_ref.at[pl.ds(pl.multiple_of(tok * ROW_CHUNKS, ROW_CHUNKS), ROW_CHUNKS)]
        dst = xbuf_ref.at[pl.ds(pl.multiple_of(slot * blk_rows + r * ROW_CHUNKS, ROW_CHUNKS), ROW_CHUNKS)]
        return pltpu.make_async_copy(src, dst, gsem.at[slot])

    def start_rows(blk, slot):
        def body(r, carry):
            row_copy(blk, slot, r).start()
            return carry
        lax.fori_loop(0, MOE_BLOCK, body, 0, unroll=8)

    def wait_rows(blk, slot):
        def body(r, carry):
            row_copy(blk, slot, r).wait()
            return carry
        lax.fori_loop(0, MOE_BLOCK, body, 0, unroll=8)

    def out_copy(blk, slot):
        src = ybuf_ref.at[pl.ds(pl.multiple_of(slot * blk_rows, blk_rows), blk_rows)]
        dst = ys_ref.at[pl.ds(pl.multiple_of(blk * blk_rows, blk_rows), blk_rows)]
        return pltpu.make_async_copy(src, dst, osem.at[slot])

    def build_slot_table():
        def clear(p, carry):
            tok_ref[p] = 0
            return carry
        lax.fori_loop(0, nslot, clear, 0, unroll=8)

        def put(t, carry):
            for kk in range(TOP_K_IN_GROUP):
                tok_ref[dest_ref[kk * ntok + t]] = t
            return carry
        lax.fori_loop(0, ntok, put, 0, unroll=8)

    def weight_copies(ex, wslot):
        copies = []
        for src, dst in ((wg_ref, wgf_ref), (wu_ref, wuf_ref), (wd_ref, wdf_ref)):
            rows = src.shape[1] // W_CHUNKS
            for c in range(W_CHUNKS):
                rs = pl.ds(c * rows, rows)
                copies.append(pltpu.make_async_copy(src.at[ex, rs], dst.at[wslot, rs], wsem.at[wslot]))
        return copies

    def start_weights(ex, wslot):
        for cp in weight_copies(ex, wslot):
            cp.start(priority=1)

    @pl.when(e == 0)
    def _():
        for a in range(W_AHEAD):
            start_weights(a, a)

    @pl.when(e + W_AHEAD < pl.num_programs(0))
    def _():
        start_weights(e + W_AHEAD, (e + W_AHEAD) % W_SLOTS)

    @pl.when(e == 0)
    def _():
        build_slot_table()
        for a in range(GATHER_AHEAD):
            @pl.when(a < nact)
            def _():
                start_rows(a, a)

    wslot = e % W_SLOTS
    for cp in weight_copies(e, wslot):
        cp.wait()
    wgb_ref[...] = wgf_ref[wslot].astype(BF16)
    wub_ref[...] = wuf_ref[wslot].astype(BF16)
    wdb_ref[...] = wdf_ref[wslot].astype(BF16)

    def block(blk, carry):
        xslot = blk % X_SLOTS
        yslot = blk % 2

        @pl.when(blk + GATHER_AHEAD < nact)
        def _():
            start_rows(blk + GATHER_AHEAD, (blk + GATHER_AHEAD) % X_SLOTS)

        wait_rows(blk, xslot)
        xb = _load_token_major(xbuf_ref, xslot * blk_rows, MOE_BLOCK).astype(BF16)
        hg = jnp.dot(xb, wgb_ref[...], preferred_element_type=F32)
        hu = jnp.dot(xb, wub_ref[...], preferred_element_type=F32)
        h = (hg * jax.nn.sigmoid(hg)) * hu
        y = jnp.dot(h.astype(BF16), wdb_ref[...], preferred_element_type=F32)

        @pl.when(blk >= 2)
        def _():
            out_copy(blk - 2, yslot).wait()

        _store_token_major(ybuf_ref, yslot * blk_rows, y)
        out_copy(blk, yslot).start()
        return carry

    lax.fori_loop(first, first + nb, block, 0)

    @pl.when(e == pl.num_programs(0) - 1)
    def _():
        @pl.when(nact >= 2)
        def _():
            out_copy(nact - 2, nact % 2).wait()

        @pl.when(nact >= 1)
        def _():
            out_copy(nact - 1, (nact - 1) % 2).wait()

        ntotal = ys_ref.shape[0] // blk_rows
        ybuf_ref[0:blk_rows, :] = jnp.zeros((blk_rows, LANES), ybuf_ref.dtype)

        def fill_start(blk, carry):
            out_copy(blk, 0).start()
            return carry

        def fill_wait(blk, carry):
            out_copy(blk, 0).wait()
            return carry

        lax.fori_loop(nact_ref[0], ntotal, fill_start, 0)
        lax.fori_loop(nact_ref[0], ntotal, fill_wait, 0)


def _experts(first_blk, n_blk, nact, dest_flat, u2, w_gate, w_up, w_down, cap):
    d = D_MODEL
    grid_spec = pltpu.PrefetchScalarGridSpec(
        num_scalar_prefetch=4,
        grid=(N_EXPERTS,),
        in_specs=[pl.BlockSpec(memory_space=pl.ANY)] * 4,
        out_specs=pl.BlockSpec(memory_space=pl.ANY),
        scratch_shapes=[
            pltpu.SMEM((cap,), I32),
            pltpu.VMEM((X_SLOTS * MOE_BLOCK * ROW_CHUNKS, LANES), F32),
            pltpu.VMEM((2 * MOE_BLOCK * ROW_CHUNKS, LANES), F32),
            pltpu.VMEM((W_SLOTS, d, EXPERT_FF), F32),
            pltpu.VMEM((W_SLOTS, d, EXPERT_FF), F32),
            pltpu.VMEM((W_SLOTS, EXPERT_FF, d), F32),
            pltpu.VMEM((d, EXPERT_FF), BF16),
            pltpu.VMEM((d, EXPERT_FF), BF16),
            pltpu.VMEM((EXPERT_FF, d), BF16),
            pltpu.SemaphoreType.DMA((X_SLOTS,)),
            pltpu.SemaphoreType.DMA((2,)),
            pltpu.SemaphoreType.DMA((W_SLOTS,)),
        ],
    )
    return pl.pallas_call(
        _expert_kernel,
        grid_spec=grid_spec,
        out_shape=jax.ShapeDtypeStruct((cap * ROW_CHUNKS, LANES), F32),
        compiler_params=pltpu.CompilerParams(dimension_semantics=("arbitrary",), has_side_effects=True),
        name="experts",
    )(first_blk, n_blk, nact, dest_flat, u2, w_gate, w_up, w_down)


COMB_TOK = 128
COMB_AHEAD = 3
COMB_SLOTS = COMB_AHEAD + 1


def _combine_kernel(dest_ref, ys_ref, x1_ref, gw_ref, gf_ref, o_ref, buf_ref, sem):
    i = pl.program_id(0)
    n = pl.num_programs(0)
    ntok = n * COMB_TOK

    def buf_base(slot, kk):
        return (slot * TOP_K_IN_GROUP + kk) * (COMB_TOK * ROW_CHUNKS)

    def copy(step, slot, tl, kk):
        d = dest_ref[kk * ntok + step * COMB_TOK + tl]
        src = ys_ref.at[pl.ds(pl.multiple_of(d * ROW_CHUNKS, ROW_CHUNKS), ROW_CHUNKS)]
        dst = buf_ref.at[pl.ds(pl.multiple_of(buf_base(slot, kk) + tl * ROW_CHUNKS, ROW_CHUNKS), ROW_CHUNKS)]
        return pltpu.make_async_copy(src, dst, sem.at[slot])

    def start_all(step, slot):
        def body(tl, carry):
            for kk in range(TOP_K_IN_GROUP):
                copy(step, slot, tl, kk).start(priority=kk)
            return carry
        lax.fori_loop(0, COMB_TOK, body, 0, unroll=8)

    def wait_all(step, slot):
        def body(tl, carry):
            for kk in range(TOP_K_IN_GROUP):
                copy(step, slot, tl, kk).wait()
            return carry
        lax.fori_loop(0, COMB_TOK, body, 0, unroll=8)

    @pl.when(i == 0)
    def _():
        for a in range(COMB_AHEAD):
            start_all(a, a)

    @pl.when(i + COMB_AHEAD < n)
    def _():
        start_all(i + COMB_AHEAD, (i + COMB_AHEAD) % COMB_SLOTS)

    slot = i % COMB_SLOTS
    wait_all(i, slot)
    y0 = _load_token_major(buf_ref, buf_base(slot, 0), COMB_TOK)
    y1 = _load_token_major(buf_ref, buf_base(slot, 1), COMB_TOK)
    y = gw_ref[:, 0:1] * y0 + gw_ref[:, 1:2] * y1
    x = x1_ref[...] + y
    ms = jnp.mean(x * x, axis=-1, keepdims=True)
    o_ref[...] = x * lax.rsqrt(ms + EPS) * gf_ref[...]


def _combine(dest, ys, x1, gw_t, gf):
    t, d = x1.shape
    return pl.pallas_call(
        _combine_kernel,
        grid=(t // COMB_TOK,),
        in_specs=[
            pl.BlockSpec(memory_space=pltpu.SMEM),
            pl.BlockSpec(memory_space=pl.ANY),
            pl.BlockSpec((COMB_TOK, d), lambda i: (i, 0)),
            pl.BlockSpec((COMB_TOK, TOP_K_IN_GROUP), lambda i: (i, 0)),
            pl.BlockSpec((1, d), lambda i: (0, 0)),
        ],
        out_specs=pl.BlockSpec((COMB_TOK, d), lambda i: (i, 0)),
        out_shape=jax.ShapeDtypeStruct((t, d), F32),
        scratch_shapes=[
            pltpu.VMEM((COMB_SLOTS * TOP_K_IN_GROUP * COMB_TOK * ROW_CHUNKS, LANES), F32),
            pltpu.SemaphoreType.DMA((COMB_SLOTS,)),
        ],
        compiler_params=pltpu.CompilerParams(dimension_semantics=("arbitrary",)),
        name="combine",
    )(dest, ys, x1, gw_t, gf)


def kernel(x, norm1_g, w_in, conv_w, conv_b, lru_wa, lru_ba, lru_wx, lru_bx, lru_lambda, w_proj_attn,
           w_proj_rnn, w_out, rel_bias, norm2_g, w_group, b_group, w_expert_router, b_expert_router,
           w_gate, w_up, w_down, norm_f_g):
    batch, seq, d = x.shape
    tokens = batch * seq
    x2 = x.reshape(tokens, d)

    w = w_in[0]
    wt_all = jnp.concatenate(
        [w[:, OFF_Q:OFF_K], w[:, OFF_QI:OFF_KI], w[:, OFF_V:OFF_QI],
         jnp.zeros((d, WT_ROWS - WT_V - KV_WIDTH), w.dtype)], axis=1).T.astype(BF16)
    wt_wi = jnp.pad(w[:, OFF_WI:OFF_XR].T, ((0, LANES - IDX_HEADS), (0, 0))).astype(BF16)
    w_kk = jnp.concatenate(
        [w[:, OFF_K:OFF_V], w[:, OFF_KI:OFF_WI],
         jnp.zeros((d, KK_COLS - KV_WIDTH - IDX_DIM), w.dtype)], axis=1).astype(BF16)
    w_c = w[:, OFF_XR:].astype(BF16)
    g1 = norm1_g[0].reshape(1, d)
    ot, wit, okk = _attn_proj(x2, g1, wt_all, wt_wi, w_kk, 1024, 1280)
    pc = _norm_proj(x2, g1, w_c, F32, 1024, 1536)

    ks = np.arange(QB)[:, None]
    qs = np.arange(QB)[None, :]
    bkt = jnp.asarray(np.stack([_rel_bucket_np(qs - ks + QB * dd) for dd in range(2)]))
    attn = _dsa_attention(rel_bias, ot, wit, okk, bkt, batch, seq).reshape(tokens, ATTN_WIDTH)

    rnn = _rglru(pc, conv_w[0], conv_b[0].reshape(1, RNN_WIDTH), lru_wa[0].astype(BF16), lru_ba[0],
                 lru_wx[0].astype(BF16), lru_bx[0], lru_lambda[0].reshape(1, RNN_WIDTH), batch, seq)

    w_r = jnp.concatenate([w_group[0], w_expert_router[0],
                           jnp.zeros((d, LANES - N_GROUPS - N_EXPERTS), F32)], axis=1)
    w_rh = w_r.astype(BF16)
    w_rl = (w_r - w_rh.astype(F32)).astype(BF16)
    b_r = jnp.concatenate([b_group[0], b_expert_router[0],
                           jnp.zeros((LANES - N_GROUPS - N_EXPERTS,), F32)]).reshape(1, LANES)
    x1, u2, lgt = _merge(attn, rnn, pc, x2, w_proj_attn[0].astype(BF16), w_proj_rnn[0].astype(BF16),
                         w_out[0].astype(BF16), norm2_g[0].reshape(1, d), w_rh, w_rl, b_r)

    n_slots = tokens * TOP_K_IN_GROUP
    cap = -(-(n_slots + N_EXPERTS * (MOE_BLOCK - 1)) // MOE_BLOCK) * MOE_BLOCK
    dest, gw, meta = _route(lgt)

    dest_flat = dest.reshape(-1)
    ys = _experts(meta[0, :N_EXPERTS], meta[1, :N_EXPERTS], meta[2, :1], dest_flat, u2,
                  w_gate[0], w_up[0], w_down[0], cap)
    out = _combine(dest_flat, ys, x1, gw.T, norm_f_g.reshape(1, d))
    return out.reshape(batch, seq, d)
```

```python
import numpy as np
import jax
import jax.numpy as jnp
from jax import lax
from jax.experimental import pallas as pl
from jax.experimental.pallas import tpu as pltpu

D_MODEL = 2048
ATTN_HEADS = 8
KV_HEADS = 2
HEAD_DIM = 128
ATTN_WIDTH = ATTN_HEADS * HEAD_DIM
KV_WIDTH = KV_HEADS * HEAD_DIM
IDX_HEADS = 16
IDX_DIM = 64
TOPK_MAX = 256
RNN_WIDTH = 1024
RNN_BLOCKS = 8
RNN_BLOCK_W = RNN_WIDTH // RNN_BLOCKS
CONV_WIDTH = 4
LRU_C = 8.0
REL_BUCKETS = 32
REL_MAX_DIST = 128
N_GROUPS = 8
EXPERTS_PER_GROUP = 8
N_EXPERTS = N_GROUPS * EXPERTS_PER_GROUP
TOP_K_IN_GROUP = 2
EXPERT_FF = 512
MOE_BLOCK = 128
MOE_SHIFT = 7
EPS = 1e-6

LANES = 128
QB = 128
KT = 256
NEG_BIG = -1e30
INT_MIN = -(2 ** 31)
LOG2E = np.float32(np.log2(np.e))
F32 = jnp.float32
BF16 = jnp.bfloat16
I32 = jnp.int32

OFF_Q = 0
OFF_K = OFF_Q + ATTN_WIDTH
OFF_V = OFF_K + KV_WIDTH
OFF_QI = OFF_V + KV_WIDTH
OFF_KI = OFF_QI + IDX_HEADS * IDX_DIM
OFF_WI = OFF_KI + IDX_DIM
OFF_XR = OFF_WI + IDX_HEADS


def _rel_bucket_np(n):
    n = np.maximum(n, 0)
    max_exact = REL_BUCKETS // 2
    nf = np.maximum(n, 1).astype(np.float32)
    large = max_exact + (np.log(nf / np.float32(max_exact)) / np.float32(np.log(REL_MAX_DIST / max_exact))
                         * np.float32(REL_BUCKETS - max_exact)).astype(np.int32)
    large = np.minimum(large, REL_BUCKETS - 1)
    return np.where(n < max_exact, n, large).astype(np.int32)


def _norm_proj_kernel(x_ref, g_ref, w_ref, o_ref, u_ref):
    @pl.when(pl.program_id(1) == 0)
    def _():
        x = x_ref[...]
        ms = jnp.mean(x * x, axis=-1, keepdims=True)
        u_ref[...] = (x * lax.rsqrt(ms + EPS) * g_ref[...]).astype(BF16)

    o_ref[...] = jnp.dot(u_ref[...], w_ref[...], preferred_element_type=F32).astype(o_ref.dtype)


def _norm_proj(x2, g, w, out_dtype, tm, tn):
    m, d = x2.shape
    n = w.shape[1]
    return pl.pallas_call(
        _norm_proj_kernel,
        grid=(m // tm, n // tn),
        in_specs=[
            pl.BlockSpec((tm, d), lambda i, j: (i, 0)),
            pl.BlockSpec((1, d), lambda i, j: (0, 0)),
            pl.BlockSpec((d, tn), lambda i, j: (0, j)),
        ],
        out_specs=pl.BlockSpec((tm, tn), lambda i, j: (i, j)),
        out_shape=jax.ShapeDtypeStruct((m, n), out_dtype),
        scratch_shapes=[pltpu.VMEM((tm, d), BF16)],
        compiler_params=pltpu.CompilerParams(dimension_semantics=("arbitrary", "arbitrary")),
        name="norm_proj",
    )(x2, g, w)


WT_ROWS = 2560
WT_Q, WT_QI, WT_V = 0, ATTN_WIDTH, ATTN_WIDTH + IDX_HEADS * IDX_DIM
KK_COLS = 384


def _attn_proj_kernel(x_ref, g_ref, wt_ref, wtwi_ref, wkk_ref, ot_ref, wit_ref, okk_ref, u_ref):
    nt = (((1,), (1,)), ((), ()))

    @pl.when(pl.program_id(1) == 0)
    def _():
        x = x_ref[...]
        ms = jnp.mean(x * x, axis=-1, keepdims=True)
        u = (x * lax.rsqrt(ms + EPS) * g_ref[...]).astype(BF16)
        u_ref[...] = u
        okk_ref[...] = jnp.dot(u, wkk_ref[...], preferred_element_type=F32).astype(okk_ref.dtype)
        wit_ref[...] = lax.dot_general(wtwi_ref[...], u, nt, preferred_element_type=F32)

    ot_ref[...] = lax.dot_general(wt_ref[...], u_ref[...], nt, preferred_element_type=F32).astype(ot_ref.dtype)


def _attn_proj(x2, g, wt_all, wt_wi, w_kk, tm, tr):
    m, d = x2.shape
    return pl.pallas_call(
        _attn_proj_kernel,
        grid=(m // tm, WT_ROWS // tr),
        in_specs=[
            pl.BlockSpec((tm, d), lambda i, j: (i, 0)),
            pl.BlockSpec((1, d), lambda i, j: (0, 0)),
            pl.BlockSpec((tr, d), lambda i, j: (j, 0)),
            pl.BlockSpec((LANES, d), lambda i, j: (0, 0)),
            pl.BlockSpec((d, KK_COLS), lambda i, j: (0, 0)),
        ],
        out_specs=[
            pl.BlockSpec((tr, tm), lambda i, j: (j, i)),
            pl.BlockSpec((LANES, tm), lambda i, j: (0, i)),
            pl.BlockSpec((tm, KK_COLS), lambda i, j: (i, 0)),
        ],
        out_shape=[
            jax.ShapeDtypeStruct((WT_ROWS, m), BF16),
            jax.ShapeDtypeStruct((LANES, m), F32),
            jax.ShapeDtypeStruct((m, KK_COLS), BF16),
        ],
        scratch_shapes=[pltpu.VMEM((tm, d), BF16)],
        compiler_params=pltpu.CompilerParams(dimension_semantics=("arbitrary", "arbitrary")),
        name="attn_proj",
    )(x2, g, wt_all, wt_wi, w_kk)


TILES_PER_TRIP = 8


def _for_tiles_grouped(lo, hi, tile_fn):
    n = hi - lo
    full = n // TILES_PER_TRIP

    def group(i, carry):
        for u in range(TILES_PER_TRIP):
            tile_fn(lo + TILES_PER_TRIP * i + u)
        return carry

    lax.fori_loop(0, full, group, 0)

    done = lo + full * TILES_PER_TRIP
    size = TILES_PER_TRIP // 2
    while size >= 1:
        take = (n & size) != 0

        @pl.when(take)
        def _(done=done, size=size):
            for u in range(size):
                tile_fn(done + u)

        done = done + jnp.where(take, size, 0)
        size //= 2


def _dsa_kernel(relb_ref, q_ref, qi_ref, wi_ref, kk_ref, vt_ref, bkt_ref, o_ref,
                keys_ref, bias_ref, m_ref, l_ref, acc_ref, s_ref):
    b = pl.program_id(0)
    j = pl.program_id(1)

    @pl.when((b == 0) & (j == 0))
    def _():
        for d in range(2):
            bk = bkt_ref[d]
            for h in range(ATTN_HEADS):
                tile = jnp.zeros((QB, QB), F32)
                for bb in range(REL_BUCKETS):
                    tile = jnp.where(bk == bb, relb_ref[bb, h] * LOG2E, tile)
                bias_ref[d, h] = tile

    t0 = j * QB
    ntile = (j + 2) // 2
    q_pos = t0 + lax.broadcasted_iota(I32, (KT, QB), 1)
    row_iota = lax.broadcasted_iota(I32, (KT, QB), 0)

    w_scale = np.float32(IDX_HEADS ** -0.5) * np.float32(IDX_DIM ** -0.5)
    wrows = wi_ref[0:IDX_HEADS, :] * w_scale

    def score_tile(kt):
        r0 = pl.multiple_of(kt * KT, KT)
        ki_t = kk_ref[pl.ds(r0, KT), KV_WIDTH:KV_WIDTH + IDX_DIM]
        acc = jnp.zeros((KT, QB), F32)
        for c in range(4):
            qi_c = jnp.concatenate([qi_ref[(c * 4 + hh) * IDX_DIM:(c * 4 + hh + 1) * IDX_DIM, :]
                                    for hh in range(4)], axis=1)
            dots = jnp.dot(ki_t, qi_c, preferred_element_type=F32)
            for hh in range(4):
                h = c * 4 + hh
                acc = acc + jnp.maximum(dots[:, hh * QB:(hh + 1) * QB], 0.0) * wrows[h:h + 1, :]
        bits = pltpu.bitcast(acc, I32)
        key = bits ^ ((bits >> 31) & jnp.int32(0x7FFFFFFF))
        key = jnp.where(r0 + row_iota <= q_pos, key, jnp.int32(INT_MIN))
        keys_ref[pl.ds(r0, KT), :] = key

    _for_tiles_grouped(0, ntile, score_tile)

    n_acc = 4

    def count_ge(cand):
        cand8 = jnp.broadcast_to(cand, (8, QB))

        def body(kt, accs):
            r0 = pl.multiple_of(kt * KT, KT)
            accs = list(accs)
            for v in range(KT // 8):
                blk = keys_ref[pl.ds(r0 + v * 8, 8), :]
                accs[v % n_acc] = accs[v % n_acc] + jnp.where(blk >= cand8, jnp.int32(1), jnp.int32(0))
            return tuple(accs)

        accs = lax.fori_loop(0, ntile, body, (jnp.zeros((8, QB), I32),) * n_acc)
        acc8 = (accs[0] + accs[1]) + (accs[2] + accs[3])
        return jnp.sum(acc8.astype(F32), axis=0, keepdims=True).astype(I32)

    c0 = count_ge(jnp.zeros((1, QB), I32))
    ok0 = c0 >= TOPK_MAX
    thr0 = jnp.where(ok0, jnp.int32(0), jnp.int32(INT_MIN))
    cnt0 = jnp.where(ok0, c0, jnp.int32(0))

    def bit_step(i, carry):
        thr, cnt = carry
        cand = thr | (jnp.int32(1) << (30 - i))
        c = count_ge(cand)
        ok = c >= TOPK_MAX
        return jnp.where(ok, cand, thr), jnp.where(ok, c, cnt)

    thr, cnt = lax.fori_loop(0, 31, bit_step, (thr0, cnt0))
    thr = jnp.maximum(thr, jnp.int32(INT_MIN + 1))

    pos_bits = (keys_ref.shape[0] - 1).bit_length()

    @pl.when(jnp.max(cnt.astype(F32)) > TOPK_MAX)
    def _():
        tied = cnt > TOPK_MAX
        need = TOPK_MAX - count_ge(thr + 1)
        thr8 = jnp.broadcast_to(thr, (8, QB))
        sub8 = lax.broadcasted_iota(I32, (8, QB), 0)

        def count_eq_before(limit):
            lim8 = jnp.broadcast_to(limit, (8, QB))

            def body(v, acc):
                r = pl.multiple_of(v * 8, 8)
                hit = jnp.logical_and(keys_ref[pl.ds(r, 8), :] == thr8, r + sub8 < lim8)
                return acc + jnp.where(hit, jnp.int32(1), jnp.int32(0))

            acc = lax.fori_loop(0, ntile * (KT // 8), body, jnp.zeros((8, QB), I32))
            return jnp.sum(acc.astype(F32), axis=0, keepdims=True).astype(I32)

        def pos_step(i, last):
            cand = last | (jnp.int32(1) << (pos_bits - 1 - i))
            return jnp.where(count_eq_before(cand) < need, cand, last)

        last = lax.fori_loop(0, pos_bits, pos_step, jnp.zeros((1, QB), I32))
        last8 = jnp.broadcast_to(jnp.where(tied, last, jnp.int32(2 ** pos_bits)), (8, QB))

        def demote(v, carry):
            r = pl.multiple_of(v * 8, 8)
            blk = keys_ref[pl.ds(r, 8), :]
            drop = jnp.logical_and(blk == thr8, r + sub8 > last8)
            keys_ref[pl.ds(r, 8), :] = jnp.where(drop, thr8 - 1, blk)
            return carry

        lax.fori_loop(0, ntile * (KT // 8), demote, 0)

    qk_scale = np.float32(HEAD_DIM ** -0.5) * LOG2E
    rep = ATTN_HEADS // KV_HEADS
    nfar = jnp.maximum((j - 1) // 2, 0)

    gw = rep * QB
    m_ref[...] = jnp.full(m_ref.shape, NEG_BIG, F32)
    l_ref[...] = jnp.zeros(l_ref.shape, F32)
    acc_ref[...] = jnp.zeros(acc_ref.shape, F32)

    def logits(kt, g, near):
        r0 = pl.multiple_of(kt * KT, KT)
        k_t = kk_ref[pl.ds(r0, KT), g * HEAD_DIM:(g + 1) * HEAD_DIM]
        q_g = jnp.concatenate([q_ref[(g * rep + r) * HEAD_DIM:(g * rep + r + 1) * HEAD_DIM, :]
                               for r in range(rep)], axis=1)
        lg = jnp.dot(k_t, q_g, preferred_element_type=F32) * qk_scale
        sel = keys_ref[pl.ds(r0, KT), :] >= thr
        cols = []
        for r in range(rep):
            h = g * rep + r
            far_bias = relb_ref[REL_BUCKETS - 1, h] * LOG2E
            blk = lg[:, r * QB:(r + 1) * QB]
            if near:
                subs = []
                for sub in range(KT // QB):
                    dblk = j - (kt * (KT // QB) + sub)
                    bias = jnp.where(dblk == 0, bias_ref[0, h],
                                     jnp.where(dblk == 1, bias_ref[1, h], far_bias))
                    subs.append(blk[sub * QB:(sub + 1) * QB, :] + bias)
                blk = jnp.concatenate(subs, axis=0)
            else:
                blk = blk + far_bias
            cols.append(jnp.where(sel, blk, NEG_BIG))
        s = jnp.concatenate(cols, axis=1)
        s_ref[pl.ds(r0, KT), g * gw:(g + 1) * gw] = s
        m_ref[g] = jnp.maximum(m_ref[g], jnp.max(s.reshape(KT // 8, 8, gw), axis=0))

    def far_tile(kt):
        for g in range(KV_HEADS):
            logits(kt, g, False)

    def near_body(kt, carry):
        for g in range(KV_HEADS):
            logits(kt, g, True)
        return carry

    _for_tiles_grouped(0, nfar, far_tile)
    lax.fori_loop(nfar, ntile, near_body, 0)

    m_fin = [jnp.max(m_ref[g], axis=0, keepdims=True) for g in range(KV_HEADS)]

    def weigh(kt):
        r0 = pl.multiple_of(kt * KT, KT)
        for g in range(KV_HEADS):
            p = jnp.exp2(s_ref[pl.ds(r0, KT), g * gw:(g + 1) * gw] - m_fin[g])
            l_ref[g] = l_ref[g] + jnp.sum(p.reshape(KT // 8, 8, gw), axis=0)
            v_t = vt_ref[g * HEAD_DIM:(g + 1) * HEAD_DIM, pl.ds(r0, KT)]
            acc_ref[g] = acc_ref[g] + jnp.dot(v_t, p.astype(BF16), preferred_element_type=F32)

    _for_tiles_grouped(0, ntile, weigh)

    for g in range(KV_HEADS):
        out_t = acc_ref[g] / jnp.sum(l_ref[g], axis=0, keepdims=True)
        for r in range(rep):
            h = g * rep + r
            o_ref[:, h * HEAD_DIM:(h + 1) * HEAD_DIM] = out_t[:, r * QB:(r + 1) * QB].T.astype(o_ref.dtype)


def _dsa_attention(rel_bias, ot, wit, okk, bkt, batch, seq):
    nblk = seq // QB
    rep = ATTN_HEADS // KV_HEADS
    return pl.pallas_call(
        _dsa_kernel,
        grid=(batch, nblk),
        in_specs=[
            pl.BlockSpec(memory_space=pltpu.SMEM),
            pl.BlockSpec((ATTN_WIDTH, QB), lambda b, j: (WT_Q // ATTN_WIDTH, b * nblk + j)),
            pl.BlockSpec((IDX_HEADS * IDX_DIM, QB), lambda b, j: (WT_QI // (IDX_HEADS * IDX_DIM), b * nblk + j)),
            pl.BlockSpec((LANES, QB), lambda b, j: (0, b * nblk + j)),
            pl.BlockSpec((seq, KK_COLS), lambda b, j: (b, 0)),
            pl.BlockSpec((KV_WIDTH, seq), lambda b, j: (WT_V // KV_WIDTH, b)),
            pl.BlockSpec((2, QB, QB), lambda b, j: (0, 0, 0)),
        ],
        out_specs=pl.BlockSpec((None, QB, ATTN_WIDTH), lambda b, j: (b, j, 0)),
        out_shape=jax.ShapeDtypeStruct((batch, seq, ATTN_WIDTH), BF16),
        scratch_shapes=[
            pltpu.VMEM((seq, QB), I32),
            pltpu.VMEM((2, ATTN_HEADS, QB, QB), F32),
            pltpu.VMEM((KV_HEADS, 8, rep * QB), F32),
            pltpu.VMEM((KV_HEADS, 8, rep * QB), F32),
            pltpu.VMEM((KV_HEADS, HEAD_DIM, rep * QB), F32),
            pltpu.VMEM((seq, ATTN_HEADS * QB), F32),
        ],
        compiler_params=pltpu.CompilerParams(dimension_semantics=("arbitrary", "arbitrary")),
        name="dsa_attention",
    )(rel_bias, ot, ot, wit, okk, ot, bkt)


RNN_TS = 256


def _gelu_tanh(x):
    c = np.float32(np.sqrt(2.0 / np.pi))
    return x * (0.5 * (1.0 + jnp.tanh(c * (x + np.float32(0.044715) * (x * x * x)))))


def _softplus(z):
    return jnp.maximum(z, 0.0) + jnp.log1p(jnp.exp(-jnp.abs(z)))


def _rglru_kernel(xr_ref, gate_ref, cw_ref, cb_ref, wa_ref, ba_ref, wx_ref, bx_ref, lam_ref, o_ref,
                  xext_ref, h_ref):
    i = pl.program_id(1)
    ts = RNN_TS

    @pl.when(i == 0)
    def _():
        xext_ref[0:8, :] = jnp.zeros((8, RNN_WIDTH), F32)
        h_ref[...] = jnp.zeros((1, RNN_WIDTH), F32)

    @pl.when(i > 0)
    def _():
        xext_ref[0:8, :] = xext_ref[ts:ts + 8, :]

    xext_ref[8:8 + ts, :] = xr_ref[...]

    row = lax.broadcasted_iota(I32, (ts, RNN_BLOCK_W), 0)
    for n in range(RNN_BLOCKS):
        cs = slice(n * RNN_BLOCK_W, (n + 1) * RNN_BLOCK_W)
        xc = cb_ref[:, cs]
        for jj in range(CONV_WIDTH):
            off = 8 - (CONV_WIDTH - 1) + jj
            xc = xc + xext_ref[off:off + ts, cs] * cw_ref[jj:jj + 1, cs]
        xcb = xc.astype(BF16)
        r = jax.nn.sigmoid(jnp.dot(xcb, wa_ref[n], preferred_element_type=F32) + ba_ref[n:n + 1, :])
        gi = jax.nn.sigmoid(jnp.dot(xcb, wx_ref[n], preferred_element_type=F32) + bx_ref[n:n + 1, :])
        log_a = (-LRU_C * r) * _softplus(-lam_ref[:, cs])
        a = jnp.exp(log_a)
        mult = jnp.sqrt(1.0 - jnp.exp(2.0 * log_a))
        bt = mult * (gi * xc)
        k = 1
        while k < ts:
            a_s = pltpu.roll(a, k, 0)
            b_s = pltpu.roll(bt, k, 0)
            keep = row >= k
            bt = jnp.where(keep, a * b_s + bt, bt)
            a = jnp.where(keep, a * a_s, a)
            k *= 2
        h = bt + a * h_ref[:, cs]
        h_ref[:, cs] = h[ts - 1:ts, :]
        o_ref[:, cs] = (h * _gelu_tanh(gate_ref[:, cs])).astype(o_ref.dtype)


def _rglru(pc, conv_w, conv_b, wa, ba, wx, bx, lam, batch, seq):
    nts = seq // RNN_TS
    full = lambda shape: pl.BlockSpec(shape, lambda b, i: (0,) * len(shape))
    return pl.pallas_call(
        _rglru_kernel,
        grid=(batch, nts),
        in_specs=[
            pl.BlockSpec((RNN_TS, RNN_WIDTH), lambda b, i: (b * nts + i, 0)),
            pl.BlockSpec((RNN_TS, RNN_WIDTH), lambda b, i: (b * nts + i, 1)),
            full((CONV_WIDTH, RNN_WIDTH)),
            full((1, RNN_WIDTH)),
            full((RNN_BLOCKS, RNN_BLOCK_W, RNN_BLOCK_W)),
            full((RNN_BLOCKS, RNN_BLOCK_W)),
            full((RNN_BLOCKS, RNN_BLOCK_W, RNN_BLOCK_W)),
            full((RNN_BLOCKS, RNN_BLOCK_W)),
            full((1, RNN_WIDTH)),
        ],
        out_specs=pl.BlockSpec((RNN_TS, RNN_WIDTH), lambda b, i: (b * nts + i, 0)),
        out_shape=jax.ShapeDtypeStruct((batch * seq, RNN_WIDTH), BF16),
        scratch_shapes=[pltpu.VMEM((RNN_TS + 8, RNN_WIDTH), F32), pltpu.VMEM((1, RNN_WIDTH), F32)],
        compiler_params=pltpu.CompilerParams(dimension_semantics=("arbitrary", "arbitrary")),
        name="rglru",
    )(pc, pc, conv_w, conv_b, wa, ba, wx, bx, lam)


MERGE_TM = 256
ROW_CHUNKS = D_MODEL // (2 * LANES)
U32 = jnp.uint32


def _store_token_major(ref, base, val):
    n = val.shape[0]
    for c in range(ROW_CHUNKS):
        hi = pltpu.bitcast(val[:, 2 * c * LANES:(2 * c + 1) * LANES].astype(BF16).astype(F32), U32)
        lo = pltpu.bitcast(val[:, (2 * c + 1) * LANES:(2 * c + 2) * LANES].astype(BF16).astype(F32), U32)
        ref[pl.ds(base + c, n, stride=ROW_CHUNKS), :] = hi | (lo >> 16)


def _load_token_major(ref, base, n):
    parts = []
    for c in range(ROW_CHUNKS):
        w = ref[pl.ds(base + c, n, stride=ROW_CHUNKS), :]
        parts.append(pltpu.bitcast(w & U32(0xFFFF0000), F32))
        parts.append(pltpu.bitcast(w << 16, F32))
    return jnp.concatenate(parts, axis=1)


def _merge_kernel(attn_ref, rnn_ref, ga_ref, gr_ref, x_ref, wpa_ref, wpr_ref, wo_ref, g2_ref,
                  wrh_ref, wrl_ref, br_ref, x1_ref, u2_ref, lgt_ref):
    pa = jnp.dot(attn_ref[...], wpa_ref[...], preferred_element_type=F32)
    pr = jnp.dot(rnn_ref[...], wpr_ref[...], preferred_element_type=F32)
    merged = jax.nn.sigmoid(ga_ref[...]) * pa + jax.nn.sigmoid(gr_ref[...]) * pr
    x1 = x_ref[...] + jnp.dot(merged.astype(BF16), wo_ref[...], preferred_element_type=F32)
    x1_ref[...] = x1
    ms = jnp.mean(x1 * x1, axis=-1, keepdims=True)
    u2 = x1 * lax.rsqrt(ms + EPS) * g2_ref[...]
    _store_token_major(u2_ref, 0, u2)
    hi = u2.astype(BF16)
    lo = (u2 - hi.astype(F32)).astype(BF16)
    lg = (jnp.dot(hi, wrh_ref[...], preferred_element_type=F32)
          + jnp.dot(lo, wrh_ref[...], preferred_element_type=F32)
          + jnp.dot(hi, wrl_ref[...], preferred_element_type=F32)) + br_ref[...]
    lgt_ref[...] = lg.T


def _merge(attn, rnn, pc, x2, wpa, wpr, wo, g2, wrh, wrl, br):
    m = x2.shape[0]
    tm = MERGE_TM
    const = lambda shape: pl.BlockSpec(shape, lambda i: (0,) * len(shape), pipeline_mode=pl.Buffered(1))
    return pl.pallas_call(
        _merge_kernel,
        grid=(m // tm,),
        in_specs=[
            pl.BlockSpec((tm, ATTN_WIDTH), lambda i: (i, 0)),
            pl.BlockSpec((tm, RNN_WIDTH), lambda i: (i, 0)),
            pl.BlockSpec((tm, D_MODEL), lambda i: (i, 1)),
            pl.BlockSpec((tm, D_MODEL), lambda i: (i, 2)),
            pl.BlockSpec((tm, D_MODEL), lambda i: (i, 0)),
            const((ATTN_WIDTH, D_MODEL)),
            const((RNN_WIDTH, D_MODEL)),
            const((D_MODEL, D_MODEL)),
            const((1, D_MODEL)),
            const((D_MODEL, LANES)),
            const((D_MODEL, LANES)),
            const((1, LANES)),
        ],
        out_specs=[
            pl.BlockSpec((tm, D_MODEL), lambda i: (i, 0)),
            pl.BlockSpec((tm * ROW_CHUNKS, LANES), lambda i: (i, 0)),
            pl.BlockSpec((LANES, tm), lambda i: (0, i)),
        ],
        out_shape=[
            jax.ShapeDtypeStruct((m, D_MODEL), F32),
            jax.ShapeDtypeStruct((m * ROW_CHUNKS, LANES), U32),
            jax.ShapeDtypeStruct((LANES, m), F32),
        ],
        compiler_params=pltpu.CompilerParams(dimension_semantics=("arbitrary",)),
        name="merge_outproj",
    )(attn, rnn, pc, pc, x2, wpa, wpr, wo, g2, wrh, wrl, br)


ROUTE_CHUNK = 256


def _first_index_of_max(v, ridx, n):
    vmax = jnp.max(v, axis=0, keepdims=True)
    idx = jnp.min(jnp.where(v == vmax, ridx, jnp.int32(n)).astype(F32), axis=0, keepdims=True)
    return vmax, idx.astype(I32)


def _route_kernel(lgt_ref, dest_ref, gw_ref, meta_ref, cum_ref):
    t = lgt_ref.shape[1]
    eg = EXPERTS_PER_GROUP
    ridx8 = lax.broadcasted_iota(I32, (eg, t), 0)
    gl = lgt_ref[0:N_GROUPS, :]
    gmax, g_sel = _first_index_of_max(gl, ridx8, N_GROUPS)
    p_sel = 1.0 / jnp.sum(jnp.exp(gl - gmax), axis=0, keepdims=True)
    el = lgt_ref[N_GROUPS:N_GROUPS + eg, :]
    for g in range(1, N_GROUPS):
        el = jnp.where(g_sel == g, lgt_ref[N_GROUPS + g * eg:N_GROUPS + (g + 1) * eg, :], el)
    v0, i0 = _first_index_of_max(el, ridx8, eg)
    el1 = jnp.where(ridx8 == i0, -jnp.inf, el)
    v1, i1 = _first_index_of_max(el1, ridx8, eg)
    e1 = jnp.exp(v1 - v0)
    den = 1.0 + e1
    gw_ref[0:1, :] = p_sel * (1.0 / den)
    gw_ref[1:2, :] = p_sel * (e1 / den)
    ex0 = g_sel * eg + i0
    ex1 = g_sel * eg + i1

    eidx = lax.broadcasted_iota(I32, (N_EXPERTS, ROUTE_CHUNK), 0)
    ui = lax.broadcasted_iota(I32, (ROUTE_CHUNK, ROUTE_CHUNK), 0)
    uj = lax.broadcasted_iota(I32, (ROUTE_CHUNK, ROUTE_CHUNK), 1)
    upper = jnp.where(ui < uj, 1.0, 0.0).astype(BF16)
    run = jnp.zeros((N_EXPERTS, 1), F32)
    for c in range(t // ROUTE_CHUNK):
        cs = slice(c * ROUTE_CHUNK, (c + 1) * ROUTE_CHUNK)
        hit = jnp.where(eidx == ex0[:, cs], 1.0, jnp.where(eidx == ex1[:, cs], 1.0, 0.0))
        cum_ref[:, cs] = jnp.dot(hit.astype(BF16), upper, preferred_element_type=F32) + run
        run = run + jnp.sum(hit, axis=1, keepdims=True)

    counts = run.astype(I32)
    padded = ((counts + (MOE_BLOCK - 1)) >> MOE_SHIFT) << MOE_SHIFT
    pe = jnp.broadcast_to(padded, (N_EXPERTS, LANES))
    erow = lax.broadcasted_iota(I32, (N_EXPERTS, LANES), 0)
    k = 1
    while k < N_EXPERTS:
        pe = pe + jnp.where(erow >= k, pltpu.roll(pe, k, 0), 0)
        k *= 2
    pends = pe[:, 0:1]
    pstarts = pends - padded

    eidx_t = lax.broadcasted_iota(I32, (N_EXPERTS, t), 0)
    slot = cum_ref[...] + pstarts.astype(F32)
    dest_ref[0:1, :] = jnp.sum(jnp.where(eidx_t == ex0, slot, 0.0), axis=0, keepdims=True).astype(I32)
    dest_ref[1:2, :] = jnp.sum(jnp.where(eidx_t == ex1, slot, 0.0), axis=0, keepdims=True).astype(I32)

    nb = meta_ref.shape[1]
    own = lax.broadcasted_iota(I32, (N_EXPERTS, nb), 0) == lax.broadcasted_iota(I32, (N_EXPERTS, nb), 1)
    first_blk = (pstarts >> MOE_SHIFT).astype(F32)
    n_blk = (padded >> MOE_SHIFT).astype(F32)
    meta_ref[0:1, :] = jnp.sum(jnp.where(own, first_blk, 0.0), axis=0, keepdims=True).astype(I32)
    meta_ref[1:2, :] = jnp.sum(jnp.where(own, n_blk, 0.0), axis=0, keepdims=True).astype(I32)
    meta_ref[2:3, :] = jnp.broadcast_to(pends[N_EXPERTS - 1:N_EXPERTS, :] >> MOE_SHIFT, (1, nb))


def _route(lgt):
    t = lgt.shape[1]
    return pl.pallas_call(
        _route_kernel,
        out_shape=[
            jax.ShapeDtypeStruct((2, t), I32),
            jax.ShapeDtypeStruct((2, t), F32),
            jax.ShapeDtypeStruct((3, LANES), I32),
        ],
        scratch_shapes=[pltpu.VMEM((N_EXPERTS, t), F32)],
        name="route",
    )(lgt)


W_CHUNKS = 8
GATHER_AHEAD = 3
X_SLOTS = GATHER_AHEAD + 1
W_AHEAD = 2
W_SLOTS = W_AHEAD + 1


def _expert_kernel(first_ref, nblk_ref, nact_ref, dest_ref, u_ref, wg_ref, wu_ref, wd_ref, ys_ref,
                   tok_ref, xbuf_ref, ybuf_ref, wgf_ref, wuf_ref, wdf_ref, wgb_ref, wub_ref, wdb_ref,
                   gsem, osem, wsem):
    e = pl.program_id(0)
    first = first_ref[e]
    nb = nblk_ref[e]
    nact = nact_ref[0]
    ntok = u_ref.shape[0] // ROW_CHUNKS
    nslot = tok_ref.shape[0]
    blk_rows = MOE_BLOCK * ROW_CHUNKS

    def row_copy(blk, slot, r):
        tok = tok_ref[blk * MOE_BLOCK + r]
        src = u_ref.at[pl.ds(pl.multiple_of(tok * ROW_CHUNKS, ROW_CHUNKS), ROW_CHUNKS)]
        dst = xbuf_ref.at[pl.ds(pl.multiple_of(slot * blk_rows + r * ROW_CHUNKS, ROW_CHUNKS), ROW_CHUNKS)]
        return pltpu.make_async_copy(src, dst, gsem.at[slot])

    def start_rows(blk, slot):
        def body(r, carry):
            row_copy(blk, slot, r).start()
            return carry
        lax.fori_loop(0, MOE_BLOCK, body, 0, unroll=8)

    def wait_rows(blk, slot):
        def body(r, carry):
            row_copy(blk, slot, r).wait()
            return carry
        lax.fori_loop(0, MOE_BLOCK, body, 0, unroll=8)

    def out_copy(blk, slot):
        src = ybuf_ref.at[pl.ds(pl.multiple_of(slot * blk_rows, blk_rows), blk_rows)]
        dst = ys_ref.at[pl.ds(pl.multiple_of(blk * blk_rows, blk_rows), blk_rows)]
        return pltpu.make_async_copy(src, dst, osem.at[slot])

    def build_slot_table():
        def clear(p, carry):
            tok_ref[p] = 0
            return carry
        lax.fori_loop(0, nslot, clear, 0, unroll=8)

        def put(t, carry):
            for kk in range(TOP_K_IN_GROUP):
                tok_ref[dest_ref[kk * ntok + t]] = t
            return carry
        lax.fori_loop(0, ntok, put, 0, unroll=8)

    def weight_copies(ex, wslot):
        copies = []
        for src, dst in ((wg_ref, wgf_ref), (wu_ref, wuf_ref), (wd_ref, wdf_ref)):
            rows = src.shape[1] // W_CHUNKS
            for c in range(W_CHUNKS):
                rs = pl.ds(c * rows, rows)
                copies.append(pltpu.make_async_copy(src.at[ex, rs], dst.at[wslot, rs], wsem.at[wslot]))
        return copies

    def start_weights(ex, wslot):
        for cp in weight_copies(ex, wslot):
            cp.start(priority=1)

    @pl.when(e == 0)
    def _():
        for a in range(W_AHEAD):
            start_weights(a, a)

    @pl.when(e + W_AHEAD < pl.num_programs(0))
    def _():
        start_weights(e + W_AHEAD, (e + W_AHEAD) % W_SLOTS)

    @pl.when(e == 0)
    def _():
        build_slot_table()
        for a in range(GATHER_AHEAD):
            @pl.when(a < nact)
            def _():
                start_rows(a, a)

    wslot = e % W_SLOTS
    for cp in weight_copies(e, wslot):
        cp.wait()
    wgb_ref[...] = wgf_ref[wslot].astype(BF16)
    wub_ref[...] = wuf_ref[wslot].astype(BF16)
    wdb_ref[...] = wdf_ref[wslot].astype(BF16)

    def block(blk, carry):
        xslot = blk % X_SLOTS
        yslot = blk % 2

        @pl.when(blk + GATHER_AHEAD < nact)
        def _():
            start_rows(blk + GATHER_AHEAD, (blk + GATHER_AHEAD) % X_SLOTS)

        wait_rows(blk, xslot)
        xb = _load_token_major(xbuf_ref, xslot * blk_rows, MOE_BLOCK).astype(BF16)
        hg = jnp.dot(xb, wgb_ref[...], preferred_element_type=F32)
        hu = jnp.dot(xb, wub_ref[...], preferred_element_type=F32)
        h = (hg * jax.nn.sigmoid(hg)) * hu
        y = jnp.dot(h.astype(BF16), wdb_ref[...], preferred_element_type=F32)

        @pl.when(blk >= 2)
        def _():
            out_copy(blk - 2, yslot).wait()

        _store_token_major(ybuf_ref, yslot * blk_rows, y)
        out_copy(blk, yslot).start()
        return carry

    lax.fori_loop(first, first + nb, block, 0)

    @pl.when(e == pl.num_programs(0) - 1)
    def _():
        @pl.when(nact >= 2)
        def _():
            out_copy(nact - 2, nact % 2).wait()

        @pl.when(nact >= 1)
        def _():
            out_copy(nact - 1, (nact - 1) % 2).wait()

        ntotal = ys_ref.shape[0] // blk_rows
        ybuf_ref[0:blk_rows, :] = jnp.zeros((blk_rows, LANES), ybuf_ref.dtype)

        def fill_start(blk, carry):
            out_copy(blk, 0).start()
            return carry

        def fill_wait(blk, carry):
            out_copy(blk, 0).wait()
            return carry

        lax.fori_loop(nact_ref[0], ntotal, fill_start, 0)
        lax.fori_loop(nact_ref[0], ntotal, fill_wait, 0)


def _experts(first_blk, n_blk, nact, dest_flat, u2, w_gate, w_up, w_down, cap):
    d = D_MODEL
    grid_spec = pltpu.PrefetchScalarGridSpec(
        num_scalar_prefetch=4,
        grid=(N_EXPERTS,),
        in_specs=[pl.BlockSpec(memory_space=pl.ANY)] * 4,
        out_specs=pl.BlockSpec(memory_space=pl.ANY),
        scratch_shapes=[
            pltpu.SMEM((cap,), I32),
            pltpu.VMEM((X_SLOTS * MOE_BLOCK * ROW_CHUNKS, LANES), U32),
            pltpu.VMEM((2 * MOE_BLOCK * ROW_CHUNKS, LANES), U32),
            pltpu.VMEM((W_SLOTS, d, EXPERT_FF), F32),
            pltpu.VMEM((W_SLOTS, d, EXPERT_FF), F32),
            pltpu.VMEM((W_SLOTS, EXPERT_FF, d), F32),
            pltpu.VMEM((d, EXPERT_FF), BF16),
            pltpu.VMEM((d, EXPERT_FF), BF16),
            pltpu.VMEM((EXPERT_FF, d), BF16),
            pltpu.SemaphoreType.DMA((X_SLOTS,)),
            pltpu.SemaphoreType.DMA((2,)),
            pltpu.SemaphoreType.DMA((W_SLOTS,)),
        ],
    )
    return pl.pallas_call(
        _expert_kernel,
        grid_spec=grid_spec,
        out_shape=jax.ShapeDtypeStruct((cap * ROW_CHUNKS, LANES), U32),
        compiler_params=pltpu.CompilerParams(dimension_semantics=("arbitrary",), has_side_effects=True),
        name="experts",
    )(first_blk, n_blk, nact, dest_flat, u2, w_gate, w_up, w_down)


COMB_TOK = 128
COMB_AHEAD = 3
COMB_SLOTS = COMB_AHEAD + 1


def _combine_kernel(dest_ref, ys_ref, x1_ref, gw_ref, gf_ref, o_ref, buf_ref, sem):
    i = pl.program_id(0)
    n = pl.num_programs(0)
    ntok = n * COMB_TOK

    def buf_base(slot, kk):
        return (slot * TOP_K_IN_GROUP + kk) * (COMB_TOK * ROW_CHUNKS)

    def copy(step, slot, tl, kk):
        d = dest_ref[kk * ntok + step * COMB_TOK + tl]
        src = ys_ref.at[pl.ds(pl.multiple_of(d * ROW_CHUNKS, ROW_CHUNKS), ROW_CHUNKS)]
        dst = buf_ref.at[pl.ds(pl.multiple_of(buf_base(slot, kk) + tl * ROW_CHUNKS, ROW_CHUNKS), ROW_CHUNKS)]
        return pltpu.make_async_copy(src, dst, sem.at[slot])

    def start_all(step, slot):
        def body(tl, carry):
            for kk in range(TOP_K_IN_GROUP):
                copy(step, slot, tl, kk).start(priority=kk)
            return carry
        lax.fori_loop(0, COMB_TOK, body, 0, unroll=8)

    def wait_all(step, slot):
        def body(tl, carry):
            for kk in range(TOP_K_IN_GROUP):
                copy(step, slot, tl, kk).wait()
            return carry
        lax.fori_loop(0, COMB_TOK, body, 0, unroll=8)

    @pl.when(i == 0)
    def _():
        for a in range(COMB_AHEAD):
            start_all(a, a)

    @pl.when(i + COMB_AHEAD < n)
    def _():
        start_all(i + COMB_AHEAD, (i + COMB_AHEAD) % COMB_SLOTS)

    slot = i % COMB_SLOTS
    wait_all(i, slot)
    y0 = _load_token_major(buf_ref, buf_base(slot, 0), COMB_TOK)
    y1 = _load_token_major(buf_ref, buf_base(slot, 1), COMB_TOK)
    y = gw_ref[:, 0:1] * y0 + gw_ref[:, 1:2] * y1
    x = x1_ref[...] + y
    ms = jnp.mean(x * x, axis=-1, keepdims=True)
    o_ref[...] = x * lax.rsqrt(ms + EPS) * gf_ref[...]


def _combine(dest, ys, x1, gw_t, gf):
    t, d = x1.shape
    return pl.pallas_call(
        _combine_kernel,
        grid=(t // COMB_TOK,),
        in_specs=[
            pl.BlockSpec(memory_space=pltpu.SMEM),
            pl.BlockSpec(memory_space=pl.ANY),
            pl.BlockSpec((COMB_TOK, d), lambda i: (i, 0)),
            pl.BlockSpec((COMB_TOK, TOP_K_IN_GROUP), lambda i: (i, 0)),
            pl.BlockSpec((1, d), lambda i: (0, 0)),
        ],
        out_specs=pl.BlockSpec((COMB_TOK, d), lambda i: (i, 0)),
        out_shape=jax.ShapeDtypeStruct((t, d), F32),
        scratch_shapes=[
            pltpu.VMEM((COMB_SLOTS * TOP_K_IN_GROUP * COMB_TOK * ROW_CHUNKS, LANES), U32),
            pltpu.SemaphoreType.DMA((COMB_SLOTS,)),
        ],
        compiler_params=pltpu.CompilerParams(dimension_semantics=("arbitrary",)),
        name="combine",
    )(dest, ys, x1, gw_t, gf)


def kernel(x, norm1_g, w_in, conv_w, conv_b, lru_wa, lru_ba, lru_wx, lru_bx, lru_lambda, w_proj_attn,
           w_proj_rnn, w_out, rel_bias, norm2_g, w_group, b_group, w_expert_router, b_expert_router,
           w_gate, w_up, w_down, norm_f_g):
    batch, seq, d = x.shape
    tokens = batch * seq
    x2 = x.reshape(tokens, d)

    w = w_in[0]
    wt_all = jnp.concatenate(
        [w[:, OFF_Q:OFF_K], w[:, OFF_QI:OFF_KI], w[:, OFF_V:OFF_QI],
         jnp.zeros((d, WT_ROWS - WT_V - KV_WIDTH), w.dtype)], axis=1).T.astype(BF16)
    wt_wi = jnp.pad(w[:, OFF_WI:OFF_XR].T, ((0, LANES - IDX_HEADS), (0, 0))).astype(BF16)
    w_kk = jnp.concatenate(
        [w[:, OFF_K:OFF_V], w[:, OFF_KI:OFF_WI],
         jnp.zeros((d, KK_COLS - KV_WIDTH - IDX_DIM), w.dtype)], axis=1).astype(BF16)
    w_c = w[:, OFF_XR:].astype(BF16)
    g1 = norm1_g[0].reshape(1, d)
    ot, wit, okk = _attn_proj(x2, g1, wt_all, wt_wi, w_kk, 1024, 1280)
    pc = _norm_proj(x2, g1, w_c, F32, 1024, 1536)

    ks = np.arange(QB)[:, None]
    qs = np.arange(QB)[None, :]
    bkt = jnp.asarray(np.stack([_rel_bucket_np(qs - ks + QB * dd) for dd in range(2)]))
    attn = _dsa_attention(rel_bias, ot, wit, okk, bkt, batch, seq).reshape(tokens, ATTN_WIDTH)

    rnn = _rglru(pc, conv_w[0], conv_b[0].reshape(1, RNN_WIDTH), lru_wa[0].astype(BF16), lru_ba[0],
                 lru_wx[0].astype(BF16), lru_bx[0], lru_lambda[0].reshape(1, RNN_WIDTH), batch, seq)

    w_r = jnp.concatenate([w_group[0], w_expert_router[0],
                           jnp.zeros((d, LANES - N_GROUPS - N_EXPERTS), F32)], axis=1)
    w_rh = w_r.astype(BF16)
    w_rl = (w_r - w_rh.astype(F32)).astype(BF16)
    b_r = jnp.concatenate([b_group[0], b_expert_router[0],
                           jnp.zeros((LANES - N_GROUPS - N_EXPERTS,), F32)]).reshape(1, LANES)
    x1, u2, lgt = _merge(attn, rnn, pc, x2, w_proj_attn[0].astype(BF16), w_proj_rnn[0].astype(BF16),
                         w_out[0].astype(BF16), norm2_g[0].reshape(1, d), w_rh, w_rl, b_r)

    n_slots = tokens * TOP_K_IN_GROUP
    cap = -(-(n_slots + N_EXPERTS * (MOE_BLOCK - 1)) // MOE_BLOCK) * MOE_BLOCK
    dest, gw, meta = _route(lgt)

    dest_flat = dest.reshape(-1)
    ys = _experts(meta[0, :N_EXPERTS], meta[1, :N_EXPERTS], meta[2, :1], dest_flat, u2,
                  w_gate[0], w_up[0], w_down[0], cap)
    out = _combine(dest_flat, ys, x1, gw.T, norm_f_g.reshape(1, d))
    return out.reshape(batch, seq, d)
```

```python
import numpy as np
import jax
import jax.numpy as jnp
from jax import lax
from jax.experimental import pallas as pl
from jax.experimental.pallas import tpu as pltpu

D_MODEL = 2048
ATTN_HEADS = 8
KV_HEADS = 2
HEAD_DIM = 128
ATTN_WIDTH = ATTN_HEADS * HEAD_DIM
KV_WIDTH = KV_HEADS * HEAD_DIM
IDX_HEADS = 16
IDX_DIM = 64
TOPK_MAX = 256
RNN_WIDTH = 1024
RNN_BLOCKS = 8
RNN_BLOCK_W = RNN_WIDTH // RNN_BLOCKS
CONV_WIDTH = 4
LRU_C = 8.0
REL_BUCKETS = 32
REL_MAX_DIST = 128
N_GROUPS = 8
EXPERTS_PER_GROUP = 8
N_EXPERTS = N_GROUPS * EXPERTS_PER_GROUP
TOP_K_IN_GROUP = 2
EXPERT_FF = 512
MOE_BLOCK = 128
MOE_SHIFT = 7
EPS = 1e-6

LANES = 128
QB = 128
KT = 256
NEG_BIG = -1e30
INT_MIN = -(2 ** 31)
LOG2E = np.float32(np.log2(np.e))
F32 = jnp.float32
BF16 = jnp.bfloat16
I32 = jnp.int32

OFF_Q = 0
OFF_K = OFF_Q + ATTN_WIDTH
OFF_V = OFF_K + KV_WIDTH
OFF_QI = OFF_V + KV_WIDTH
OFF_KI = OFF_QI + IDX_HEADS * IDX_DIM
OFF_WI = OFF_KI + IDX_DIM
OFF_XR = OFF_WI + IDX_HEADS


def _rel_bucket_np(n):
    n = np.maximum(n, 0)
    max_exact = REL_BUCKETS // 2
    nf = np.maximum(n, 1).astype(np.float32)
    large = max_exact + (np.log(nf / np.float32(max_exact)) / np.float32(np.log(REL_MAX_DIST / max_exact))
                         * np.float32(REL_BUCKETS - max_exact)).astype(np.int32)
    large = np.minimum(large, REL_BUCKETS - 1)
    return np.where(n < max_exact, n, large).astype(np.int32)


def _norm_proj_kernel(x_ref, g_ref, w_ref, o_ref, u_ref):
    @pl.when(pl.program_id(1) == 0)
    def _():
        x = x_ref[...]
        ms = jnp.mean(x * x, axis=-1, keepdims=True)
        u_ref[...] = (x * lax.rsqrt(ms + EPS) * g_ref[...]).astype(BF16)

    o_ref[...] = jnp.dot(u_ref[...], w_ref[...], preferred_element_type=F32).astype(o_ref.dtype)


def _norm_proj(x2, g, w, out_dtype, tm, tn):
    m, d = x2.shape
    n = w.shape[1]
    return pl.pallas_call(
        _norm_proj_kernel,
        grid=(m // tm, n // tn),
        in_specs=[
            pl.BlockSpec((tm, d), lambda i, j: (i, 0)),
            pl.BlockSpec((1, d), lambda i, j: (0, 0)),
            pl.BlockSpec((d, tn), lambda i, j: (0, j)),
        ],
        out_specs=pl.BlockSpec((tm, tn), lambda i, j: (i, j)),
        out_shape=jax.ShapeDtypeStruct((m, n), out_dtype),
        scratch_shapes=[pltpu.VMEM((tm, d), BF16)],
        compiler_params=pltpu.CompilerParams(dimension_semantics=("arbitrary", "arbitrary")),
        name="norm_proj",
    )(x2, g, w)


WT_ROWS = 2560
WT_Q, WT_QI, WT_V = 0, ATTN_WIDTH, ATTN_WIDTH + IDX_HEADS * IDX_DIM
KK_COLS = 384


def _attn_proj_kernel(x_ref, g_ref, wt_ref, wtwi_ref, wkk_ref, ot_ref, wit_ref, okk_ref, u_ref):
    nt = (((1,), (1,)), ((), ()))

    @pl.when(pl.program_id(1) == 0)
    def _():
        x = x_ref[...]
        ms = jnp.mean(x * x, axis=-1, keepdims=True)
        u = (x * lax.rsqrt(ms + EPS) * g_ref[...]).astype(BF16)
        u_ref[...] = u
        okk_ref[...] = jnp.dot(u, wkk_ref[...], preferred_element_type=F32).astype(okk_ref.dtype)
        wit_ref[...] = lax.dot_general(wtwi_ref[...], u, nt, preferred_element_type=F32)

    ot_ref[...] = lax.dot_general(wt_ref[...], u_ref[...], nt, preferred_element_type=F32).astype(ot_ref.dtype)


def _attn_proj(x2, g, wt_all, wt_wi, w_kk, tm, tr):
    m, d = x2.shape
    return pl.pallas_call(
        _attn_proj_kernel,
        grid=(m // tm, WT_ROWS // tr),
        in_specs=[
            pl.BlockSpec((tm, d), lambda i, j: (i, 0)),
            pl.BlockSpec((1, d), lambda i, j: (0, 0)),
            pl.BlockSpec((tr, d), lambda i, j: (j, 0)),
            pl.BlockSpec((LANES, d), lambda i, j: (0, 0)),
            pl.BlockSpec((d, KK_COLS), lambda i, j: (0, 0)),
        ],
        out_specs=[
            pl.BlockSpec((tr, tm), lambda i, j: (j, i)),
            pl.BlockSpec((LANES, tm), lambda i, j: (0, i)),
            pl.BlockSpec((tm, KK_COLS), lambda i, j: (i, 0)),
        ],
        out_shape=[
            jax.ShapeDtypeStruct((WT_ROWS, m), BF16),
            jax.ShapeDtypeStruct((LANES, m), F32),
            jax.ShapeDtypeStruct((m, KK_COLS), BF16),
        ],
        scratch_shapes=[pltpu.VMEM((tm, d), BF16)],
        compiler_params=pltpu.CompilerParams(dimension_semantics=("arbitrary", "arbitrary")),
        name="attn_proj",
    )(x2, g, wt_all, wt_wi, w_kk)


TILES_PER_TRIP = 8


def _for_tiles_grouped(lo, hi, tile_fn):
    n = hi - lo
    full = n // TILES_PER_TRIP

    def group(i, carry):
        for u in range(TILES_PER_TRIP):
            tile_fn(lo + TILES_PER_TRIP * i + u)
        return carry

    lax.fori_loop(0, full, group, 0)

    done = lo + full * TILES_PER_TRIP
    size = TILES_PER_TRIP // 2
    while size >= 1:
        take = (n & size) != 0

        @pl.when(take)
        def _(done=done, size=size):
            for u in range(size):
                tile_fn(done + u)

        done = done + jnp.where(take, size, 0)
        size //= 2


def _dsa_kernel(relb_ref, q_ref, qi_ref, wi_ref, kk_ref, vt_ref, bkt_ref, o_ref,
                keys_ref, bias_ref, m_ref, l_ref, acc_ref, s_ref):
    b = pl.program_id(0)
    j = pl.program_id(1)

    @pl.when((b == 0) & (j == 0))
    def _():
        for d in range(2):
            bk = bkt_ref[d]
            for h in range(ATTN_HEADS):
                tile = jnp.zeros((QB, QB), F32)
                for bb in range(REL_BUCKETS):
                    tile = jnp.where(bk == bb, relb_ref[bb, h] * LOG2E, tile)
                bias_ref[d, h] = tile

    t0 = j * QB
    ntile = (j + 2) // 2
    q_pos = t0 + lax.broadcasted_iota(I32, (KT, QB), 1)
    row_iota = lax.broadcasted_iota(I32, (KT, QB), 0)

    w_scale = np.float32(IDX_HEADS ** -0.5) * np.float32(IDX_DIM ** -0.5)
    wrows = wi_ref[0:IDX_HEADS, :] * w_scale

    def score_tile(kt):
        r0 = pl.multiple_of(kt * KT, KT)
        ki_t = kk_ref[pl.ds(r0, KT), KV_WIDTH:KV_WIDTH + IDX_DIM]
        acc = jnp.zeros((KT, QB), F32)
        for c in range(4):
            qi_c = jnp.concatenate([qi_ref[(c * 4 + hh) * IDX_DIM:(c * 4 + hh + 1) * IDX_DIM, :]
                                    for hh in range(4)], axis=1)
            dots = jnp.dot(ki_t, qi_c, preferred_element_type=F32)
            for hh in range(4):
                h = c * 4 + hh
                acc = acc + jnp.maximum(dots[:, hh * QB:(hh + 1) * QB], 0.0) * wrows[h:h + 1, :]
        bits = pltpu.bitcast(acc, I32)
        key = bits ^ ((bits >> 31) & jnp.int32(0x7FFFFFFF))
        key = jnp.where(r0 + row_iota <= q_pos, key, jnp.int32(INT_MIN))
        keys_ref[pl.ds(r0, KT), :] = key

    _for_tiles_grouped(0, ntile, score_tile)

    n_acc = 4

    def count_ge(cand):
        cand8 = jnp.broadcast_to(cand, (8, QB))

        def body(kt, accs):
            r0 = pl.multiple_of(kt * KT, KT)
            accs = list(accs)
            for v in range(KT // 8):
                blk = keys_ref[pl.ds(r0 + v * 8, 8), :]
                accs[v % n_acc] = accs[v % n_acc] + jnp.where(blk >= cand8, jnp.int32(1), jnp.int32(0))
            return tuple(accs)

        def body_pair(i, accs):
            return body(2 * i + 1, body(2 * i, accs))

        accs = lax.fori_loop(0, ntile // 2, body_pair, (jnp.zeros((8, QB), I32),) * n_acc)
        accs = lax.fori_loop(2 * (ntile // 2), ntile, body, accs)
        acc8 = (accs[0] + accs[1]) + (accs[2] + accs[3])
        return jnp.sum(acc8.astype(F32), axis=0, keepdims=True).astype(I32)

    c0 = count_ge(jnp.zeros((1, QB), I32))
    ok0 = c0 >= TOPK_MAX
    thr0 = jnp.where(ok0, jnp.int32(0), jnp.int32(INT_MIN))
    cnt0 = jnp.where(ok0, c0, jnp.int32(0))

    def bit_step(i, carry):
        thr, cnt = carry
        cand = thr | (jnp.int32(1) << (30 - i))
        c = count_ge(cand)
        ok = c >= TOPK_MAX
        return jnp.where(ok, cand, thr), jnp.where(ok, c, cnt)

    thr, cnt = lax.fori_loop(0, 31, bit_step, (thr0, cnt0))
    thr = jnp.maximum(thr, jnp.int32(INT_MIN + 1))

    pos_bits = (keys_ref.shape[0] - 1).bit_length()

    @pl.when(jnp.max(cnt.astype(F32)) > TOPK_MAX)
    def _():
        tied = cnt > TOPK_MAX
        need = TOPK_MAX - count_ge(thr + 1)
        thr8 = jnp.broadcast_to(thr, (8, QB))
        sub8 = lax.broadcasted_iota(I32, (8, QB), 0)

        def count_eq_before(limit):
            lim8 = jnp.broadcast_to(limit, (8, QB))

            def body(v, acc):
                r = pl.multiple_of(v * 8, 8)
                hit = jnp.logical_and(keys_ref[pl.ds(r, 8), :] == thr8, r + sub8 < lim8)
                return acc + jnp.where(hit, jnp.int32(1), jnp.int32(0))

            acc = lax.fori_loop(0, ntile * (KT // 8), body, jnp.zeros((8, QB), I32))
            return jnp.sum(acc.astype(F32), axis=0, keepdims=True).astype(I32)

        def pos_step(i, last):
            cand = last | (jnp.int32(1) << (pos_bits - 1 - i))
            return jnp.where(count_eq_before(cand) < need, cand, last)

        last = lax.fori_loop(0, pos_bits, pos_step, jnp.zeros((1, QB), I32))
        last8 = jnp.broadcast_to(jnp.where(tied, last, jnp.int32(2 ** pos_bits)), (8, QB))

        def demote(v, carry):
            r = pl.multiple_of(v * 8, 8)
            blk = keys_ref[pl.ds(r, 8), :]
            drop = jnp.logical_and(blk == thr8, r + sub8 > last8)
            keys_ref[pl.ds(r, 8), :] = jnp.where(drop, thr8 - 1, blk)
            return carry

        lax.fori_loop(0, ntile * (KT // 8), demote, 0)

    qk_scale = np.float32(HEAD_DIM ** -0.5) * LOG2E
    rep = ATTN_HEADS // KV_HEADS
    nfar = jnp.maximum((j - 1) // 2, 0)

    gw = rep * QB
    m_ref[...] = jnp.full(m_ref.shape, NEG_BIG, F32)
    l_ref[...] = jnp.zeros(l_ref.shape, F32)
    acc_ref[...] = jnp.zeros(acc_ref.shape, F32)

    def logits(kt, g, near):
        r0 = pl.multiple_of(kt * KT, KT)
        k_t = kk_ref[pl.ds(r0, KT), g * HEAD_DIM:(g + 1) * HEAD_DIM]
        q_g = jnp.concatenate([q_ref[(g * rep + r) * HEAD_DIM:(g * rep + r + 1) * HEAD_DIM, :]
                               for r in range(rep)], axis=1)
        lg = jnp.dot(k_t, q_g, preferred_element_type=F32) * qk_scale
        sel = keys_ref[pl.ds(r0, KT), :] >= thr
        cols = []
        for r in range(rep):
            h = g * rep + r
            far_bias = relb_ref[REL_BUCKETS - 1, h] * LOG2E
            blk = lg[:, r * QB:(r + 1) * QB]
            if near:
                subs = []
                for sub in range(KT // QB):
                    dblk = j - (kt * (KT // QB) + sub)
                    bias = jnp.where(dblk == 0, bias_ref[0, h],
                                     jnp.where(dblk == 1, bias_ref[1, h], far_bias))
                    subs.append(blk[sub * QB:(sub + 1) * QB, :] + bias)
                blk = jnp.concatenate(subs, axis=0)
            else:
                blk = blk + far_bias
            cols.append(jnp.where(sel, blk, NEG_BIG))
        s = jnp.concatenate(cols, axis=1)
        s_ref[pl.ds(r0, KT), g * gw:(g + 1) * gw] = s
        m_ref[g] = jnp.maximum(m_ref[g], jnp.max(s.reshape(KT // 8, 8, gw), axis=0))

    def far_tile(kt):
        for g in range(KV_HEADS):
            logits(kt, g, False)

    def near_body(kt, carry):
        for g in range(KV_HEADS):
            logits(kt, g, True)
        return carry

    _for_tiles_grouped(0, nfar, far_tile)
    lax.fori_loop(nfar, ntile, near_body, 0)

    m_fin = [jnp.max(m_ref[g], axis=0, keepdims=True) for g in range(KV_HEADS)]

    def weigh(kt):
        r0 = pl.multiple_of(kt * KT, KT)
        for g in range(KV_HEADS):
            p = jnp.exp2(s_ref[pl.ds(r0, KT), g * gw:(g + 1) * gw] - m_fin[g])
            l_ref[g] = l_ref[g] + jnp.sum(p.reshape(KT // 8, 8, gw), axis=0)
            v_t = vt_ref[g * HEAD_DIM:(g + 1) * HEAD_DIM, pl.ds(r0, KT)]
            acc_ref[g] = acc_ref[g] + jnp.dot(v_t, p.astype(BF16), preferred_element_type=F32)

    _for_tiles_grouped(0, ntile, weigh)

    for g in range(KV_HEADS):
        out_t = acc_ref[g] / jnp.sum(l_ref[g], axis=0, keepdims=True)
        for r in range(rep):
            h = g * rep + r
            o_ref[:, h * HEAD_DIM:(h + 1) * HEAD_DIM] = out_t[:, r * QB:(r + 1) * QB].T.astype(o_ref.dtype)


def _dsa_attention(rel_bias, ot, wit, okk, bkt, batch, seq):
    nblk = seq // QB
    rep = ATTN_HEADS // KV_HEADS
    return pl.pallas_call(
        _dsa_kernel,
        grid=(batch, nblk),
        in_specs=[
            pl.BlockSpec(memory_space=pltpu.SMEM),
            pl.BlockSpec((ATTN_WIDTH, QB), lambda b, j: (WT_Q // ATTN_WIDTH, b * nblk + j)),
            pl.BlockSpec((IDX_HEADS * IDX_DIM, QB), lambda b, j: (WT_QI // (IDX_HEADS * IDX_DIM), b * nblk + j)),
            pl.BlockSpec((LANES, QB), lambda b, j: (0, b * nblk + j)),
            pl.BlockSpec((seq, KK_COLS), lambda b, j: (b, 0)),
            pl.BlockSpec((KV_WIDTH, seq), lambda b, j: (WT_V // KV_WIDTH, b)),
            pl.BlockSpec((2, QB, QB), lambda b, j: (0, 0, 0)),
        ],
        out_specs=pl.BlockSpec((None, QB, ATTN_WIDTH), lambda b, j: (b, j, 0)),
        out_shape=jax.ShapeDtypeStruct((batch, seq, ATTN_WIDTH), BF16),
        scratch_shapes=[
            pltpu.VMEM((seq, QB), I32),
            pltpu.VMEM((2, ATTN_HEADS, QB, QB), F32),
            pltpu.VMEM((KV_HEADS, 8, rep * QB), F32),
            pltpu.VMEM((KV_HEADS, 8, rep * QB), F32),
            pltpu.VMEM((KV_HEADS, HEAD_DIM, rep * QB), F32),
            pltpu.VMEM((seq, ATTN_HEADS * QB), F32),
        ],
        compiler_params=pltpu.CompilerParams(dimension_semantics=("arbitrary", "arbitrary")),
        name="dsa_attention",
    )(rel_bias, ot, ot, wit, okk, ot, bkt)


RNN_TS = 256


def _gelu_tanh(x):
    c = np.float32(np.sqrt(2.0 / np.pi))
    return x * (0.5 * (1.0 + jnp.tanh(c * (x + np.float32(0.044715) * (x * x * x)))))


def _softplus(z):
    return jnp.maximum(z, 0.0) + jnp.log1p(jnp.exp(-jnp.abs(z)))


def _rglru_kernel(xr_ref, gate_ref, cw_ref, cb_ref, wa_ref, ba_ref, wx_ref, bx_ref, lam_ref, o_ref,
                  xext_ref, h_ref):
    i = pl.program_id(1)
    ts = RNN_TS

    @pl.when(i == 0)
    def _():
        xext_ref[0:8, :] = jnp.zeros((8, RNN_WIDTH), F32)
        h_ref[...] = jnp.zeros((1, RNN_WIDTH), F32)

    @pl.when(i > 0)
    def _():
        xext_ref[0:8, :] = xext_ref[ts:ts + 8, :]

    xext_ref[8:8 + ts, :] = xr_ref[...]

    row = lax.broadcasted_iota(I32, (ts, RNN_BLOCK_W), 0)
    for n in range(RNN_BLOCKS):
        cs = slice(n * RNN_BLOCK_W, (n + 1) * RNN_BLOCK_W)
        xc = cb_ref[:, cs]
        for jj in range(CONV_WIDTH):
            off = 8 - (CONV_WIDTH - 1) + jj
            xc = xc + xext_ref[off:off + ts, cs] * cw_ref[jj:jj + 1, cs]
        xcb = xc.astype(BF16)
        r = jax.nn.sigmoid(jnp.dot(xcb, wa_ref[n], preferred_element_type=F32) + ba_ref[n:n + 1, :])
        gi = jax.nn.sigmoid(jnp.dot(xcb, wx_ref[n], preferred_element_type=F32) + bx_ref[n:n + 1, :])
        log_a = (-LRU_C * r) * _softplus(-lam_ref[:, cs])
        a = jnp.exp(log_a)
        mult = jnp.sqrt(1.0 - jnp.exp(2.0 * log_a))
        bt = mult * (gi * xc)
        k = 1
        while k < ts:
            a_s = pltpu.roll(a, k, 0)
            b_s = pltpu.roll(bt, k, 0)
            keep = row >= k
            bt = jnp.where(keep, a * b_s + bt, bt)
            a = jnp.where(keep, a * a_s, a)
            k *= 2
        h = bt + a * h_ref[:, cs]
        h_ref[:, cs] = h[ts - 1:ts, :]
        o_ref[:, cs] = (h * _gelu_tanh(gate_ref[:, cs])).astype(o_ref.dtype)


def _rglru(pc, conv_w, conv_b, wa, ba, wx, bx, lam, batch, seq):
    nts = seq // RNN_TS
    full = lambda shape: pl.BlockSpec(shape, lambda b, i: (0,) * len(shape))
    return pl.pallas_call(
        _rglru_kernel,
        grid=(batch, nts),
        in_specs=[
            pl.BlockSpec((RNN_TS, RNN_WIDTH), lambda b, i: (b * nts + i, 0)),
            pl.BlockSpec((RNN_TS, RNN_WIDTH), lambda b, i: (b * nts + i, 1)),
            full((CONV_WIDTH, RNN_WIDTH)),
            full((1, RNN_WIDTH)),
            full((RNN_BLOCKS, RNN_BLOCK_W, RNN_BLOCK_W)),
            full((RNN_BLOCKS, RNN_BLOCK_W)),
            full((RNN_BLOCKS, RNN_BLOCK_W, RNN_BLOCK_W)),
            full((RNN_BLOCKS, RNN_BLOCK_W)),
            full((1, RNN_WIDTH)),
        ],
        out_specs=pl.BlockSpec((RNN_TS, RNN_WIDTH), lambda b, i: (b * nts + i, 0)),
        out_shape=jax.ShapeDtypeStruct((batch * seq, RNN_WIDTH), BF16),
        scratch_shapes=[pltpu.VMEM((RNN_TS + 8, RNN_WIDTH), F32), pltpu.VMEM((1, RNN_WIDTH), F32)],
        compiler_params=pltpu.CompilerParams(dimension_semantics=("arbitrary", "arbitrary")),
        name="rglru",
    )(pc, pc, conv_w, conv_b, wa, ba, wx, bx, lam)


MERGE_TM = 256
ROW_CHUNKS = D_MODEL // (2 * LANES)
U32 = jnp.uint32


def _store_token_major(ref, base, val):
    n = val.shape[0]
    for c in range(ROW_CHUNKS):
        hi = pltpu.bitcast(val[:, 2 * c * LANES:(2 * c + 1) * LANES].astype(BF16).astype(F32), U32)
        lo = pltpu.bitcast(val[:, (2 * c + 1) * LANES:(2 * c + 2) * LANES].astype(BF16).astype(F32), U32)
        ref[pl.ds(base + c, n, stride=ROW_CHUNKS), :] = hi | (lo >> 16)


def _load_token_major(ref, base, n):
    parts = []
    for c in range(ROW_CHUNKS):
        w = ref[pl.ds(base + c, n, stride=ROW_CHUNKS), :]
        parts.append(pltpu.bitcast(w & U32(0xFFFF0000), F32))
        parts.append(pltpu.bitcast(w << 16, F32))
    return jnp.concatenate(parts, axis=1)


def _merge_kernel(attn_ref, rnn_ref, ga_ref, gr_ref, x_ref, wpa_ref, wpr_ref, wo_ref, g2_ref,
                  wrh_ref, wrl_ref, br_ref, x1_ref, u2_ref, lgt_ref):
    pa = jnp.dot(attn_ref[...], wpa_ref[...], preferred_element_type=F32)
    pr = jnp.dot(rnn_ref[...], wpr_ref[...], preferred_element_type=F32)
    merged = jax.nn.sigmoid(ga_ref[...]) * pa + jax.nn.sigmoid(gr_ref[...]) * pr
    x1 = x_ref[...] + jnp.dot(merged.astype(BF16), wo_ref[...], preferred_element_type=F32)
    x1_ref[...] = x1
    ms = jnp.mean(x1 * x1, axis=-1, keepdims=True)
    u2 = x1 * lax.rsqrt(ms + EPS) * g2_ref[...]
    _store_token_major(u2_ref, 0, u2)
    hi = u2.astype(BF16)
    lo = (u2 - hi.astype(F32)).astype(BF16)
    lg = (jnp.dot(hi, wrh_ref[...], preferred_element_type=F32)
          + jnp.dot(lo, wrh_ref[...], preferred_element_type=F32)
          + jnp.dot(hi, wrl_ref[...], preferred_element_type=F32)) + br_ref[...]
    lgt_ref[...] = lg.T


def _merge(attn, rnn, pc, x2, wpa, wpr, wo, g2, wrh, wrl, br):
    m = x2.shape[0]
    tm = MERGE_TM
    const = lambda shape: pl.BlockSpec(shape, lambda i: (0,) * len(shape), pipeline_mode=pl.Buffered(1))
    return pl.pallas_call(
        _merge_kernel,
        grid=(m // tm,),
        in_specs=[
            pl.BlockSpec((tm, ATTN_WIDTH), lambda i: (i, 0)),
            pl.BlockSpec((tm, RNN_WIDTH), lambda i: (i, 0)),
            pl.BlockSpec((tm, D_MODEL), lambda i: (i, 1)),
            pl.BlockSpec((tm, D_MODEL), lambda i: (i, 2)),
            pl.BlockSpec((tm, D_MODEL), lambda i: (i, 0)),
            const((ATTN_WIDTH, D_MODEL)),
            const((RNN_WIDTH, D_MODEL)),
            const((D_MODEL, D_MODEL)),
            const((1, D_MODEL)),
            const((D_MODEL, LANES)),
            const((D_MODEL, LANES)),
            const((1, LANES)),
        ],
        out_specs=[
            pl.BlockSpec((tm, D_MODEL), lambda i: (i, 0)),
            pl.BlockSpec((tm * ROW_CHUNKS, LANES), lambda i: (i, 0)),
            pl.BlockSpec((LANES, tm), lambda i: (0, i)),
        ],
        out_shape=[
            jax.ShapeDtypeStruct((m, D_MODEL), F32),
            jax.ShapeDtypeStruct((m * ROW_CHUNKS, LANES), U32),
            jax.ShapeDtypeStruct((LANES, m), F32),
        ],
        compiler_params=pltpu.CompilerParams(dimension_semantics=("arbitrary",)),
        name="merge_outproj",
    )(attn, rnn, pc, pc, x2, wpa, wpr, wo, g2, wrh, wrl, br)


ROUTE_CHUNK = 256


def _first_index_of_max(v, ridx, n):
    vmax = jnp.max(v, axis=0, keepdims=True)
    idx = jnp.min(jnp.where(v == vmax, ridx, jnp.int32(n)).astype(F32), axis=0, keepdims=True)
    return vmax, idx.astype(I32)


def _route_kernel(lgt_ref, dest_ref, gw_ref, meta_ref, cum_ref):
    t = lgt_ref.shape[1]
    eg = EXPERTS_PER_GROUP
    ridx8 = lax.broadcasted_iota(I32, (eg, t), 0)
    gl = lgt_ref[0:N_GROUPS, :]
    gmax, g_sel = _first_index_of_max(gl, ridx8, N_GROUPS)
    p_sel = 1.0 / jnp.sum(jnp.exp(gl - gmax), axis=0, keepdims=True)
    el = lgt_ref[N_GROUPS:N_GROUPS + eg, :]
    for g in range(1, N_GROUPS):
        el = jnp.where(g_sel == g, lgt_ref[N_GROUPS + g * eg:N_GROUPS + (g + 1) * eg, :], el)
    v0, i0 = _first_index_of_max(el, ridx8, eg)
    el1 = jnp.where(ridx8 == i0, -jnp.inf, el)
    v1, i1 = _first_index_of_max(el1, ridx8, eg)
    e1 = jnp.exp(v1 - v0)
    den = 1.0 + e1
    gw_ref[0:1, :] = p_sel * (1.0 / den)
    gw_ref[1:2, :] = p_sel * (e1 / den)
    ex0 = g_sel * eg + i0
    ex1 = g_sel * eg + i1

    eidx = lax.broadcasted_iota(I32, (N_EXPERTS, ROUTE_CHUNK), 0)
    ui = lax.broadcasted_iota(I32, (ROUTE_CHUNK, ROUTE_CHUNK), 0)
    uj = lax.broadcasted_iota(I32, (ROUTE_CHUNK, ROUTE_CHUNK), 1)
    upper = jnp.where(ui < uj, 1.0, 0.0).astype(BF16)
    run = jnp.zeros((N_EXPERTS, 1), F32)
    for c in range(t // ROUTE_CHUNK):
        cs = slice(c * ROUTE_CHUNK, (c + 1) * ROUTE_CHUNK)
        hit = jnp.where(eidx == ex0[:, cs], 1.0, jnp.where(eidx == ex1[:, cs], 1.0, 0.0))
        cum_ref[:, cs] = jnp.dot(hit.astype(BF16), upper, preferred_element_type=F32) + run
        run = run + jnp.sum(hit, axis=1, keepdims=True)

    counts = run.astype(I32)
    padded = ((counts + (MOE_BLOCK - 1)) >> MOE_SHIFT) << MOE_SHIFT
    pe = jnp.broadcast_to(padded, (N_EXPERTS, LANES))
    erow = lax.broadcasted_iota(I32, (N_EXPERTS, LANES), 0)
    k = 1
    while k < N_EXPERTS:
        pe = pe + jnp.where(erow >= k, pltpu.roll(pe, k, 0), 0)
        k *= 2
    pends = pe[:, 0:1]
    pstarts = pends - padded

    eidx_t = lax.broadcasted_iota(I32, (N_EXPERTS, t), 0)
    slot = cum_ref[...] + pstarts.astype(F32)
    dest_ref[0:1, :] = jnp.sum(jnp.where(eidx_t == ex0, slot, 0.0), axis=0, keepdims=True).astype(I32)
    dest_ref[1:2, :] = jnp.sum(jnp.where(eidx_t == ex1, slot, 0.0), axis=0, keepdims=True).astype(I32)

    nb = meta_ref.shape[1]
    own = lax.broadcasted_iota(I32, (N_EXPERTS, nb), 0) == lax.broadcasted_iota(I32, (N_EXPERTS, nb), 1)
    first_blk = (pstarts >> MOE_SHIFT).astype(F32)
    n_blk = (padded >> MOE_SHIFT).astype(F32)
    meta_ref[0:1, :] = jnp.sum(jnp.where(own, first_blk, 0.0), axis=0, keepdims=True).astype(I32)
    meta_ref[1:2, :] = jnp.sum(jnp.where(own, n_blk, 0.0), axis=0, keepdims=True).astype(I32)
    meta_ref[2:3, :] = jnp.broadcast_to(pends[N_EXPERTS - 1:N_EXPERTS, :] >> MOE_SHIFT, (1, nb))


def _route(lgt):
    t = lgt.shape[1]
    return pl.pallas_call(
        _route_kernel,
        out_shape=[
            jax.ShapeDtypeStruct((2, t), I32),
            jax.ShapeDtypeStruct((2, t), F32),
            jax.ShapeDtypeStruct((3, LANES), I32),
        ],
        scratch_shapes=[pltpu.VMEM((N_EXPERTS, t), F32)],
        name="route",
    )(lgt)


W_CHUNKS = 8
GATHER_AHEAD = 3
X_SLOTS = GATHER_AHEAD + 1
W_AHEAD = 2
W_SLOTS = W_AHEAD + 1


def _expert_kernel(first_ref, nblk_ref, nact_ref, dest_ref, u_ref, wg_ref, wu_ref, wd_ref, ys_ref,
                   tok_ref, xbuf_ref, ybuf_ref, wgf_ref, wuf_ref, wdf_ref, wgb_ref, wub_ref, wdb_ref,
                   gsem, osem, wsem):
    e = pl.program_id(0)
    first = first_ref[e]
    nb = nblk_ref[e]
    nact = nact_ref[0]
    ntok = u_ref.shape[0] // ROW_CHUNKS
    nslot = tok_ref.shape[0]
    blk_rows = MOE_BLOCK * ROW_CHUNKS

    def row_copy(blk, slot, r):
        tok = tok_ref[blk * MOE_BLOCK + r]
        src = u_ref.at[pl.ds(pl.multiple_of(tok * ROW_CHUNKS, ROW_CHUNKS), ROW_CHUNKS)]
        dst = xbuf_ref.at[pl.ds(pl.multiple_of(slot * blk_rows + r * ROW_CHUNKS, ROW_CHUNKS), ROW_CHUNKS)]
        return pltpu.make_async_copy(src, dst, gsem.at[slot])

    def start_rows(blk, slot):
        def body(r, carry):
            row_copy(blk, slot, r).start()
            return carry
        lax.fori_loop(0, MOE_BLOCK, body, 0, unroll=8)

    def wait_rows(blk, slot):
        def body(r, carry):
            row_copy(blk, slot, r).wait()
            return carry
        lax.fori_loop(0, MOE_BLOCK, body, 0, unroll=8)

    def out_copy(blk, slot):
        src = ybuf_ref.at[pl.ds(pl.multiple_of(slot * blk_rows, blk_rows), blk_rows)]
        dst = ys_ref.at[pl.ds(pl.multiple_of(blk * blk_rows, blk_rows), blk_rows)]
        return pltpu.make_async_copy(src, dst, osem.at[slot])

    def build_slot_table():
        def clear(p, carry):
            tok_ref[p] = 0
            return carry
        lax.fori_loop(0, nslot, clear, 0, unroll=8)

        def put(t, carry):
            for kk in range(TOP_K_IN_GROUP):
                tok_ref[dest_ref[kk * ntok + t]] = t
            return carry
        lax.fori_loop(0, ntok, put, 0, unroll=8)

    def weight_copies(ex, wslot):
        copies = []
        for src, dst in ((wg_ref, wgf_ref), (wu_ref, wuf_ref), (wd_ref, wdf_ref)):
            rows = src.shape[1] // W_CHUNKS
            for c in range(W_CHUNKS):
                rs = pl.ds(c * rows, rows)
                copies.append(pltpu.make_async_copy(src.at[ex, rs], dst.at[wslot, rs], wsem.at[wslot]))
        return copies

    def start_weights(ex, wslot):
        for cp in weight_copies(ex, wslot):
            cp.start(priority=1)

    @pl.when(e == 0)
    def _():
        for a in range(W_AHEAD):
            start_weights(a, a)

    @pl.when(e + W_AHEAD < pl.num_programs(0))
    def _():
        start_weights(e + W_AHEAD, (e + W_AHEAD) % W_SLOTS)

    @pl.when(e == 0)
    def _():
        build_slot_table()
        for a in range(GATHER_AHEAD):
            @pl.when(a < nact)
            def _():
                start_rows(a, a)

    wslot = e % W_SLOTS
    for cp in weight_copies(e, wslot):
        cp.wait()
    wgb_ref[...] = wgf_ref[wslot].astype(BF16)
    wub_ref[...] = wuf_ref[wslot].astype(BF16)
    wdb_ref[...] = wdf_ref[wslot].astype(BF16)

    def block(blk, carry):
        xslot = blk % X_SLOTS
        yslot = blk % 2

        @pl.when(blk + GATHER_AHEAD < nact)
        def _():
            start_rows(blk + GATHER_AHEAD, (blk + GATHER_AHEAD) % X_SLOTS)

        wait_rows(blk, xslot)
        xb = _load_token_major(xbuf_ref, xslot * blk_rows, MOE_BLOCK).astype(BF16)
        hg = jnp.dot(xb, wgb_ref[...], preferred_element_type=F32)
        hu = jnp.dot(xb, wub_ref[...], preferred_element_type=F32)
        h = (hg * jax.nn.sigmoid(hg)) * hu
        y = jnp.dot(h.astype(BF16), wdb_ref[...], preferred_element_type=F32)

        @pl.when(blk >= 2)
        def _():
            out_copy(blk - 2, yslot).wait()

        _store_token_major(ybuf_ref, yslot * blk_rows, y)
        out_copy(blk, yslot).start()
        return carry

    lax.fori_loop(first, first + nb, block, 0)

    @pl.when(e == pl.num_programs(0) - 1)
    def _():
        @pl.when(nact >= 2)
        def _():
            out_copy(nact - 2, nact % 2).wait()

        @pl.when(nact >= 1)
        def _():
            out_copy(nact - 1, (nact - 1) % 2).wait()

        ntotal = ys_ref.shape[0] // blk_rows
        ybuf_ref[0:blk_rows, :] = jnp.zeros((blk_rows, LANES), ybuf_ref.dtype)

        def fill_start(blk, carry):
            out_copy(blk, 0).start()
            return carry

        def fill_wait(blk, carry):
            out_copy(blk, 0).wait()
            return carry

        lax.fori_loop(nact_ref[0], ntotal, fill_start, 0)
        lax.fori_loop(nact_ref[0], ntotal, fill_wait, 0)


def _experts(first_blk, n_blk, nact, dest_flat, u2, w_gate, w_up, w_down, cap):
    d = D_MODEL
    grid_spec = pltpu.PrefetchScalarGridSpec(
        num_scalar_prefetch=4,
        grid=(N_EXPERTS,),
        in_specs=[pl.BlockSpec(memory_space=pl.ANY)] * 4,
        out_specs=pl.BlockSpec(memory_space=pl.ANY),
        scratch_shapes=[
            pltpu.SMEM((cap,), I32),
            pltpu.VMEM((X_SLOTS * MOE_BLOCK * ROW_CHUNKS, LANES), U32),
            pltpu.VMEM((2 * MOE_BLOCK * ROW_CHUNKS, LANES), U32),
            pltpu.VMEM((W_SLOTS, d, EXPERT_FF), F32),
            pltpu.VMEM((W_SLOTS, d, EXPERT_FF), F32),
            pltpu.VMEM((W_SLOTS, EXPERT_FF, d), F32),
            pltpu.VMEM((d, EXPERT_FF), BF16),
            pltpu.VMEM((d, EXPERT_FF), BF16),
            pltpu.VMEM((EXPERT_FF, d), BF16),
            pltpu.SemaphoreType.DMA((X_SLOTS,)),
            pltpu.SemaphoreType.DMA((2,)),
            pltpu.SemaphoreType.DMA((W_SLOTS,)),
        ],
    )
    return pl.pallas_call(
        _expert_kernel,
        grid_spec=grid_spec,
        out_shape=jax.ShapeDtypeStruct((cap * ROW_CHUNKS, LANES), U32),
        compiler_params=pltpu.CompilerParams(dimension_semantics=("arbitrary",), has_side_effects=True),
        name="experts",
    )(first_blk, n_blk, nact, dest_flat, u2, w_gate, w_up, w_down)


COMB_TOK = 128
COMB_AHEAD = 3
COMB_SLOTS = COMB_AHEAD + 1


def _combine_kernel(dest_ref, ys_ref, x1_ref, gw_ref, gf_ref, o_ref, buf_ref, sem):
    i = pl.program_id(0)
    n = pl.num_programs(0)
    ntok = n * COMB_TOK

    def buf_base(slot, kk):
        return (slot * TOP_K_IN_GROUP + kk) * (COMB_TOK * ROW_CHUNKS)

    def copy(step, slot, tl, kk):
        d = dest_ref[kk * ntok + step * COMB_TOK + tl]
        src = ys_ref.at[pl.ds(pl.multiple_of(d * ROW_CHUNKS, ROW_CHUNKS), ROW_CHUNKS)]
        dst = buf_ref.at[pl.ds(pl.multiple_of(buf_base(slot, kk) + tl * ROW_CHUNKS, ROW_CHUNKS), ROW_CHUNKS)]
        return pltpu.make_async_copy(src, dst, sem.at[slot])

    def start_all(step, slot):
        def body(tl, carry):
            for kk in range(TOP_K_IN_GROUP):
                copy(step, slot, tl, kk).start(priority=kk)
            return carry
        lax.fori_loop(0, COMB_TOK, body, 0, unroll=8)

    def wait_all(step, slot):
        def body(tl, carry):
            for kk in range(TOP_K_IN_GROUP):
                copy(step, slot, tl, kk).wait()
            return carry
        lax.fori_loop(0, COMB_TOK, body, 0, unroll=8)

    @pl.when(i == 0)
    def _():
        for a in range(COMB_AHEAD):
            start_all(a, a)

    @pl.when(i + COMB_AHEAD < n)
    def _():
        start_all(i + COMB_AHEAD, (i + COMB_AHEAD) % COMB_SLOTS)

    slot = i % COMB_SLOTS
    wait_all(i, slot)
    y0 = _load_token_major(buf_ref, buf_base(slot, 0), COMB_TOK)
    y1 = _load_token_major(buf_ref, buf_base(slot, 1), COMB_TOK)
    y = gw_ref[:, 0:1] * y0 + gw_ref[:, 1:2] * y1
    x = x1_ref[...] + y
    ms = jnp.mean(x * x, axis=-1, keepdims=True)
    o_ref[...] = x * lax.rsqrt(ms + EPS) * gf_ref[...]


def _combine(dest, ys, x1, gw_t, gf):
    t, d = x1.shape
    return pl.pallas_call(
        _combine_kernel,
        grid=(t // COMB_TOK,),
        in_specs=[
            pl.BlockSpec(memory_space=pltpu.SMEM),
            pl.BlockSpec(memory_space=pl.ANY),
            pl.BlockSpec((COMB_TOK, d), lambda i: (i, 0)),
            pl.BlockSpec((COMB_TOK, TOP_K_IN_GROUP), lambda i: (i, 0)),
            pl.BlockSpec((1, d), lambda i: (0, 0)),
        ],
        out_specs=pl.BlockSpec((COMB_TOK, d), lambda i: (i, 0)),
        out_shape=jax.ShapeDtypeStruct((t, d), F32),
        scratch_shapes=[
            pltpu.VMEM((COMB_SLOTS * TOP_K_IN_GROUP * COMB_TOK * ROW_CHUNKS, LANES), U32),
            pltpu.SemaphoreType.DMA((COMB_SLOTS,)),
        ],
        compiler_params=pltpu.CompilerParams(dimension_semantics=("arbitrary",)),
        name="combine",
    )(dest, ys, x1, gw_t, gf)


def kernel(x, norm1_g, w_in, conv_w, conv_b, lru_wa, lru_ba, lru_wx, lru_bx, lru_lambda, w_proj_attn,
           w_proj_rnn, w_out, rel_bias, norm2_g, w_group, b_group, w_expert_router, b_expert_router,
           w_gate, w_up, w_down, norm_f_g):
    batch, seq, d = x.shape
    tokens = batch * seq
    x2 = x.reshape(tokens, d)

    w = w_in[0]
    wt_all = jnp.concatenate(
        [w[:, OFF_Q:OFF_K], w[:, OFF_QI:OFF_KI], w[:, OFF_V:OFF_QI],
         jnp.zeros((d, WT_ROWS - WT_V - KV_WIDTH), w.dtype)], axis=1).T.astype(BF16)
    wt_wi = jnp.pad(w[:, OFF_WI:OFF_XR].T, ((0, LANES - IDX_HEADS), (0, 0))).astype(BF16)
    w_kk = jnp.concatenate(
        [w[:, OFF_K:OFF_V], w[:, OFF_KI:OFF_WI],
         jnp.zeros((d, KK_COLS - KV_WIDTH - IDX_DIM), w.dtype)], axis=1).astype(BF16)
    w_c = w[:, OFF_XR:].astype(BF16)
    g1 = norm1_g[0].reshape(1, d)
    ot, wit, okk = _attn_proj(x2, g1, wt_all, wt_wi, w_kk, 1024, 1280)
    pc = _norm_proj(x2, g1, w_c, F32, 1024, 1536)

    ks = np.arange(QB)[:, None]
    qs = np.arange(QB)[None, :]
    bkt = jnp.asarray(np.stack([_rel_bucket_np(qs - ks + QB * dd) for dd in range(2)]))
    attn = _dsa_attention(rel_bias, ot, wit, okk, bkt, batch, seq).reshape(tokens, ATTN_WIDTH)

    rnn = _rglru(pc, conv_w[0], conv_b[0].reshape(1, RNN_WIDTH), lru_wa[0].astype(BF16), lru_ba[0],
                 lru_wx[0].astype(BF16), lru_bx[0], lru_lambda[0].reshape(1, RNN_WIDTH), batch, seq)

    w_r = jnp.concatenate([w_group[0], w_expert_router[0],
                           jnp.zeros((d, LANES - N_GROUPS - N_EXPERTS), F32)], axis=1)
    w_rh = w_r.astype(BF16)
    w_rl = (w_r - w_rh.astype(F32)).astype(BF16)
    b_r = jnp.concatenate([b_group[0], b_expert_router[0],
                           jnp.zeros((LANES - N_GROUPS - N_EXPERTS,), F32)]).reshape(1, LANES)
    x1, u2, lgt = _merge(attn, rnn, pc, x2, w_proj_attn[0].astype(BF16), w_proj_rnn[0].astype(BF16),
                         w_out[0].astype(BF16), norm2_g[0].reshape(1, d), w_rh, w_rl, b_r)

    n_slots = tokens * TOP_K_IN_GROUP
    cap = -(-(n_slots + N_EXPERTS * (MOE_BLOCK - 1)) // MOE_BLOCK) * MOE_BLOCK
    dest, gw, meta = _route(lgt)

    dest_flat = dest.reshape(-1)
    ys = _experts(meta[0, :N_EXPERTS], meta[1, :N_EXPERTS], meta[2, :1], dest_flat, u2,
                  w_gate[0], w_up[0], w_down[0], cap)
    out = _combine(dest_flat, ys, x1, gw.T, norm_f_g.reshape(1, d))
    return out.reshape(batch, seq, d)
```
